```python
import jax, jax.numpy as jnp
from jax import lax
import numpy as np

D_MODEL = 2048
BATCH = 2
SEQ = 8192
DEPTH = 1

MEM_LEN = 256
HEAD_DIM = 128
D_MIX = D_MODEL
N_MOBA_HEADS = D_MIX // 2 // HEAD_DIM
MOBA_WIDTH = N_MOBA_HEADS * HEAD_DIM
N_MEM_HEADS = 4
MEM_WIDTH = N_MEM_HEADS * HEAD_DIM
POOL_WIDTH = D_MIX - MOBA_WIDTH - MEM_WIDTH
POOL_WINDOWS = (2, 4, 8, 16)
N_POOL_GROUPS = len(POOL_WINDOWS)
POOL_GROUP_DIM = POOL_WIDTH // N_POOL_GROUPS
MOBA_BLOCK = 256
MOBA_TOPK = 3
QUERY_CHUNK = 32
EPS = 1e-6
IN_SIZES = (MOBA_WIDTH, MOBA_WIDTH, MOBA_WIDTH, MOBA_WIDTH,
            POOL_WIDTH, POOL_WIDTH,
            MEM_WIDTH, MEM_WIDTH)
D_IN = sum(IN_SIZES)
IN_OFFSETS = [int(o) for o in np.cumsum(IN_SIZES)[:-1]]

kernel_name = "hybrid_moba_pool_memory_layer"


def rmsnorm(x, g):
    xf = x.astype(jnp.float32)
    y = xf * lax.rsqrt(jnp.mean(xf * xf, axis=-1, keepdims=True) + EPS)
    return (y * g.astype(jnp.float32)).astype(x.dtype)


def moba_attention(q, k, v):
    B, S, H, Dh = q.shape
    nb = -(-S // MOBA_BLOCK)
    s_pad = nb * MOBA_BLOCK
    pad = ((0, 0), (0, 0), (0, s_pad - S), (0, 0))
    q = jnp.pad(q.transpose(0, 2, 1, 3), pad)
    k = jnp.pad(k.transpose(0, 2, 1, 3), pad)
    v = jnp.pad(v.transpose(0, 2, 1, 3), pad)
    k_blk = k.reshape(B, H, nb, MOBA_BLOCK, Dh)
    v_blk = v.reshape(B, H, nb, MOBA_BLOCK, Dh)
    k_mean = jnp.mean(k_blk.astype(jnp.float32), axis=3)
    n_chunks = s_pad // QUERY_CHUNK
    q_chunks = q.reshape(B, H, n_chunks, QUERY_CHUNK, Dh).transpose(2, 0, 1, 3, 4)
    topk = min(MOBA_TOPK, nb - 1)
    scale = Dh ** -0.5
    b_ix = jnp.arange(B)[:, None, None, None]
    h_ix = jnp.arange(H)[None, :, None, None]
    blk_ids = jnp.arange(nb)

    def chunk(args):
        qc, c = args
        start = c * QUERY_CHUNK
        blk = start // MOBA_BLOCK
        qpos = start + jnp.arange(QUERY_CHUNK)
        kpos = blk * MOBA_BLOCK + jnp.arange(MOBA_BLOCK)
        k_own = lax.dynamic_index_in_dim(k_blk, blk, axis=2, keepdims=False)
        v_own = lax.dynamic_index_in_dim(v_blk, blk, axis=2, keepdims=False)
        s_own = jnp.einsum('bhqd,bhkd->bhqk', qc, k_own).astype(jnp.float32) * scale
        s_own = jnp.where(kpos[None, :] <= qpos[:, None], s_own, -jnp.inf)
        if topk == 0:
            p = jax.nn.softmax(s_own, axis=-1).astype(v.dtype)
            return jnp.einsum('bhqk,bhkd->bhqd', p, v_own)
        gate = jnp.einsum('bhqd,bhnd->bhqn', qc.astype(jnp.float32), k_mean)
        gate = jnp.where(blk_ids < blk, gate, -jnp.inf)
        _, idx = lax.top_k(gate, topk)
        valid = idx < blk
        k_sel = k_blk[b_ix, h_ix, idx]
        v_sel = v_blk[b_ix, h_ix, idx]
        s_sel = jnp.einsum('bhqd,bhqnkd->bhqnk', qc, k_sel).astype(jnp.float32) * scale
        s_sel = jnp.where(valid[..., None], s_sel, -jnp.inf)
        s_sel = s_sel.reshape(B, H, QUERY_CHUNK, topk * MOBA_BLOCK)
        p = jax.nn.softmax(jnp.concatenate([s_sel, s_own], axis=-1), axis=-1).astype(v.dtype)
        p_sel = p[..., :topk * MOBA_BLOCK].reshape(B, H, QUERY_CHUNK, topk, MOBA_BLOCK)
        p_own = p[..., topk * MOBA_BLOCK:]
        return (jnp.einsum('bhqnk,bhqnkd->bhqd', p_sel, v_sel)
                + jnp.einsum('bhqk,bhkd->bhqd', p_own, v_own))

    out = lax.map(chunk, (q_chunks, jnp.arange(n_chunks)))
    out = out.transpose(1, 0, 3, 2, 4).reshape(B, s_pad, H, Dh)
    return out[:, :S]


def multiscale_pool(u, w_pool, pool_scale):
    B, S, _ = u.shape
    uf = u.astype(jnp.float32)
    cs = jnp.cumsum(uf, axis=1)
    t = jnp.arange(S)
    outs = []
    for g, w in enumerate(POOL_WINDOWS):
        csg = cs[..., g * POOL_GROUP_DIM:(g + 1) * POOL_GROUP_DIM]
        prev = jnp.pad(csg, ((0, 0), (w, 0), (0, 0)))[:, :S]
        cnt = jnp.minimum(t + 1, w).astype(jnp.float32)[None, :, None]
        outs.append((csg - prev) / cnt)
    pooled = jnp.stack(outs, axis=2) - uf.reshape(B, S, N_POOL_GROUPS, POOL_GROUP_DIM)
    mixed = jnp.einsum('bsgc,gcd->bsgd', pooled.astype(u.dtype), w_pool)
    return mixed.reshape(B, S, POOL_WIDTH) * pool_scale


def memory_attention(q, mem_k, mem_v):
    s = jnp.einsum('bshd,bmhd->bhsm', q, mem_k).astype(jnp.float32) * (HEAD_DIM ** -0.5)
    p = jax.nn.softmax(s, axis=-1).astype(mem_v.dtype)
    return jnp.einsum('bhsm,bmhd->bshd', p, mem_v)


def hybrid_layer(x, mem, norm_g, mem_norm_g, w_in, w_mem_kv, w_pool, pool_scale, w_out):
    B, S, _ = x.shape
    h = rmsnorm(x, norm_g)
    z = h @ w_in
    q_a, k_a, v_a, g_a, u_p, g_p, q_m, g_m = jnp.split(z, IN_OFFSETS, axis=-1)
    hd = (B, S, N_MOBA_HEADS, HEAD_DIM)
    y_a = moba_attention(q_a.reshape(hd), k_a.reshape(hd), v_a.reshape(hd))
    y_a = y_a.reshape(B, S, MOBA_WIDTH) * jax.nn.silu(g_a)
    y_p = multiscale_pool(u_p, w_pool, pool_scale) * jax.nn.silu(g_p)
    mem_h = rmsnorm(mem, mem_norm_g)
    mk, mv = jnp.split(mem_h @ w_mem_kv, 2, axis=-1)
    M = mem.shape[1]
    y_m = memory_attention(q_m.reshape(B, S, N_MEM_HEADS, HEAD_DIM),
                           mk.reshape(B, M, N_MEM_HEADS, HEAD_DIM),
                           mv.reshape(B, M, N_MEM_HEADS, HEAD_DIM))
    y_m = y_m.reshape(B, S, MEM_WIDTH) * jax.nn.silu(g_m)
    y = jnp.concatenate([y_a, y_p, y_m], axis=-1) @ w_out
    return x + y


def setup_inputs(seed: int = 0) -> dict:
    key = jax.random.key(seed)
    ks = jax.random.split(key, 11)
    f32 = jnp.float32
    x = jax.random.normal(ks[0], (BATCH, SEQ, D_MODEL), f32)
    mem = jax.random.normal(ks[1], (BATCH, MEM_LEN, D_MODEL), f32)
    norm_g = 1.0 + 0.02 * jax.random.normal(ks[2], (DEPTH, D_MODEL), f32)
    mem_norm_g = 1.0 + 0.02 * jax.random.normal(ks[3], (DEPTH, D_MODEL), f32)
    w_in = jax.random.normal(ks[4], (DEPTH, D_MODEL, D_IN), f32) * D_MODEL ** -0.5
    w_mem_kv = jax.random.normal(ks[5], (DEPTH, D_MODEL, 2 * MEM_WIDTH), f32) * D_MODEL ** -0.5
    w_pool = jax.random.normal(ks[6], (DEPTH, N_POOL_GROUPS, POOL_GROUP_DIM, POOL_GROUP_DIM), f32) * POOL_GROUP_DIM ** -0.5
    pool_scale = 1.0 + 0.1 * jax.random.normal(ks[7], (DEPTH, POOL_WIDTH), f32)
    w_out = jax.random.normal(ks[8], (DEPTH, D_MIX, D_MODEL), f32) * D_MIX ** -0.5
    final_norm_g = 1.0 + 0.02 * jax.random.normal(ks[9], (D_MODEL,), f32)
    return {"x": x, "mem": mem, "norm_g": norm_g, "mem_norm_g": mem_norm_g,
            "w_in": w_in, "w_mem_kv": w_mem_kv, "w_pool": w_pool,
            "pool_scale": pool_scale, "w_out": w_out, "final_norm_g": final_norm_g}


def reference(x, mem, norm_g, mem_norm_g, w_in, w_mem_kv, w_pool, pool_scale, w_out, final_norm_g):
    for l in range(DEPTH):
        x = hybrid_layer(x, mem, norm_g[l], mem_norm_g[l], w_in[l], w_mem_kv[l],
                         w_pool[l], pool_scale[l], w_out[l])
    return rmsnorm(x, final_norm_g)
```

```python
import functools

import jax
import jax.numpy as jnp
from jax import lax
from jax.experimental import pallas as pl
from jax.experimental.pallas import tpu as pltpu

HEAD_DIM = 128
N_MOBA_HEADS = 8
MOBA_WIDTH = N_MOBA_HEADS * HEAD_DIM
N_MEM_HEADS = 4
MEM_WIDTH = N_MEM_HEADS * HEAD_DIM
POOL_WINDOWS = (2, 4, 8, 16)
POOL_GROUP_DIM = 128
POOL_WIDTH = len(POOL_WINDOWS) * POOL_GROUP_DIM
POOL_HALO = 16
MOBA_BLOCK = 256
MOBA_TOPK = 3
EPS = 1e-6
SCALE = HEAD_DIM ** -0.5

_Q0, _K0, _GA0 = 0, MOBA_WIDTH, 2 * MOBA_WIDTH
_REST0 = 3 * MOBA_WIDTH
_REST_W = 2 * POOL_WIDTH + 2 * MEM_WIDTH

V7X_VMEM_LIMIT_BYTES = 56 * 1024 * 1024

IN_PROJ_ROWS = 256
POOL_MEM_ROWS = 512
OUT_PROJ_ROWS = 512

_F32 = jnp.float32
_BF16 = jnp.bfloat16
_NEG_INF = float("-inf")


def _nt_dot(a, b):
    return lax.dot_general(a, b, (((1,), (1,)), ((), ())), preferred_element_type=_F32)


def _split_bf16(a):
    hi = a.astype(_BF16)
    lo = (a - hi.astype(_F32)).astype(_BF16)
    return hi, lo


def _nt_dot_f32(a, b):
    a_hi, a_lo = _split_bf16(a)
    b_hi, b_lo = _split_bf16(b)
    return _nt_dot(a_hi, b_hi) + (_nt_dot(a_hi, b_lo) + _nt_dot(a_lo, b_hi))


def _silu(a):
    return a * (1.0 / (1.0 + jnp.exp(-a)))


def _rmsnorm(x, g):
    ms = jnp.mean(x * x, axis=-1, keepdims=True)
    return x * lax.rsqrt(ms + EPS) * g


def _mem_kv_kernel(mem_ref, g_ref, w_ref, mk_ref, mv_ref):
    h = _rmsnorm(mem_ref[...], g_ref[...]).astype(_BF16)
    kv = jnp.dot(h, w_ref[...], preferred_element_type=_F32)
    mk_ref[...] = kv[:, :MEM_WIDTH].astype(_BF16)
    mv_ref[...] = kv[:, MEM_WIDTH:].astype(_BF16)


def _mem_kv(mem, g, w_bf16):
    B, M, D = mem.shape
    return pl.pallas_call(
        _mem_kv_kernel,
        grid=(B,),
        in_specs=[
            pl.BlockSpec((None, M, D), lambda b: (b, 0, 0)),
            pl.BlockSpec((1, D), lambda b: (0, 0)),
            pl.BlockSpec((D, 2 * MEM_WIDTH), lambda b: (0, 0)),
        ],
        out_specs=[
            pl.BlockSpec((None, M, MEM_WIDTH), lambda b: (b, 0, 0)),
            pl.BlockSpec((None, M, MEM_WIDTH), lambda b: (b, 0, 0)),
        ],
        out_shape=[jax.ShapeDtypeStruct((B, M, MEM_WIDTH), _BF16)] * 2,
        compiler_params=pltpu.CompilerParams(dimension_semantics=("arbitrary",)),
        name="mem_kv",
    )(mem, g, w_bf16)


def _in_proj_kernel(x_ref, g_ref, w_ref, wvt_ref,
                    q_ref, k_ref, vt_ref, sga_ref, rest_ref, bias_ref,
                    kmean_sc, gate_sc, *, tiles_per_batch, n_blocks):
    tm = x_ref.shape[0]
    blocks_per_tile = tm // MOBA_BLOCK
    t = pl.program_id(0)
    first_block = lax.rem(t, tiles_per_batch) * blocks_per_tile

    @pl.when(t == 0)
    def _():
        kmean_sc[...] = jnp.zeros_like(kmean_sc)

    h = _rmsnorm(x_ref[...], g_ref[...]).astype(_BF16)

    k = jnp.dot(h, w_ref[:, _K0:_K0 + MOBA_WIDTH], preferred_element_type=_F32)
    k_ref[...] = k.astype(_BF16)
    for b in range(blocks_per_tile):
        kmean_sc[pl.ds(first_block + b, 1), :] = jnp.mean(
            k[b * MOBA_BLOCK:(b + 1) * MOBA_BLOCK], axis=0, keepdims=True)

    q = jnp.dot(h, w_ref[:, _Q0:_Q0 + MOBA_WIDTH], preferred_element_type=_F32)
    q_ref[...] = (q * SCALE).astype(_BF16)

    n_idx = lax.broadcasted_iota(jnp.int32, (n_blocks, tm), 0)
    col = lax.broadcasted_iota(jnp.int32, (n_blocks, tm), 1)
    own_block = first_block + col // MOBA_BLOCK
    is_past = n_idx < own_block
    for hd in range(N_MOBA_HEADS):
        cols = slice(hd * HEAD_DIM, (hd + 1) * HEAD_DIM)
        gate = jnp.where(is_past, _nt_dot_f32(kmean_sc[:, cols], q[:, cols]), _NEG_INF)
        gate_sc[...] = gate
        rank = jnp.zeros((n_blocks, tm), jnp.int32)
        for n2 in range(n_blocks):
            other = gate_sc[pl.ds(n2, 1), :]
            beats = (other > gate) | ((other == gate) & (n_idx > n2))
            rank = rank + jnp.where(beats, 1, 0)
        keep = is_past & (rank < MOBA_TOPK)
        bias_ref[hd] = jnp.where(keep, 0.0, _NEG_INF)

    vt = _nt_dot(wvt_ref[...], h).astype(_BF16)
    for b in range(blocks_per_tile):
        vt_ref[b] = vt[:, b * MOBA_BLOCK:(b + 1) * MOBA_BLOCK]

    ga = jnp.dot(h, w_ref[:, _GA0:_GA0 + MOBA_WIDTH], preferred_element_type=_F32)
    sga_ref[...] = _silu(ga).astype(_BF16)

    rest = jnp.dot(h, w_ref[:, _REST0:_REST0 + _REST_W], preferred_element_type=_F32)
    u0, gp0, qm0, gm0 = 0, POOL_WIDTH, 2 * POOL_WIDTH, 2 * POOL_WIDTH + MEM_WIDTH
    rest_ref[:, u0:gp0] = rest[:, u0:gp0].astype(_BF16)
    rest_ref[:, gp0:qm0] = _silu(rest[:, gp0:qm0]).astype(_BF16)
    rest_ref[:, qm0:gm0] = (rest[:, qm0:gm0] * SCALE).astype(_BF16)
    rest_ref[:, gm0:] = _silu(rest[:, gm0:]).astype(_BF16)


def _in_proj(xf, g, w_nov, wvt, *, batch, seq):
    N, D = xf.shape
    tm = IN_PROJ_ROWS
    tiles_per_batch = seq // tm
    blocks_per_tile = tm // MOBA_BLOCK
    n_blocks = seq // MOBA_BLOCK
    resident = pl.Buffered(1)
    kern = functools.partial(_in_proj_kernel, tiles_per_batch=tiles_per_batch, n_blocks=n_blocks)
    return pl.pallas_call(
        kern,
        grid=(N // tm,),
        in_specs=[
            pl.BlockSpec((tm, D), lambda t: (t, 0)),
            pl.BlockSpec((1, D), lambda t: (0, 0)),
            pl.BlockSpec(w_nov.shape, lambda t: (0, 0), pipeline_mode=resident),
            pl.BlockSpec(wvt.shape, lambda t: (0, 0), pipeline_mode=resident),
        ],
        out_specs=[
            pl.BlockSpec((tm, MOBA_WIDTH), lambda t: (t, 0)),
            pl.BlockSpec((tm, MOBA_WIDTH), lambda t: (t, 0)),
            pl.BlockSpec((None, blocks_per_tile, MOBA_WIDTH, MOBA_BLOCK),
                         lambda t: (t // tiles_per_batch, t % tiles_per_batch, 0, 0)),
            pl.BlockSpec((tm, MOBA_WIDTH), lambda t: (t, 0)),
            pl.BlockSpec((tm, _REST_W), lambda t: (t, 0)),
            pl.BlockSpec((None, N_MOBA_HEADS, n_blocks, tm),
                         lambda t: (t // tiles_per_batch, 0, 0, t % tiles_per_batch)),
        ],
        out_shape=[
            jax.ShapeDtypeStruct((N, MOBA_WIDTH), _BF16),
            jax.ShapeDtypeStruct((N, MOBA_WIDTH), _BF16),
            jax.ShapeDtypeStruct((batch, n_blocks, MOBA_WIDTH, MOBA_BLOCK), _BF16),
            jax.ShapeDtypeStruct((N, MOBA_WIDTH), _BF16),
            jax.ShapeDtypeStruct((N, _REST_W), _BF16),
            jax.ShapeDtypeStruct((batch, N_MOBA_HEADS, n_blocks, seq), _F32),
        ],
        scratch_shapes=[
            pltpu.VMEM((n_blocks, MOBA_WIDTH), _F32),
            pltpu.VMEM((n_blocks, tm), _F32),
        ],
        compiler_params=pltpu.CompilerParams(
            dimension_semantics=("arbitrary",), vmem_limit_bytes=V7X_VMEM_LIMIT_BYTES),
        name="in_proj",
    )(xf, g, w_nov, wvt)


def _moba_kernel(q_ref, k_ref, vt_ref, bias_ref, sga_ref, o_ref):
    tq = q_ref.shape[0]
    i = pl.program_id(2)
    q = q_ref[...]

    def block(j):
        start = pl.multiple_of(j * MOBA_BLOCK, MOBA_BLOCK)
        return _nt_dot(k_ref[pl.ds(start, MOBA_BLOCK), :], q)

    kpos = lax.broadcasted_iota(jnp.int32, (MOBA_BLOCK, tq), 0)
    qpos = lax.broadcasted_iota(jnp.int32, (MOBA_BLOCK, tq), 1)
    s = jnp.where(kpos <= qpos, block(i), _NEG_INF)
    m = jnp.max(s, axis=0, keepdims=True)
    p = jnp.exp(s - m)
    l = jnp.sum(p, axis=0, keepdims=True)
    acc = jnp.dot(vt_ref[i], p.astype(_BF16), preferred_element_type=_F32)

    def body(j, carry):
        m, l, acc = carry
        s = block(j) + bias_ref[pl.ds(j, 1), :]
        m_new = jnp.maximum(m, jnp.max(s, axis=0, keepdims=True))
        alpha = jnp.exp(m - m_new)
        p = jnp.exp(s - m_new)
        l = alpha * l + jnp.sum(p, axis=0, keepdims=True)
        acc = alpha * acc + jnp.dot(vt_ref[j], p.astype(_BF16), preferred_element_type=_F32)
        return m_new, l, acc

    m, l, acc = lax.fori_loop(0, i, body, (m, l, acc))
    o = (acc / l).T
    o_ref[...] = (o * sga_ref[...].astype(_F32)).astype(_BF16)


def _moba(q, k, vt, bias, sga, *, batch, seq):
    N = q.shape[0]
    n_blocks = seq // MOBA_BLOCK
    tq = MOBA_BLOCK
    k3 = k.reshape(batch, seq, MOBA_WIDTH)
    row_block = lambda b, h, i: (b * n_blocks + i, h)
    return pl.pallas_call(
        _moba_kernel,
        grid=(batch, N_MOBA_HEADS, n_blocks),
        in_specs=[
            pl.BlockSpec((tq, HEAD_DIM), row_block),
            pl.BlockSpec((None, seq, HEAD_DIM), lambda b, h, i: (b, 0, h)),
            pl.BlockSpec((None, n_blocks, HEAD_DIM, MOBA_BLOCK), lambda b, h, i: (b, 0, h, 0)),
            pl.BlockSpec((None, None, n_blocks, tq), lambda b, h, i: (b, h, 0, i)),
            pl.BlockSpec((tq, HEAD_DIM), row_block),
        ],
        out_specs=pl.BlockSpec((tq, HEAD_DIM), row_block),
        out_shape=jax.ShapeDtypeStruct((N, MOBA_WIDTH), _BF16),
        compiler_params=pltpu.CompilerParams(
            dimension_semantics=("arbitrary", "arbitrary", "arbitrary")),
        name="moba",
    )(q, k3, vt, bias, sga)


def _pool_mem_kernel(rest_ref, halo_ref, mk_ref, mv_ref, wp_ref, ps_ref, o_ref, ext_sc,
                     *, tiles_per_batch):
    tm = rest_ref.shape[0]
    tb = lax.rem(pl.program_id(0), tiles_per_batch)
    u0, gp0, qm0, gm0 = 0, POOL_WIDTH, 2 * POOL_WIDTH, 2 * POOL_WIDTH + MEM_WIDTH

    u = rest_ref[:, u0:gp0].astype(_F32)
    ext_sc[0:POOL_HALO, :] = jnp.where(tb != 0, halo_ref[...].astype(_F32), 0.0)
    ext_sc[POOL_HALO:, :] = u
    pos = tb * tm + lax.broadcasted_iota(jnp.int32, (tm, POOL_GROUP_DIM), 0)
    for g, w in enumerate(POOL_WINDOWS):
        cols = slice(g * POOL_GROUP_DIM, (g + 1) * POOL_GROUP_DIM)
        u_g = u[:, cols]
        win = u_g
        for back in range(1, w):
            win = win + ext_sc[pl.ds(POOL_HALO - back, tm), cols]
        cnt = jnp.minimum(pos + 1, w).astype(_F32)
        pooled = win / cnt - u_g
        mixed = jnp.dot(pooled.astype(_BF16), wp_ref[g], preferred_element_type=_F32)
        gate = rest_ref[:, gp0 + g * POOL_GROUP_DIM:gp0 + (g + 1) * POOL_GROUP_DIM].astype(_F32)
        o_ref[:, cols] = (mixed * ps_ref[:, cols] * gate).astype(_BF16)

    for hd in range(N_MEM_HEADS):
        cols = slice(hd * HEAD_DIM, (hd + 1) * HEAD_DIM)
        qm = rest_ref[:, qm0 + hd * HEAD_DIM:qm0 + (hd + 1) * HEAD_DIM]
        s = _nt_dot(qm, mk_ref[:, cols])
        m = jnp.max(s, axis=-1, keepdims=True)
        e = jnp.exp(s - m)
        l = jnp.sum(e, axis=-1, keepdims=True)
        o = jnp.dot(e.astype(_BF16), mv_ref[:, cols], preferred_element_type=_F32) / l
        gate = rest_ref[:, gm0 + hd * HEAD_DIM:gm0 + (hd + 1) * HEAD_DIM].astype(_F32)
        o_ref[:, POOL_WIDTH + hd * HEAD_DIM:POOL_WIDTH + (hd + 1) * HEAD_DIM] = (o * gate).astype(_BF16)


def _pool_mem(rest, mk, mv, wp_bf16, pool_scale, *, seq):
    N = rest.shape[0]
    tm = POOL_MEM_ROWS
    tiles_per_batch = seq // tm
    halo_blocks_per_tile = tm // POOL_HALO
    M = mk.shape[1]
    kern = functools.partial(_pool_mem_kernel, tiles_per_batch=tiles_per_batch)
    return pl.pallas_call(
        kern,
        grid=(N // tm,),
        in_specs=[
            pl.BlockSpec((tm, _REST_W), lambda t: (t, 0)),
            pl.BlockSpec((POOL_HALO, POOL_WIDTH),
                         lambda t: (jnp.maximum(t * halo_blocks_per_tile - 1, 0), 0)),
            pl.BlockSpec((None, M, MEM_WIDTH), lambda t: (t // tiles_per_batch, 0, 0)),
            pl.BlockSpec((None, M, MEM_WIDTH), lambda t: (t // tiles_per_batch, 0, 0)),
            pl.BlockSpec(wp_bf16.shape, lambda t: (0, 0, 0)),
            pl.BlockSpec((1, POOL_WIDTH), lambda t: (0, 0)),
        ],
        out_specs=pl.BlockSpec((tm, POOL_WIDTH + MEM_WIDTH), lambda t: (t, 0)),
        out_shape=jax.ShapeDtypeStruct((N, POOL_WIDTH + MEM_WIDTH), _BF16),
        scratch_shapes=[pltpu.VMEM((tm + POOL_HALO, POOL_WIDTH), _F32)],
        compiler_params=pltpu.CompilerParams(dimension_semantics=("arbitrary",)),
        name="pool_mem",
    )(rest, rest, mk, mv, wp_bf16, pool_scale)


def _out_proj_kernel(ya_ref, ypm_ref, x_ref, w_ref, fg_ref, o_ref, *, final_norm):
    y = jnp.dot(ya_ref[...], w_ref[0:MOBA_WIDTH, :], preferred_element_type=_F32)
    y = y + jnp.dot(ypm_ref[...], w_ref[MOBA_WIDTH:, :], preferred_element_type=_F32)
    r = x_ref[...] + y
    o_ref[...] = _rmsnorm(r, fg_ref[...]) if final_norm else r


def _out_proj(ya, ypm, xf, w_bf16, final_g, *, final_norm):
    N, D = xf.shape
    tm = OUT_PROJ_ROWS
    kern = functools.partial(_out_proj_kernel, final_norm=final_norm)
    return pl.pallas_call(
        kern,
        grid=(N // tm,),
        in_specs=[
            pl.BlockSpec((tm, MOBA_WIDTH), lambda t: (t, 0)),
            pl.BlockSpec((tm, POOL_WIDTH + MEM_WIDTH), lambda t: (t, 0)),
            pl.BlockSpec((tm, D), lambda t: (t, 0)),
            pl.BlockSpec(w_bf16.shape, lambda t: (0, 0), pipeline_mode=pl.Buffered(1)),
            pl.BlockSpec((1, D), lambda t: (0, 0)),
        ],
        out_specs=pl.BlockSpec((tm, D), lambda t: (t, 0)),
        out_shape=jax.ShapeDtypeStruct((N, D), _F32),
        compiler_params=pltpu.CompilerParams(
            dimension_semantics=("arbitrary",), vmem_limit_bytes=V7X_VMEM_LIMIT_BYTES),
        name="out_proj",
    )(ya, ypm, xf, w_bf16, final_g)


def _layer(xf, mem, norm_g, mem_norm_g, w_in, w_mem_kv, w_pool, pool_scale, w_out, final_g,
           *, batch, seq, final_norm):
    v0 = 2 * MOBA_WIDTH
    w_nov = jnp.concatenate([w_in[:, :v0], w_in[:, v0 + MOBA_WIDTH:]], axis=1).astype(_BF16)
    wvt = w_in[:, v0:v0 + MOBA_WIDTH].T.astype(_BF16)

    mk, mv = _mem_kv(mem, mem_norm_g[None, :], w_mem_kv.astype(_BF16))
    q, k, vt, sga, rest, bias = _in_proj(xf, norm_g[None, :], w_nov, wvt, batch=batch, seq=seq)
    ya = _moba(q, k, vt, bias, sga, batch=batch, seq=seq)
    ypm = _pool_mem(rest, mk, mv, w_pool.astype(_BF16), pool_scale[None, :], seq=seq)
    return _out_proj(ya, ypm, xf, w_out.astype(_BF16), final_g[None, :], final_norm=final_norm)


def kernel(x, mem, norm_g, mem_norm_g, w_in, w_mem_kv, w_pool, pool_scale, w_out, final_norm_g):
    batch, seq, d_model = x.shape
    depth = norm_g.shape[0]
    assert seq % MOBA_BLOCK == 0 and seq % POOL_MEM_ROWS == 0 and seq % OUT_PROJ_ROWS == 0
    assert w_in.shape[2] == 4 * MOBA_WIDTH + 2 * POOL_WIDTH + 2 * MEM_WIDTH
    xf = x.reshape(batch * seq, d_model)
    for l in range(depth):
        xf = _layer(xf, mem, norm_g[l], mem_norm_g[l], w_in[l], w_mem_kv[l], w_pool[l],
                    pool_scale[l], w_out[l], final_norm_g,
                    batch=batch, seq=seq, final_norm=(l == depth - 1))
    return xf.reshape(batch, seq, d_model)
```

```python
import functools

import jax
import jax.numpy as jnp
from jax import lax
from jax.experimental import pallas as pl
from jax.experimental.pallas import tpu as pltpu

HEAD_DIM = 128
N_MOBA_HEADS = 8
MOBA_WIDTH = N_MOBA_HEADS * HEAD_DIM
N_MEM_HEADS = 4
MEM_WIDTH = N_MEM_HEADS * HEAD_DIM
POOL_WINDOWS = (2, 4, 8, 16)
POOL_GROUP_DIM = 128
POOL_WIDTH = len(POOL_WINDOWS) * POOL_GROUP_DIM
POOL_HALO = 16
MOBA_BLOCK = 256
MOBA_TOPK = 3
EPS = 1e-6
SCALE = HEAD_DIM ** -0.5
LOG2_E = 1.4426950408889634
MOBA_SUM_ROWS = 16

_Q0, _K0, _GA0 = 0, MOBA_WIDTH, 2 * MOBA_WIDTH
_REST0 = 3 * MOBA_WIDTH
_REST_W = 2 * POOL_WIDTH + 2 * MEM_WIDTH

V7X_VMEM_LIMIT_BYTES = 56 * 1024 * 1024

IN_PROJ_ROWS = 256
POOL_MEM_ROWS = 512
OUT_PROJ_ROWS = 512

_F32 = jnp.float32
_BF16 = jnp.bfloat16
_NEG_INF = float("-inf")


def _nt_dot(a, b):
    return lax.dot_general(a, b, (((1,), (1,)), ((), ())), preferred_element_type=_F32)


def _split_bf16(a):
    hi = a.astype(_BF16)
    lo = (a - hi.astype(_F32)).astype(_BF16)
    return hi, lo


def _nt_dot_f32(a, b):
    a_hi, a_lo = _split_bf16(a)
    b_hi, b_lo = _split_bf16(b)
    return _nt_dot(a_hi, b_hi) + (_nt_dot(a_hi, b_lo) + _nt_dot(a_lo, b_hi))


def _silu(a):
    return a * (1.0 / (1.0 + jnp.exp(-a)))


def _rmsnorm(x, g):
    ms = jnp.mean(x * x, axis=-1, keepdims=True)
    return x * lax.rsqrt(ms + EPS) * g


def _mem_kv_kernel(mem_ref, g_ref, w_ref, mk_ref, mv_ref):
    h = _rmsnorm(mem_ref[...], g_ref[...]).astype(_BF16)
    kv = jnp.dot(h, w_ref[...], preferred_element_type=_F32)
    mk_ref[...] = kv[:, :MEM_WIDTH].astype(_BF16)
    mv_ref[...] = kv[:, MEM_WIDTH:].astype(_BF16)


def _mem_kv(mem, g, w_bf16):
    B, M, D = mem.shape
    return pl.pallas_call(
        _mem_kv_kernel,
        grid=(B,),
        in_specs=[
            pl.BlockSpec((None, M, D), lambda b: (b, 0, 0)),
            pl.BlockSpec((1, D), lambda b: (0, 0)),
            pl.BlockSpec((D, 2 * MEM_WIDTH), lambda b: (0, 0)),
        ],
        out_specs=[
            pl.BlockSpec((None, M, MEM_WIDTH), lambda b: (b, 0, 0)),
            pl.BlockSpec((None, M, MEM_WIDTH), lambda b: (b, 0, 0)),
        ],
        out_shape=[jax.ShapeDtypeStruct((B, M, MEM_WIDTH), _BF16)] * 2,
        compiler_params=pltpu.CompilerParams(dimension_semantics=("arbitrary",)),
        name="mem_kv",
    )(mem, g, w_bf16)


def _in_proj_kernel(x_ref, g_ref, w_ref, wvt_ref,
                    q_ref, k_ref, vt_ref, sga_ref, rest_ref, bias_ref,
                    kmean_sc, gate_sc, *, tiles_per_batch, n_blocks):
    tm = x_ref.shape[0]
    blocks_per_tile = tm // MOBA_BLOCK
    t = pl.program_id(0)
    first_block = lax.rem(t, tiles_per_batch) * blocks_per_tile

    @pl.when(t == 0)
    def _():
        kmean_sc[...] = jnp.zeros_like(kmean_sc)

    h = _rmsnorm(x_ref[...], g_ref[...]).astype(_BF16)

    k = jnp.dot(h, w_ref[:, _K0:_K0 + MOBA_WIDTH], preferred_element_type=_F32)
    k_ref[...] = k.astype(_BF16)
    for b in range(blocks_per_tile):
        kmean_sc[pl.ds(first_block + b, 1), :] = jnp.mean(
            k[b * MOBA_BLOCK:(b + 1) * MOBA_BLOCK], axis=0, keepdims=True)

    q = jnp.dot(h, w_ref[:, _Q0:_Q0 + MOBA_WIDTH], preferred_element_type=_F32)
    q_ref[...] = (q * (SCALE * LOG2_E)).astype(_BF16)

    n_idx = lax.broadcasted_iota(jnp.int32, (n_blocks, tm), 0)
    col = lax.broadcasted_iota(jnp.int32, (n_blocks, tm), 1)
    own_block = first_block + col // MOBA_BLOCK
    is_past = n_idx < own_block
    for hd in range(N_MOBA_HEADS):
        cols = slice(hd * HEAD_DIM, (hd + 1) * HEAD_DIM)
        gate = jnp.where(is_past, _nt_dot_f32(kmean_sc[:, cols], q[:, cols]), _NEG_INF)
        gate_sc[...] = gate
        rank = jnp.zeros((n_blocks, tm), jnp.int32)
        for n2 in range(n_blocks):
            other = gate_sc[pl.ds(n2, 1), :]
            beats = (other > gate) | ((other == gate) & (n_idx > n2))
            rank = rank + jnp.where(beats, 1, 0)
        keep = (is_past & (rank < MOBA_TOPK)) | (n_idx == own_block)
        bias_ref[hd] = jnp.where(keep, 0.0, _NEG_INF)

    vt = _nt_dot(wvt_ref[...], h).astype(_BF16)
    for b in range(blocks_per_tile):
        vt_ref[b] = vt[:, b * MOBA_BLOCK:(b + 1) * MOBA_BLOCK]

    ga = jnp.dot(h, w_ref[:, _GA0:_GA0 + MOBA_WIDTH], preferred_element_type=_F32)
    sga_ref[...] = _silu(ga).astype(_BF16)

    rest = jnp.dot(h, w_ref[:, _REST0:_REST0 + _REST_W], preferred_element_type=_F32)
    u0, gp0, qm0, gm0 = 0, POOL_WIDTH, 2 * POOL_WIDTH, 2 * POOL_WIDTH + MEM_WIDTH
    rest_ref[:, u0:gp0] = rest[:, u0:gp0].astype(_BF16)
    rest_ref[:, gp0:qm0] = _silu(rest[:, gp0:qm0]).astype(_BF16)
    rest_ref[:, qm0:gm0] = (rest[:, qm0:gm0] * SCALE).astype(_BF16)
    rest_ref[:, gm0:] = _silu(rest[:, gm0:]).astype(_BF16)


def _in_proj(xf, g, w_nov, wvt, *, batch, seq):
    N, D = xf.shape
    tm = IN_PROJ_ROWS
    tiles_per_batch = seq // tm
    blocks_per_tile = tm // MOBA_BLOCK
    n_blocks = seq // MOBA_BLOCK
    resident = pl.Buffered(1)
    kern = functools.partial(_in_proj_kernel, tiles_per_batch=tiles_per_batch, n_blocks=n_blocks)
    return pl.pallas_call(
        kern,
        grid=(N // tm,),
        in_specs=[
            pl.BlockSpec((tm, D), lambda t: (t, 0)),
            pl.BlockSpec((1, D), lambda t: (0, 0)),
            pl.BlockSpec(w_nov.shape, lambda t: (0, 0), pipeline_mode=resident),
            pl.BlockSpec(wvt.shape, lambda t: (0, 0), pipeline_mode=resident),
        ],
        out_specs=[
            pl.BlockSpec((tm, MOBA_WIDTH), lambda t: (t, 0)),
            pl.BlockSpec((tm, MOBA_WIDTH), lambda t: (t, 0)),
            pl.BlockSpec((None, blocks_per_tile, MOBA_WIDTH, MOBA_BLOCK),
                         lambda t: (t // tiles_per_batch, t % tiles_per_batch, 0, 0)),
            pl.BlockSpec((tm, MOBA_WIDTH), lambda t: (t, 0)),
            pl.BlockSpec((tm, _REST_W), lambda t: (t, 0)),
            pl.BlockSpec((None, N_MOBA_HEADS, n_blocks, tm),
                         lambda t: (t // tiles_per_batch, 0, 0, t % tiles_per_batch)),
        ],
        out_shape=[
            jax.ShapeDtypeStruct((N, MOBA_WIDTH), _BF16),
            jax.ShapeDtypeStruct((N, MOBA_WIDTH), _BF16),
            jax.ShapeDtypeStruct((batch, n_blocks, MOBA_WIDTH, MOBA_BLOCK), _BF16),
            jax.ShapeDtypeStruct((N, MOBA_WIDTH), _BF16),
            jax.ShapeDtypeStruct((N, _REST_W), _BF16),
            jax.ShapeDtypeStruct((batch, N_MOBA_HEADS, n_blocks, seq), _F32),
        ],
        scratch_shapes=[
            pltpu.VMEM((n_blocks, MOBA_WIDTH), _F32),
            pltpu.VMEM((n_blocks, tm), _F32),
        ],
        compiler_params=pltpu.CompilerParams(
            dimension_semantics=("arbitrary",), vmem_limit_bytes=V7X_VMEM_LIMIT_BYTES),
        name="in_proj",
    )(xf, g, w_nov, wvt)


def _moba_kernel(q_ref, k_ref, vt_ref, bias_ref, sga_ref, o_ref, s_sc, m_sc, acc_sc):
    tq = q_ref.shape[0]
    i = pl.program_id(1)
    ones_rows = jnp.ones((MOBA_SUM_ROWS, MOBA_BLOCK), _BF16)

    def head_cols(hd):
        return slice(hd * HEAD_DIM, (hd + 1) * HEAD_DIM)

    def produce(j, slot, mask):
        start = pl.multiple_of(j * MOBA_BLOCK, MOBA_BLOCK)
        for hd in range(N_MOBA_HEADS):
            s = _nt_dot(k_ref[pl.ds(start, MOBA_BLOCK), head_cols(hd)],
                        q_ref[:, head_cols(hd)])
            s_sc[slot, hd] = mask(s)

    def consume(j, slot):
        for hd in range(N_MOBA_HEADS):
            row = pl.ds(hd, 1)
            s = s_sc[slot, hd]
            bias = bias_ref[hd, pl.ds(j, 1), :]
            m = m_sc[row, :]
            m_new = jnp.maximum(m, jnp.max(s, axis=0, keepdims=True) + bias)
            alpha = jnp.exp2(m - m_new)
            p = jnp.exp2(s - (m_new - bias))
            m_sc[row, :] = m_new
            v_and_ones = jnp.concatenate([vt_ref[j, head_cols(hd), :], ones_rows], axis=0)
            acc_sc[hd] = alpha * acc_sc[hd] + jnp.dot(
                v_and_ones, p.astype(_BF16), preferred_element_type=_F32)

    m_sc[...] = jnp.full_like(m_sc, _NEG_INF)
    acc_sc[...] = jnp.zeros_like(acc_sc)

    kpos = lax.broadcasted_iota(jnp.int32, (MOBA_BLOCK, tq), 0)
    qpos = lax.broadcasted_iota(jnp.int32, (MOBA_BLOCK, tq), 1)
    causal = kpos <= qpos
    produce(i, 0, lambda s: jnp.where(causal, s, _NEG_INF))

    def produce_past(t, slot):
        produce(t, slot, lambda s: s)

    def block_before(t):
        return jnp.where(t == 0, i, t - 1)

    def body(pair, carry):
        t = 2 * pair
        produce_past(t, 1)
        consume(block_before(t), 0)
        produce_past(t + 1, 0)
        consume(t, 1)
        return carry

    lax.fori_loop(0, i // 2, body, 0)

    @pl.when(i % 2 == 1)
    def _():
        produce_past(i - 1, 1)
        consume(block_before(i - 1), 0)
        consume(i - 1, 1)

    @pl.when(i % 2 == 0)
    def _():
        consume(block_before(i), 0)

    for hd in range(N_MOBA_HEADS):
        l = acc_sc[hd, HEAD_DIM:HEAD_DIM + 1, :]
        o = (acc_sc[hd, 0:HEAD_DIM, :] / l).T
        o_ref[:, head_cols(hd)] = (o * sga_ref[:, head_cols(hd)].astype(_F32)).astype(_BF16)


def _moba(q, k, vt, bias, sga, *, batch, seq):
    N = q.shape[0]
    n_blocks = seq // MOBA_BLOCK
    tq = MOBA_BLOCK
    k3 = k.reshape(batch, seq, MOBA_WIDTH)
    row_block = lambda b, i: (b * n_blocks + i, 0)
    per_batch = pl.Buffered(1)
    return pl.pallas_call(
        _moba_kernel,
        grid=(batch, n_blocks),
        in_specs=[
            pl.BlockSpec((tq, MOBA_WIDTH), row_block),
            pl.BlockSpec((None, seq, MOBA_WIDTH), lambda b, i: (b, 0, 0), pipeline_mode=per_batch),
            pl.BlockSpec((None, n_blocks, MOBA_WIDTH, MOBA_BLOCK), lambda b, i: (b, 0, 0, 0),
                         pipeline_mode=per_batch),
            pl.BlockSpec((None, N_MOBA_HEADS, n_blocks, tq), lambda b, i: (b, 0, 0, i)),
            pl.BlockSpec((tq, MOBA_WIDTH), row_block),
        ],
        out_specs=pl.BlockSpec((tq, MOBA_WIDTH), row_block),
        out_shape=jax.ShapeDtypeStruct((N, MOBA_WIDTH), _BF16),
        scratch_shapes=[
            pltpu.VMEM((2, N_MOBA_HEADS, MOBA_BLOCK, tq), _F32),
            pltpu.VMEM((N_MOBA_HEADS, tq), _F32),
            pltpu.VMEM((N_MOBA_HEADS, HEAD_DIM + MOBA_SUM_ROWS, tq), _F32),
        ],
        compiler_params=pltpu.CompilerParams(
            dimension_semantics=("arbitrary", "arbitrary"), vmem_limit_bytes=V7X_VMEM_LIMIT_BYTES),
        name="moba",
    )(q, k3, vt, bias, sga)


def _pool_mem_kernel(rest_ref, halo_ref, mk_ref, mv_ref, wp_ref, ps_ref, o_ref, ext_sc,
                     *, tiles_per_batch):
    tm = rest_ref.shape[0]
    tb = lax.rem(pl.program_id(0), tiles_per_batch)
    u0, gp0, qm0, gm0 = 0, POOL_WIDTH, 2 * POOL_WIDTH, 2 * POOL_WIDTH + MEM_WIDTH

    u = rest_ref[:, u0:gp0].astype(_F32)
    ext_sc[0:POOL_HALO, :] = jnp.where(tb != 0, halo_ref[...].astype(_F32), 0.0)
    ext_sc[POOL_HALO:, :] = u
    pos = tb * tm + lax.broadcasted_iota(jnp.int32, (tm, POOL_GROUP_DIM), 0)
    for g, w in enumerate(POOL_WINDOWS):
        cols = slice(g * POOL_GROUP_DIM, (g + 1) * POOL_GROUP_DIM)
        u_g = u[:, cols]
        win = u_g
        for back in range(1, w):
            win = win + ext_sc[pl.ds(POOL_HALO - back, tm), cols]
        cnt = jnp.minimum(pos + 1, w).astype(_F32)
        pooled = win / cnt - u_g
        mixed = jnp.dot(pooled.astype(_BF16), wp_ref[g], preferred_element_type=_F32)
        gate = rest_ref[:, gp0 + g * POOL_GROUP_DIM:gp0 + (g + 1) * POOL_GROUP_DIM].astype(_F32)
        o_ref[:, cols] = (mixed * ps_ref[:, cols] * gate).astype(_BF16)

    for hd in range(N_MEM_HEADS):
        cols = slice(hd * HEAD_DIM, (hd + 1) * HEAD_DIM)
        qm = rest_ref[:, qm0 + hd * HEAD_DIM:qm0 + (hd + 1) * HEAD_DIM]
        s = _nt_dot(qm, mk_ref[:, cols])
        m = jnp.max(s, axis=-1, keepdims=True)
        e = jnp.exp(s - m)
        l = jnp.sum(e, axis=-1, keepdims=True)
        o = jnp.dot(e.astype(_BF16), mv_ref[:, cols], preferred_element_type=_F32) / l
        gate = rest_ref[:, gm0 + hd * HEAD_DIM:gm0 + (hd + 1) * HEAD_DIM].astype(_F32)
        o_ref[:, POOL_WIDTH + hd * HEAD_DIM:POOL_WIDTH + (hd + 1) * HEAD_DIM] = (o * gate).astype(_BF16)


def _pool_mem(rest, mk, mv, wp_bf16, pool_scale, *, seq):
    N = rest.shape[0]
    tm = POOL_MEM_ROWS
    tiles_per_batch = seq // tm
    halo_blocks_per_tile = tm // POOL_HALO
    M = mk.shape[1]
    kern = functools.partial(_pool_mem_kernel, tiles_per_batch=tiles_per_batch)
    return pl.pallas_call(
        kern,
        grid=(N // tm,),
        in_specs=[
            pl.BlockSpec((tm, _REST_W), lambda t: (t, 0)),
            pl.BlockSpec((POOL_HALO, POOL_WIDTH),
                         lambda t: (jnp.maximum(t * halo_blocks_per_tile - 1, 0), 0)),
            pl.BlockSpec((None, M, MEM_WIDTH), lambda t: (t // tiles_per_batch, 0, 0)),
            pl.BlockSpec((None, M, MEM_WIDTH), lambda t: (t // tiles_per_batch, 0, 0)),
            pl.BlockSpec(wp_bf16.shape, lambda t: (0, 0, 0)),
            pl.BlockSpec((1, POOL_WIDTH), lambda t: (0, 0)),
        ],
        out_specs=pl.BlockSpec((tm, POOL_WIDTH + MEM_WIDTH), lambda t: (t, 0)),
        out_shape=jax.ShapeDtypeStruct((N, POOL_WIDTH + MEM_WIDTH), _BF16),
        scratch_shapes=[pltpu.VMEM((tm + POOL_HALO, POOL_WIDTH), _F32)],
        compiler_params=pltpu.CompilerParams(dimension_semantics=("arbitrary",)),
        name="pool_mem",
    )(rest, rest, mk, mv, wp_bf16, pool_scale)


def _out_proj_kernel(ya_ref, ypm_ref, x_ref, w_ref, fg_ref, o_ref, *, final_norm):
    y = jnp.dot(ya_ref[...], w_ref[0:MOBA_WIDTH, :], preferred_element_type=_F32)
    y = y + jnp.dot(ypm_ref[...], w_ref[MOBA_WIDTH:, :], preferred_element_type=_F32)
    r = x_ref[...] + y
    o_ref[...] = _rmsnorm(r, fg_ref[...]) if final_norm else r


def _out_proj(ya, ypm, xf, w_bf16, final_g, *, final_norm):
    N, D = xf.shape
    tm = OUT_PROJ_ROWS
    kern = functools.partial(_out_proj_kernel, final_norm=final_norm)
    return pl.pallas_call(
        kern,
        grid=(N // tm,),
        in_specs=[
            pl.BlockSpec((tm, MOBA_WIDTH), lambda t: (t, 0)),
            pl.BlockSpec((tm, POOL_WIDTH + MEM_WIDTH), lambda t: (t, 0)),
            pl.BlockSpec((tm, D), lambda t: (t, 0)),
            pl.BlockSpec(w_bf16.shape, lambda t: (0, 0), pipeline_mode=pl.Buffered(1)),
            pl.BlockSpec((1, D), lambda t: (0, 0)),
        ],
        out_specs=pl.BlockSpec((tm, D), lambda t: (t, 0)),
        out_shape=jax.ShapeDtypeStruct((N, D), _F32),
        compiler_params=pltpu.CompilerParams(
            dimension_semantics=("arbitrary",), vmem_limit_bytes=V7X_VMEM_LIMIT_BYTES),
        name="out_proj",
    )(ya, ypm, xf, w_bf16, final_g)


def _layer(xf, mem, norm_g, mem_norm_g, w_in, w_mem_kv, w_pool, pool_scale, w_out, final_g,
           *, batch, seq, final_norm):
    v0 = 2 * MOBA_WIDTH
    w_nov = jnp.concatenate([w_in[:, :v0], w_in[:, v0 + MOBA_WIDTH:]], axis=1).astype(_BF16)
    wvt = w_in[:, v0:v0 + MOBA_WIDTH].T.astype(_BF16)

    mk, mv = _mem_kv(mem, mem_norm_g[None, :], w_mem_kv.astype(_BF16))
    q, k, vt, sga, rest, bias = _in_proj(xf, norm_g[None, :], w_nov, wvt, batch=batch, seq=seq)
    ya = _moba(q, k, vt, bias, sga, batch=batch, seq=seq)
    ypm = _pool_mem(rest, mk, mv, w_pool.astype(_BF16), pool_scale[None, :], seq=seq)
    return _out_proj(ya, ypm, xf, w_out.astype(_BF16), final_g[None, :], final_norm=final_norm)


def kernel(x, mem, norm_g, mem_norm_g, w_in, w_mem_kv, w_pool, pool_scale, w_out, final_norm_g):
    batch, seq, d_model = x.shape
    depth = norm_g.shape[0]
    assert seq % MOBA_BLOCK == 0 and seq % POOL_MEM_ROWS == 0 and seq % OUT_PROJ_ROWS == 0
    assert w_in.shape[2] == 4 * MOBA_WIDTH + 2 * POOL_WIDTH + 2 * MEM_WIDTH
    xf = x.reshape(batch * seq, d_model)
    for l in range(depth):
        xf = _layer(xf, mem, norm_g[l], mem_norm_g[l], w_in[l], w_mem_kv[l], w_pool[l],
                    pool_scale[l], w_out[l], final_norm_g,
                    batch=batch, seq=seq, final_norm=(l == depth - 1))
    return xf.reshape(batch, seq, d_model)
```

```python
import functools

import jax
import jax.numpy as jnp
from jax import lax
from jax.experimental import pallas as pl
from jax.experimental.pallas import tpu as pltpu

HEAD_DIM = 128
N_MOBA_HEADS = 8
MOBA_WIDTH = N_MOBA_HEADS * HEAD_DIM
N_MEM_HEADS = 4
MEM_WIDTH = N_MEM_HEADS * HEAD_DIM
POOL_WINDOWS = (2, 4, 8, 16)
POOL_GROUP_DIM = 128
POOL_WIDTH = len(POOL_WINDOWS) * POOL_GROUP_DIM
POOL_HALO = 16
MOBA_BLOCK = 256
MOBA_TOPK = 3
EPS = 1e-6
SCALE = HEAD_DIM ** -0.5
LOG2_E = 1.4426950408889634
MOBA_SUM_ROWS = 16

_Q0, _K0, _V0, _GA0 = 0, MOBA_WIDTH, 2 * MOBA_WIDTH, 3 * MOBA_WIDTH
_REST0 = 4 * MOBA_WIDTH
_REST_W = 2 * POOL_WIDTH + 2 * MEM_WIDTH

V7X_VMEM_LIMIT_BYTES = 56 * 1024 * 1024

IN_PROJ_ROWS = 256
POOL_MEM_ROWS = 512
OUT_PROJ_ROWS = 512

_F32 = jnp.float32
_BF16 = jnp.bfloat16
_NEG_INF = float("-inf")


def _nt_dot(a, b):
    return lax.dot_general(a, b, (((1,), (1,)), ((), ())), preferred_element_type=_F32)


def _split_bf16(a):
    hi = a.astype(_BF16)
    lo = (a - hi.astype(_F32)).astype(_BF16)
    return hi, lo


def _nt_dot_f32(a, b):
    a_hi, a_lo = _split_bf16(a)
    b_hi, b_lo = _split_bf16(b)
    return _nt_dot(a_hi, b_hi) + (_nt_dot(a_hi, b_lo) + _nt_dot(a_lo, b_hi))


def _silu(a):
    return a * (1.0 / (1.0 + jnp.exp(-a)))


def _rmsnorm(x, g):
    ms = jnp.mean(x * x, axis=-1, keepdims=True)
    return x * lax.rsqrt(ms + EPS) * g


def _mem_kv_kernel(mem_ref, g_ref, w_ref, mk_ref, mv_ref):
    h = _rmsnorm(mem_ref[...], g_ref[...]).astype(_BF16)
    kv = jnp.dot(h, w_ref[...], preferred_element_type=_F32)
    mk_ref[...] = kv[:, :MEM_WIDTH].astype(_BF16)
    mv_ref[...] = kv[:, MEM_WIDTH:].astype(_BF16)


def _mem_kv(mem, g, w_bf16):
    B, M, D = mem.shape
    return pl.pallas_call(
        _mem_kv_kernel,
        grid=(B,),
        in_specs=[
            pl.BlockSpec((None, M, D), lambda b: (b, 0, 0)),
            pl.BlockSpec((1, D), lambda b: (0, 0)),
            pl.BlockSpec((D, 2 * MEM_WIDTH), lambda b: (0, 0)),
        ],
        out_specs=[
            pl.BlockSpec((None, M, MEM_WIDTH), lambda b: (b, 0, 0)),
            pl.BlockSpec((None, M, MEM_WIDTH), lambda b: (b, 0, 0)),
        ],
        out_shape=[jax.ShapeDtypeStruct((B, M, MEM_WIDTH), _BF16)] * 2,
        compiler_params=pltpu.CompilerParams(dimension_semantics=("arbitrary",)),
        name="mem_kv",
    )(mem, g, w_bf16)


def _in_proj_kernel(x_ref, g_ref, w_ref,
                    q_ref, k_ref, vt_ref, sga_ref, rest_ref, bias_ref,
                    kmean_sc, *, tiles_per_batch, n_blocks):
    tm = x_ref.shape[0]
    blocks_per_tile = tm // MOBA_BLOCK
    t = pl.program_id(0)
    first_block = lax.rem(t, tiles_per_batch) * blocks_per_tile

    @pl.when(t == 0)
    def _():
        kmean_sc[...] = jnp.zeros_like(kmean_sc)

    h = _rmsnorm(x_ref[...], g_ref[...]).astype(_BF16)

    k = jnp.dot(h, w_ref[:, _K0:_K0 + MOBA_WIDTH], preferred_element_type=_F32)
    k_ref[...] = k.astype(_BF16)
    for b in range(blocks_per_tile):
        kmean_sc[pl.ds(first_block + b, 1), :] = jnp.mean(
            k[b * MOBA_BLOCK:(b + 1) * MOBA_BLOCK], axis=0, keepdims=True)

    q = jnp.dot(h, w_ref[:, _Q0:_Q0 + MOBA_WIDTH], preferred_element_type=_F32)
    q_ref[...] = (q * (SCALE * LOG2_E)).astype(_BF16)

    n_idx = lax.broadcasted_iota(jnp.int32, (n_blocks, tm), 0)
    col = lax.broadcasted_iota(jnp.int32, (n_blocks, tm), 1)
    own_block = first_block + col // MOBA_BLOCK
    is_past = n_idx < own_block
    for hd in range(N_MOBA_HEADS):
        cols = slice(hd * HEAD_DIM, (hd + 1) * HEAD_DIM)
        gate = jnp.where(is_past, _nt_dot_f32(kmean_sc[:, cols], q[:, cols]), _NEG_INF)
        keep = n_idx == own_block
        for _ in range(MOBA_TOPK):
            best = jnp.max(gate, axis=0, keepdims=True)
            first = jnp.min(jnp.where(gate == best, n_idx, n_blocks), axis=0, keepdims=True)
            pick = n_idx == first
            keep = keep | (pick & is_past)
            gate = jnp.where(pick, _NEG_INF, gate)
        bias_ref[hd] = jnp.where(keep, 0.0, _NEG_INF)

    v = jnp.dot(h, w_ref[:, _V0:_V0 + MOBA_WIDTH], preferred_element_type=_F32)
    for b in range(blocks_per_tile):
        vt_ref[b] = v[b * MOBA_BLOCK:(b + 1) * MOBA_BLOCK].T.astype(_BF16)

    ga = jnp.dot(h, w_ref[:, _GA0:_GA0 + MOBA_WIDTH], preferred_element_type=_F32)
    sga_ref[...] = _silu(ga).astype(_BF16)

    rest = jnp.dot(h, w_ref[:, _REST0:_REST0 + _REST_W], preferred_element_type=_F32)
    u0, gp0, qm0, gm0 = 0, POOL_WIDTH, 2 * POOL_WIDTH, 2 * POOL_WIDTH + MEM_WIDTH
    rest_ref[:, u0:gp0] = rest[:, u0:gp0].astype(_BF16)
    rest_ref[:, gp0:qm0] = _silu(rest[:, gp0:qm0]).astype(_BF16)
    rest_ref[:, qm0:gm0] = (rest[:, qm0:gm0] * SCALE).astype(_BF16)
    rest_ref[:, gm0:] = _silu(rest[:, gm0:]).astype(_BF16)


def _in_proj(xf, g, w_bf16, *, batch, seq):
    N, D = xf.shape
    tm = IN_PROJ_ROWS
    tiles_per_batch = seq // tm
    blocks_per_tile = tm // MOBA_BLOCK
    n_blocks = seq // MOBA_BLOCK
    resident = pl.Buffered(1)
    kern = functools.partial(_in_proj_kernel, tiles_per_batch=tiles_per_batch, n_blocks=n_blocks)
    return pl.pallas_call(
        kern,
        grid=(N // tm,),
        in_specs=[
            pl.BlockSpec((tm, D), lambda t: (t, 0)),
            pl.BlockSpec((1, D), lambda t: (0, 0)),
            pl.BlockSpec(w_bf16.shape, lambda t: (0, 0), pipeline_mode=resident),
        ],
        out_specs=[
            pl.BlockSpec((tm, MOBA_WIDTH), lambda t: (t, 0)),
            pl.BlockSpec((tm, MOBA_WIDTH), lambda t: (t, 0)),
            pl.BlockSpec((None, blocks_per_tile, MOBA_WIDTH, MOBA_BLOCK),
                         lambda t: (t // tiles_per_batch, t % tiles_per_batch, 0, 0)),
            pl.BlockSpec((tm, MOBA_WIDTH), lambda t: (t, 0)),
            pl.BlockSpec((tm, _REST_W), lambda t: (t, 0)),
            pl.BlockSpec((None, N_MOBA_HEADS, n_blocks, tm),
                         lambda t: (t // tiles_per_batch, 0, 0, t % tiles_per_batch)),
        ],
        out_shape=[
            jax.ShapeDtypeStruct((N, MOBA_WIDTH), _BF16),
            jax.ShapeDtypeStruct((N, MOBA_WIDTH), _BF16),
            jax.ShapeDtypeStruct((batch, n_blocks, MOBA_WIDTH, MOBA_BLOCK), _BF16),
            jax.ShapeDtypeStruct((N, MOBA_WIDTH), _BF16),
            jax.ShapeDtypeStruct((N, _REST_W), _BF16),
            jax.ShapeDtypeStruct((batch, N_MOBA_HEADS, n_blocks, seq), _F32),
        ],
        scratch_shapes=[pltpu.VMEM((n_blocks, MOBA_WIDTH), _F32)],
        compiler_params=pltpu.CompilerParams(
            dimension_semantics=("arbitrary",), vmem_limit_bytes=V7X_VMEM_LIMIT_BYTES),
        name="in_proj",
    )(xf, g, w_bf16)


def _moba_kernel(q_ref, k_ref, vt_ref, bias_ref, sga_ref, o_ref, s_sc, m_sc, acc_sc):
    tq = q_ref.shape[0]
    i = pl.program_id(1)
    ones_rows = jnp.ones((MOBA_SUM_ROWS, MOBA_BLOCK), _BF16)

    def head_cols(hd):
        return slice(hd * HEAD_DIM, (hd + 1) * HEAD_DIM)

    def produce(j, slot, mask):
        start = pl.multiple_of(j * MOBA_BLOCK, MOBA_BLOCK)
        for hd in range(N_MOBA_HEADS):
            s = _nt_dot(k_ref[pl.ds(start, MOBA_BLOCK), head_cols(hd)],
                        q_ref[:, head_cols(hd)])
            s_sc[slot, hd] = mask(s)

    def consume(j, slot):
        for hd in range(N_MOBA_HEADS):
            row = pl.ds(hd, 1)
            s = s_sc[slot, hd]
            bias = bias_ref[hd, pl.ds(j, 1), :]
            m = m_sc[row, :]
            m_new = jnp.maximum(m, jnp.max(s, axis=0, keepdims=True) + bias)
            alpha = jnp.exp2(m - m_new)
            p = jnp.exp2(s - (m_new - bias))
            m_sc[row, :] = m_new
            v_and_ones = jnp.concatenate([vt_ref[j, head_cols(hd), :], ones_rows], axis=0)
            acc_sc[hd] = alpha * acc_sc[hd] + jnp.dot(
                v_and_ones, p.astype(_BF16), preferred_element_type=_F32)

    m_sc[...] = jnp.full_like(m_sc, _NEG_INF)
    acc_sc[...] = jnp.zeros_like(acc_sc)

    kpos = lax.broadcasted_iota(jnp.int32, (MOBA_BLOCK, tq), 0)
    qpos = lax.broadcasted_iota(jnp.int32, (MOBA_BLOCK, tq), 1)
    causal = kpos <= qpos
    produce(i, 0, lambda s: jnp.where(causal, s, _NEG_INF))

    def produce_past(t, slot):
        produce(t, slot, lambda s: s)

    def block_before(t):
        return jnp.where(t == 0, i, t - 1)

    def body(pair, carry):
        t = 2 * pair
        produce_past(t, 1)
        consume(block_before(t), 0)
        produce_past(t + 1, 0)
        consume(t, 1)
        return carry

    lax.fori_loop(0, i // 2, body, 0)

    @pl.when(i % 2 == 1)
    def _():
        produce_past(i - 1, 1)
        consume(block_before(i - 1), 0)
        consume(i - 1, 1)

    @pl.when(i % 2 == 0)
    def _():
        consume(block_before(i), 0)

    for hd in range(N_MOBA_HEADS):
        l = acc_sc[hd, HEAD_DIM:HEAD_DIM + 1, :]
        o = (acc_sc[hd, 0:HEAD_DIM, :] / l).T
        o_ref[:, head_cols(hd)] = (o * sga_ref[:, head_cols(hd)].astype(_F32)).astype(_BF16)


def _moba(q, k, vt, bias, sga, *, batch, seq):
    N = q.shape[0]
    n_blocks = seq // MOBA_BLOCK
    tq = MOBA_BLOCK
    k3 = k.reshape(batch, seq, MOBA_WIDTH)
    row_block = lambda b, i: (b * n_blocks + i, 0)
    per_batch = pl.Buffered(1)
    return pl.pallas_call(
        _moba_kernel,
        grid=(batch, n_blocks),
        in_specs=[
            pl.BlockSpec((tq, MOBA_WIDTH), row_block),
            pl.BlockSpec((None, seq, MOBA_WIDTH), lambda b, i: (b, 0, 0), pipeline_mode=per_batch),
            pl.BlockSpec((None, n_blocks, MOBA_WIDTH, MOBA_BLOCK), lambda b, i: (b, 0, 0, 0),
                         pipeline_mode=per_batch),
            pl.BlockSpec((None, N_MOBA_HEADS, n_blocks, tq), lambda b, i: (b, 0, 0, i)),
            pl.BlockSpec((tq, MOBA_WIDTH), row_block),
        ],
        out_specs=pl.BlockSpec((tq, MOBA_WIDTH), row_block),
        out_shape=jax.ShapeDtypeStruct((N, MOBA_WIDTH), _BF16),
        scratch_shapes=[
            pltpu.VMEM((2, N_MOBA_HEADS, MOBA_BLOCK, tq), _F32),
            pltpu.VMEM((N_MOBA_HEADS, tq), _F32),
            pltpu.VMEM((N_MOBA_HEADS, HEAD_DIM + MOBA_SUM_ROWS, tq), _F32),
        ],
        compiler_params=pltpu.CompilerParams(
            dimension_semantics=("arbitrary", "arbitrary"), vmem_limit_bytes=V7X_VMEM_LIMIT_BYTES),
        name="moba",
    )(q, k3, vt, bias, sga)


def _pool_mem_kernel(rest_ref, halo_ref, mk_ref, mv_ref, wp_ref, ps_ref, o_ref, ext_sc,
                     *, tiles_per_batch):
    tm = rest_ref.shape[0]
    tb = lax.rem(pl.program_id(0), tiles_per_batch)
    u0, gp0, qm0, gm0 = 0, POOL_WIDTH, 2 * POOL_WIDTH, 2 * POOL_WIDTH + MEM_WIDTH

    u = rest_ref[:, u0:gp0].astype(_F32)
    ext_sc[0:POOL_HALO, :] = jnp.where(tb != 0, halo_ref[...].astype(_F32), 0.0)
    ext_sc[POOL_HALO:, :] = u
    pos = tb * tm + lax.broadcasted_iota(jnp.int32, (tm, POOL_GROUP_DIM), 0)
    for g, w in enumerate(POOL_WINDOWS):
        cols = slice(g * POOL_GROUP_DIM, (g + 1) * POOL_GROUP_DIM)
        u_g = u[:, cols]
        win = u_g
        for back in range(1, w):
            win = win + ext_sc[pl.ds(POOL_HALO - back, tm), cols]
        cnt = jnp.minimum(pos + 1, w).astype(_F32)
        pooled = win / cnt - u_g
        mixed = jnp.dot(pooled.astype(_BF16), wp_ref[g], preferred_element_type=_F32)
        gate = rest_ref[:, gp0 + g * POOL_GROUP_DIM:gp0 + (g + 1) * POOL_GROUP_DIM].astype(_F32)
        o_ref[:, cols] = (mixed * ps_ref[:, cols] * gate).astype(_BF16)

    for hd in range(N_MEM_HEADS):
        cols = slice(hd * HEAD_DIM, (hd + 1) * HEAD_DIM)
        qm = rest_ref[:, qm0 + hd * HEAD_DIM:qm0 + (hd + 1) * HEAD_DIM]
        s = _nt_dot(qm, mk_ref[:, cols])
        m = jnp.max(s, axis=-1, keepdims=True)
        e = jnp.exp(s - m)
        l = jnp.sum(e, axis=-1, keepdims=True)
        o = jnp.dot(e.astype(_BF16), mv_ref[:, cols], preferred_element_type=_F32) / l
        gate = rest_ref[:, gm0 + hd * HEAD_DIM:gm0 + (hd + 1) * HEAD_DIM].astype(_F32)
        o_ref[:, POOL_WIDTH + hd * HEAD_DIM:POOL_WIDTH + (hd + 1) * HEAD_DIM] = (o * gate).astype(_BF16)


def _pool_mem(rest, mk, mv, wp_bf16, pool_scale, *, seq):
    N = rest.shape[0]
    tm = POOL_MEM_ROWS
    tiles_per_batch = seq // tm
    halo_blocks_per_tile = tm // POOL_HALO
    M = mk.shape[1]
    kern = functools.partial(_pool_mem_kernel, tiles_per_batch=tiles_per_batch)
    return pl.pallas_call(
        kern,
        grid=(N // tm,),
        in_specs=[
            pl.BlockSpec((tm, _REST_W), lambda t: (t, 0)),
            pl.BlockSpec((POOL_HALO, POOL_WIDTH),
                         lambda t: (jnp.maximum(t * halo_blocks_per_tile - 1, 0), 0)),
            pl.BlockSpec((None, M, MEM_WIDTH), lambda t: (t // tiles_per_batch, 0, 0)),
            pl.BlockSpec((None, M, MEM_WIDTH), lambda t: (t // tiles_per_batch, 0, 0)),
            pl.BlockSpec(wp_bf16.shape, lambda t: (0, 0, 0)),
            pl.BlockSpec((1, POOL_WIDTH), lambda t: (0, 0)),
        ],
        out_specs=pl.BlockSpec((tm, POOL_WIDTH + MEM_WIDTH), lambda t: (t, 0)),
        out_shape=jax.ShapeDtypeStruct((N, POOL_WIDTH + MEM_WIDTH), _BF16),
        scratch_shapes=[pltpu.VMEM((tm + POOL_HALO, POOL_WIDTH), _F32)],
        compiler_params=pltpu.CompilerParams(dimension_semantics=("arbitrary",)),
        name="pool_mem",
    )(rest, rest, mk, mv, wp_bf16, pool_scale)


def _out_proj_kernel(ya_ref, ypm_ref, x_ref, w_ref, fg_ref, o_ref, *, final_norm):
    y = jnp.dot(ya_ref[...], w_ref[0:MOBA_WIDTH, :], preferred_element_type=_F32)
    y = y + jnp.dot(ypm_ref[...], w_ref[MOBA_WIDTH:, :], preferred_element_type=_F32)
    r = x_ref[...] + y
    o_ref[...] = _rmsnorm(r, fg_ref[...]) if final_norm else r


def _out_proj(ya, ypm, xf, w_bf16, final_g, *, final_norm):
    N, D = xf.shape
    tm = OUT_PROJ_ROWS
    kern = functools.partial(_out_proj_kernel, final_norm=final_norm)
    return pl.pallas_call(
        kern,
        grid=(N // tm,),
        in_specs=[
            pl.BlockSpec((tm, MOBA_WIDTH), lambda t: (t, 0)),
            pl.BlockSpec((tm, POOL_WIDTH + MEM_WIDTH), lambda t: (t, 0)),
            pl.BlockSpec((tm, D), lambda t: (t, 0)),
            pl.BlockSpec(w_bf16.shape, lambda t: (0, 0), pipeline_mode=pl.Buffered(1)),
            pl.BlockSpec((1, D), lambda t: (0, 0)),
        ],
        out_specs=pl.BlockSpec((tm, D), lambda t: (t, 0)),
        out_shape=jax.ShapeDtypeStruct((N, D), _F32),
        compiler_params=pltpu.CompilerParams(
            dimension_semantics=("arbitrary",), vmem_limit_bytes=V7X_VMEM_LIMIT_BYTES),
        name="out_proj",
    )(ya, ypm, xf, w_bf16, final_g)


def _layer(xf, mem, norm_g, mem_norm_g, w_in, w_mem_kv, w_pool, pool_scale, w_out, final_g,
           *, batch, seq, final_norm):
    mk, mv = _mem_kv(mem, mem_norm_g[None, :], w_mem_kv.astype(_BF16))
    q, k, vt, sga, rest, bias = _in_proj(xf, norm_g[None, :], w_in.astype(_BF16), batch=batch, seq=seq)
    ya = _moba(q, k, vt, bias, sga, batch=batch, seq=seq)
    ypm = _pool_mem(rest, mk, mv, w_pool.astype(_BF16), pool_scale[None, :], seq=seq)
    return _out_proj(ya, ypm, xf, w_out.astype(_BF16), final_g[None, :], final_norm=final_norm)


def kernel(x, mem, norm_g, mem_norm_g, w_in, w_mem_kv, w_pool, pool_scale, w_out, final_norm_g):
    batch, seq, d_model = x.shape
    depth = norm_g.shape[0]
    assert seq % MOBA_BLOCK == 0 and seq % POOL_MEM_ROWS == 0 and seq % OUT_PROJ_ROWS == 0
    assert w_in.shape[2] == 4 * MOBA_WIDTH + 2 * POOL_WIDTH + 2 * MEM_WIDTH
    xf = x.reshape(batch * seq, d_model)
    for l in range(depth):
        xf = _layer(xf, mem, norm_g[l], mem_norm_g[l], w_in[l], w_mem_kv[l], w_pool[l],
                    pool_scale[l], w_out[l], final_norm_g,
                    batch=batch, seq=seq, final_norm=(l == depth - 1))
    return xf.reshape(batch, seq, d_model)
```

```python
import functools

import jax
import jax.numpy as jnp
from jax import lax
from jax.experimental import pallas as pl
from jax.experimental.pallas import tpu as pltpu

HEAD_DIM = 128
N_MOBA_HEADS = 8
MOBA_WIDTH = N_MOBA_HEADS * HEAD_DIM
N_MEM_HEADS = 4
MEM_WIDTH = N_MEM_HEADS * HEAD_DIM
POOL_WINDOWS = (2, 4, 8, 16)
POOL_GROUP_DIM = 128
POOL_WIDTH = len(POOL_WINDOWS) * POOL_GROUP_DIM
POOL_HALO = 16
MOBA_BLOCK = 256
MOBA_TOPK = 3
EPS = 1e-6
SCALE = HEAD_DIM ** -0.5
LOG2_E = 1.4426950408889634
MOBA_SUM_ROWS = 16

_Q0, _K0, _V0, _GA0 = 0, MOBA_WIDTH, 2 * MOBA_WIDTH, 3 * MOBA_WIDTH
_REST0 = 4 * MOBA_WIDTH
_REST_W = 2 * POOL_WIDTH + 2 * MEM_WIDTH

V7X_VMEM_LIMIT_BYTES = 56 * 1024 * 1024

IN_PROJ_ROWS = 256
POOL_MEM_ROWS = 512
OUT_PROJ_ROWS = 512

_F32 = jnp.float32
_BF16 = jnp.bfloat16
_NEG_INF = float("-inf")


def _nt_dot(a, b):
    return lax.dot_general(a, b, (((1,), (1,)), ((), ())), preferred_element_type=_F32)


def _split_bf16(a):
    hi = a.astype(_BF16)
    lo = (a - hi.astype(_F32)).astype(_BF16)
    return hi, lo


def _nt_dot_f32(a, b):
    a_hi, a_lo = _split_bf16(a)
    b_hi, b_lo = _split_bf16(b)
    return _nt_dot(a_hi, b_hi) + (_nt_dot(a_hi, b_lo) + _nt_dot(a_lo, b_hi))


def _silu(a):
    return a * (1.0 / (1.0 + jnp.exp(-a)))


def _rmsnorm(x, g):
    ms = jnp.mean(x * x, axis=-1, keepdims=True)
    return x * lax.rsqrt(ms + EPS) * g


def _mem_kv_kernel(mem_ref, g_ref, w_ref, mk_ref, mv_ref):
    h = _rmsnorm(mem_ref[...], g_ref[...]).astype(_BF16)
    kv = jnp.dot(h, w_ref[...], preferred_element_type=_F32)
    mk_ref[...] = kv[:, :MEM_WIDTH].astype(_BF16)
    mv_ref[...] = kv[:, MEM_WIDTH:].astype(_BF16)


def _mem_kv(mem, g, w_bf16):
    B, M, D = mem.shape
    return pl.pallas_call(
        _mem_kv_kernel,
        grid=(B,),
        in_specs=[
            pl.BlockSpec((None, M, D), lambda b: (b, 0, 0)),
            pl.BlockSpec((1, D), lambda b: (0, 0)),
            pl.BlockSpec((D, 2 * MEM_WIDTH), lambda b: (0, 0)),
        ],
        out_specs=[
            pl.BlockSpec((None, M, MEM_WIDTH), lambda b: (b, 0, 0)),
            pl.BlockSpec((None, M, MEM_WIDTH), lambda b: (b, 0, 0)),
        ],
        out_shape=[jax.ShapeDtypeStruct((B, M, MEM_WIDTH), _BF16)] * 2,
        compiler_params=pltpu.CompilerParams(dimension_semantics=("arbitrary",)),
        name="mem_kv",
    )(mem, g, w_bf16)


def _in_proj_kernel(x_ref, g_ref, w_ref,
                    q_ref, k_ref, vt_ref, sga_ref, rest_ref, bias_ref,
                    kmean_sc, *, tiles_per_batch, n_blocks):
    tm = x_ref.shape[0]
    blocks_per_tile = tm // MOBA_BLOCK
    t = pl.program_id(0)
    first_block = lax.rem(t, tiles_per_batch) * blocks_per_tile

    @pl.when(t == 0)
    def _():
        kmean_sc[...] = jnp.zeros_like(kmean_sc)

    h = _rmsnorm(x_ref[...], g_ref[...]).astype(_BF16)

    k = jnp.dot(h, w_ref[:, _K0:_K0 + MOBA_WIDTH], preferred_element_type=_F32)
    k_ref[...] = k.astype(_BF16)
    for b in range(blocks_per_tile):
        kmean_sc[pl.ds(first_block + b, 1), :] = jnp.mean(
            k[b * MOBA_BLOCK:(b + 1) * MOBA_BLOCK], axis=0, keepdims=True)

    q = jnp.dot(h, w_ref[:, _Q0:_Q0 + MOBA_WIDTH], preferred_element_type=_F32)
    q_ref[...] = (q * (SCALE * LOG2_E)).astype(_BF16)

    n_idx = lax.broadcasted_iota(jnp.int32, (n_blocks, tm), 0)
    col = lax.broadcasted_iota(jnp.int32, (n_blocks, tm), 1)
    own_block = first_block + col // MOBA_BLOCK
    is_past = n_idx < own_block
    for hd in range(N_MOBA_HEADS):
        cols = slice(hd * HEAD_DIM, (hd + 1) * HEAD_DIM)
        gate = jnp.where(is_past, _nt_dot_f32(kmean_sc[:, cols], q[:, cols]), _NEG_INF)
        keep = None
        for _ in range(MOBA_TOPK):
            best = jnp.max(gate, axis=0, keepdims=True)
            first = jnp.min(jnp.where(gate == best, n_idx, n_blocks), axis=0, keepdims=True)
            pick = n_idx == first
            keep = pick if keep is None else keep | pick
            gate = jnp.where(pick, _NEG_INF, gate)
        bias_ref[hd] = jnp.where(keep & is_past, 0.0, _NEG_INF)

    v = jnp.dot(h, w_ref[:, _V0:_V0 + MOBA_WIDTH], preferred_element_type=_F32)
    for b in range(blocks_per_tile):
        vt_ref[b] = v[b * MOBA_BLOCK:(b + 1) * MOBA_BLOCK].T.astype(_BF16)

    ga = jnp.dot(h, w_ref[:, _GA0:_GA0 + MOBA_WIDTH], preferred_element_type=_F32)
    sga_ref[...] = _silu(ga).astype(_BF16)

    rest = jnp.dot(h, w_ref[:, _REST0:_REST0 + _REST_W], preferred_element_type=_F32)
    u0, gp0, qm0, gm0 = 0, POOL_WIDTH, 2 * POOL_WIDTH, 2 * POOL_WIDTH + MEM_WIDTH
    rest_ref[:, u0:gp0] = rest[:, u0:gp0].astype(_BF16)
    rest_ref[:, gp0:qm0] = _silu(rest[:, gp0:qm0]).astype(_BF16)
    rest_ref[:, qm0:gm0] = (rest[:, qm0:gm0] * SCALE).astype(_BF16)
    rest_ref[:, gm0:] = _silu(rest[:, gm0:]).astype(_BF16)


def _in_proj(xf, g, w_bf16, *, batch, seq):
    N, D = xf.shape
    tm = IN_PROJ_ROWS
    tiles_per_batch = seq // tm
    blocks_per_tile = tm // MOBA_BLOCK
    n_blocks = seq // MOBA_BLOCK
    resident = pl.Buffered(1)
    kern = functools.partial(_in_proj_kernel, tiles_per_batch=tiles_per_batch, n_blocks=n_blocks)
    return pl.pallas_call(
        kern,
        grid=(N // tm,),
        in_specs=[
            pl.BlockSpec((tm, D), lambda t: (t, 0)),
            pl.BlockSpec((1, D), lambda t: (0, 0)),
            pl.BlockSpec(w_bf16.shape, lambda t: (0, 0), pipeline_mode=resident),
        ],
        out_specs=[
            pl.BlockSpec((tm, MOBA_WIDTH), lambda t: (t, 0)),
            pl.BlockSpec((tm, MOBA_WIDTH), lambda t: (t, 0)),
            pl.BlockSpec((None, blocks_per_tile, MOBA_WIDTH, MOBA_BLOCK),
                         lambda t: (t // tiles_per_batch, t % tiles_per_batch, 0, 0)),
            pl.BlockSpec((tm, MOBA_WIDTH), lambda t: (t, 0)),
            pl.BlockSpec((tm, _REST_W), lambda t: (t, 0)),
            pl.BlockSpec((None, N_MOBA_HEADS, n_blocks, tm),
                         lambda t: (t // tiles_per_batch, 0, 0, t % tiles_per_batch)),
        ],
        out_shape=[
            jax.ShapeDtypeStruct((N, MOBA_WIDTH), _BF16),
            jax.ShapeDtypeStruct((N, MOBA_WIDTH), _BF16),
            jax.ShapeDtypeStruct((batch, n_blocks, MOBA_WIDTH, MOBA_BLOCK), _BF16),
            jax.ShapeDtypeStruct((N, MOBA_WIDTH), _BF16),
            jax.ShapeDtypeStruct((N, _REST_W), _BF16),
            jax.ShapeDtypeStruct((batch, N_MOBA_HEADS, n_blocks, seq), _F32),
        ],
        scratch_shapes=[pltpu.VMEM((n_blocks, MOBA_WIDTH), _F32)],
        compiler_params=pltpu.CompilerParams(
            dimension_semantics=("arbitrary",), vmem_limit_bytes=V7X_VMEM_LIMIT_BYTES),
        name="in_proj",
    )(xf, g, w_bf16)


def _moba_kernel(q_ref, k_ref, vt_ref, bias_ref, sga_ref, o_ref, s_sc, smax_sc, m_sc, acc_sc):
    tq = q_ref.shape[0]
    i = pl.program_id(1)
    ones_rows = jnp.ones((MOBA_SUM_ROWS, MOBA_BLOCK), _BF16)
    zero_bias = jnp.zeros((1, tq), _F32)

    def head_cols(hd):
        return slice(hd * HEAD_DIM, (hd + 1) * HEAD_DIM)

    def produce(j, slot, mask):
        start = pl.multiple_of(j * MOBA_BLOCK, MOBA_BLOCK)
        for hd in range(N_MOBA_HEADS):
            s = mask(_nt_dot(k_ref[pl.ds(start, MOBA_BLOCK), head_cols(hd)],
                             q_ref[:, head_cols(hd)]))
            s_sc[slot, hd] = s
            smax_sc[slot, pl.ds(hd, 1), :] = jnp.max(s, axis=0, keepdims=True)

    def consume(blocks):
        for hd in range(N_MOBA_HEADS):
            row = pl.ds(hd, 1)
            biases = [zero_bias if own else bias_ref[hd, pl.ds(j, 1), :]
                      for j, _, own in blocks]
            m = m_sc[row, :]
            m_new = m
            for (_, slot, _), bias in zip(blocks, biases):
                m_new = jnp.maximum(m_new, smax_sc[slot, row, :] + bias)
            alpha = jnp.exp2(m - m_new)
            m_sc[row, :] = m_new
            pv = None
            for (j, slot, _), bias in zip(blocks, biases):
                p = jnp.exp2(s_sc[slot, hd] - (m_new - bias))
                v_and_ones = jnp.concatenate([vt_ref[j, head_cols(hd), :], ones_rows], axis=0)
                d = jnp.dot(v_and_ones, p.astype(_BF16), preferred_element_type=_F32)
                pv = d if pv is None else pv + d
            acc_sc[hd] = alpha * acc_sc[hd] + pv

    m_sc[...] = jnp.full_like(m_sc, _NEG_INF)
    acc_sc[...] = jnp.zeros_like(acc_sc)

    n_pairs = jnp.maximum((i + 1) // 2, 1)

    def produce_pair(pair, slot_set):
        produce(2 * pair, 2 * slot_set, lambda s: s)
        produce(2 * pair + 1, 2 * slot_set + 1, lambda s: s)

    def consume_pair(pair, slot_set):
        consume([(2 * pair, 2 * slot_set, False), (2 * pair + 1, 2 * slot_set + 1, False)])

    kpos = lax.broadcasted_iota(jnp.int32, (MOBA_BLOCK, tq), 0)
    qpos = lax.broadcasted_iota(jnp.int32, (MOBA_BLOCK, tq), 1)
    causal = kpos <= qpos
    produce(i, 4, lambda s: jnp.where(causal, s, _NEG_INF))
    produce_pair(0, 0)
    consume([(i, 4, True)])

    def body(it, carry):
        produce_pair(2 * it + 1, 1)
        consume_pair(2 * it, 0)
        produce_pair(2 * it + 2, 0)
        consume_pair(2 * it + 1, 1)
        return carry

    n_full = (n_pairs - 1) // 2
    lax.fori_loop(0, n_full, body, 0)

    @pl.when(n_pairs - 2 * n_full == 2)
    def _():
        produce_pair(2 * n_full + 1, 1)
        consume_pair(2 * n_full, 0)
        consume_pair(2 * n_full + 1, 1)

    @pl.when(n_pairs - 2 * n_full == 1)
    def _():
        consume_pair(2 * n_full, 0)

    for hd in range(N_MOBA_HEADS):
        l = acc_sc[hd, HEAD_DIM:HEAD_DIM + 1, :]
        o = (acc_sc[hd, 0:HEAD_DIM, :] / l).T
        o_ref[:, head_cols(hd)] = (o * sga_ref[:, head_cols(hd)].astype(_F32)).astype(_BF16)


def _moba(q, k, vt, bias, sga, *, batch, seq):
    N = q.shape[0]
    n_blocks = seq // MOBA_BLOCK
    tq = MOBA_BLOCK
    k3 = k.reshape(batch, seq, MOBA_WIDTH)
    row_block = lambda b, i: (b * n_blocks + i, 0)
    per_batch = pl.Buffered(1)
    return pl.pallas_call(
        _moba_kernel,
        grid=(batch, n_blocks),
        in_specs=[
            pl.BlockSpec((tq, MOBA_WIDTH), row_block),
            pl.BlockSpec((None, seq, MOBA_WIDTH), lambda b, i: (b, 0, 0), pipeline_mode=per_batch),
            pl.BlockSpec((None, n_blocks, MOBA_WIDTH, MOBA_BLOCK), lambda b, i: (b, 0, 0, 0),
                         pipeline_mode=per_batch),
            pl.BlockSpec((None, N_MOBA_HEADS, n_blocks, tq), lambda b, i: (b, 0, 0, i)),
            pl.BlockSpec((tq, MOBA_WIDTH), row_block),
        ],
        out_specs=pl.BlockSpec((tq, MOBA_WIDTH), row_block),
        out_shape=jax.ShapeDtypeStruct((N, MOBA_WIDTH), _BF16),
        scratch_shapes=[
            pltpu.VMEM((5, N_MOBA_HEADS, MOBA_BLOCK, tq), _F32),
            pltpu.VMEM((5, N_MOBA_HEADS, tq), _F32),
            pltpu.VMEM((N_MOBA_HEADS, tq), _F32),
            pltpu.VMEM((N_MOBA_HEADS, HEAD_DIM + MOBA_SUM_ROWS, tq), _F32),
        ],
        compiler_params=pltpu.CompilerParams(
            dimension_semantics=("arbitrary", "arbitrary"), vmem_limit_bytes=V7X_VMEM_LIMIT_BYTES),
        name="moba",
    )(q, k3, vt, bias, sga)


def _pool_mem_kernel(rest_ref, halo_ref, mk_ref, mv_ref, wp_ref, ps_ref, o_ref, ext_sc,
                     *, tiles_per_batch):
    tm = rest_ref.shape[0]
    tb = lax.rem(pl.program_id(0), tiles_per_batch)
    u0, gp0, qm0, gm0 = 0, POOL_WIDTH, 2 * POOL_WIDTH, 2 * POOL_WIDTH + MEM_WIDTH

    u = rest_ref[:, u0:gp0].astype(_F32)
    ext_sc[0:POOL_HALO, :] = jnp.where(tb != 0, halo_ref[...].astype(_F32), 0.0)
    ext_sc[POOL_HALO:, :] = u
    pos = tb * tm + lax.broadcasted_iota(jnp.int32, (tm, POOL_GROUP_DIM), 0)
    for g, w in enumerate(POOL_WINDOWS):
        cols = slice(g * POOL_GROUP_DIM, (g + 1) * POOL_GROUP_DIM)
        u_g = u[:, cols]
        win = u_g
        for back in range(1, w):
            win = win + ext_sc[pl.ds(POOL_HALO - back, tm), cols]
        cnt = jnp.minimum(pos + 1, w).astype(_F32)
        pooled = win / cnt - u_g
        mixed = jnp.dot(pooled.astype(_BF16), wp_ref[g], preferred_element_type=_F32)
        gate = rest_ref[:, gp0 + g * POOL_GROUP_DIM:gp0 + (g + 1) * POOL_GROUP_DIM].astype(_F32)
        o_ref[:, cols] = (mixed * ps_ref[:, cols] * gate).astype(_BF16)

    for hd in range(N_MEM_HEADS):
        cols = slice(hd * HEAD_DIM, (hd + 1) * HEAD_DIM)
        qm = rest_ref[:, qm0 + hd * HEAD_DIM:qm0 + (hd + 1) * HEAD_DIM]
        s = _nt_dot(qm, mk_ref[:, cols])
        m = jnp.max(s, axis=-1, keepdims=True)
        e = jnp.exp(s - m)
        l = jnp.sum(e, axis=-1, keepdims=True)
        o = jnp.dot(e.astype(_BF16), mv_ref[:, cols], preferred_element_type=_F32) / l
        gate = rest_ref[:, gm0 + hd * HEAD_DIM:gm0 + (hd + 1) * HEAD_DIM].astype(_F32)
        o_ref[:, POOL_WIDTH + hd * HEAD_DIM:POOL_WIDTH + (hd + 1) * HEAD_DIM] = (o * gate).astype(_BF16)


def _pool_mem(rest, mk, mv, wp_bf16, pool_scale, *, seq):
    N = rest.shape[0]
    tm = POOL_MEM_ROWS
    tiles_per_batch = seq // tm
    halo_blocks_per_tile = tm // POOL_HALO
    M = mk.shape[1]
    kern = functools.partial(_pool_mem_kernel, tiles_per_batch=tiles_per_batch)
    return pl.pallas_call(
        kern,
        grid=(N // tm,),
        in_specs=[
            pl.BlockSpec((tm, _REST_W), lambda t: (t, 0)),
            pl.BlockSpec((POOL_HALO, POOL_WIDTH),
                         lambda t: (jnp.maximum(t * halo_blocks_per_tile - 1, 0), 0)),
            pl.BlockSpec((None, M, MEM_WIDTH), lambda t: (t // tiles_per_batch, 0, 0)),
            pl.BlockSpec((None, M, MEM_WIDTH), lambda t: (t // tiles_per_batch, 0, 0)),
            pl.BlockSpec(wp_bf16.shape, lambda t: (0, 0, 0)),
            pl.BlockSpec((1, POOL_WIDTH), lambda t: (0, 0)),
        ],
        out_specs=pl.BlockSpec((tm, POOL_WIDTH + MEM_WIDTH), lambda t: (t, 0)),
        out_shape=jax.ShapeDtypeStruct((N, POOL_WIDTH + MEM_WIDTH), _BF16),
        scratch_shapes=[pltpu.VMEM((tm + POOL_HALO, POOL_WIDTH), _F32)],
        compiler_params=pltpu.CompilerParams(dimension_semantics=("arbitrary",)),
        name="pool_mem",
    )(rest, rest, mk, mv, wp_bf16, pool_scale)


def _out_proj_kernel(ya_ref, ypm_ref, x_ref, w_ref, fg_ref, o_ref, *, final_norm):
    y = jnp.dot(ya_ref[...], w_ref[0:MOBA_WIDTH, :], preferred_element_type=_F32)
    y = y + jnp.dot(ypm_ref[...], w_ref[MOBA_WIDTH:, :], preferred_element_type=_F32)
    r = x_ref[...] + y
    o_ref[...] = _rmsnorm(r, fg_ref[...]) if final_norm else r


def _out_proj(ya, ypm, xf, w_bf16, final_g, *, final_norm):
    N, D = xf.shape
    tm = OUT_PROJ_ROWS
    kern = functools.partial(_out_proj_kernel, final_norm=final_norm)
    return pl.pallas_call(
        kern,
        grid=(N // tm,),
        in_specs=[
            pl.BlockSpec((tm, MOBA_WIDTH), lambda t: (t, 0)),
            pl.BlockSpec((tm, POOL_WIDTH + MEM_WIDTH), lambda t: (t, 0)),
            pl.BlockSpec((tm, D), lambda t: (t, 0)),
            pl.BlockSpec(w_bf16.shape, lambda t: (0, 0), pipeline_mode=pl.Buffered(1)),
            pl.BlockSpec((1, D), lambda t: (0, 0)),
        ],
        out_specs=pl.BlockSpec((tm, D), lambda t: (t, 0)),
        out_shape=jax.ShapeDtypeStruct((N, D), _F32),
        compiler_params=pltpu.CompilerParams(
            dimension_semantics=("arbitrary",), vmem_limit_bytes=V7X_VMEM_LIMIT_BYTES),
        name="out_proj",
    )(ya, ypm, xf, w_bf16, final_g)


def _layer(xf, mem, norm_g, mem_norm_g, w_in, w_mem_kv, w_pool, pool_scale, w_out, final_g,
           *, batch, seq, final_norm):
    mk, mv = _mem_kv(mem, mem_norm_g[None, :], w_mem_kv.astype(_BF16))
    q, k, vt, sga, rest, bias = _in_proj(xf, norm_g[None, :], w_in.astype(_BF16), batch=batch, seq=seq)
    ya = _moba(q, k, vt, bias, sga, batch=batch, seq=seq)
    ypm = _pool_mem(rest, mk, mv, w_pool.astype(_BF16), pool_scale[None, :], seq=seq)
    return _out_proj(ya, ypm, xf, w_out.astype(_BF16), final_g[None, :], final_norm=final_norm)


def kernel(x, mem, norm_g, mem_norm_g, w_in, w_mem_kv, w_pool, pool_scale, w_out, final_norm_g):
    batch, seq, d_model = x.shape
    depth = norm_g.shape[0]
    assert seq % MOBA_BLOCK == 0 and seq % POOL_MEM_ROWS == 0 and seq % OUT_PROJ_ROWS == 0
    assert w_in.shape[2] == 4 * MOBA_WIDTH + 2 * POOL_WIDTH + 2 * MEM_WIDTH
    xf = x.reshape(batch * seq, d_model)
    for l in range(depth):
        xf = _layer(xf, mem, norm_g[l], mem_norm_g[l], w_in[l], w_mem_kv[l], w_pool[l],
                    pool_scale[l], w_out[l], final_norm_g,
                    batch=batch, seq=seq, final_norm=(l == depth - 1))
    return xf.reshape(batch, seq, d_model)
```

```python
import functools

import jax
import jax.numpy as jnp
from jax import lax
from jax.experimental import pallas as pl
from jax.experimental.pallas import tpu as pltpu

HEAD_DIM = 128
N_MOBA_HEADS = 8
MOBA_WIDTH = N_MOBA_HEADS * HEAD_DIM
N_MEM_HEADS = 4
MEM_WIDTH = N_MEM_HEADS * HEAD_DIM
POOL_WINDOWS = (2, 4, 8, 16)
POOL_GROUP_DIM = 128
POOL_WIDTH = len(POOL_WINDOWS) * POOL_GROUP_DIM
POOL_HALO = 16
MOBA_BLOCK = 256
MOBA_TOPK = 3
EPS = 1e-6
SCALE = HEAD_DIM ** -0.5
LOG2_E = 1.4426950408889634
MOBA_SUM_ROWS = 16

_Q0, _K0, _V0, _GA0 = 0, MOBA_WIDTH, 2 * MOBA_WIDTH, 3 * MOBA_WIDTH
_REST0 = 4 * MOBA_WIDTH
_REST_W = 2 * POOL_WIDTH + 2 * MEM_WIDTH

V7X_VMEM_LIMIT_BYTES = 56 * 1024 * 1024

IN_PROJ_ROWS = 512
POOL_MEM_ROWS = 512
OUT_PROJ_ROWS = 512
OUT_PROJ_SUBROWS = 256

_F32 = jnp.float32
_BF16 = jnp.bfloat16
_NEG_INF = float("-inf")


def _nt_dot(a, b):
    return lax.dot_general(a, b, (((1,), (1,)), ((), ())), preferred_element_type=_F32)


def _split_bf16(a):
    hi = a.astype(_BF16)
    lo = (a - hi.astype(_F32)).astype(_BF16)
    return hi, lo


def _nt_dot_f32(a, b):
    a_hi, a_lo = _split_bf16(a)
    b_hi, b_lo = _split_bf16(b)
    return _nt_dot(a_hi, b_hi) + (_nt_dot(a_hi, b_lo) + _nt_dot(a_lo, b_hi))


def _silu(a):
    return a * (1.0 / (1.0 + jnp.exp(-a)))


def _rmsnorm(x, g):
    ms = jnp.mean(x * x, axis=-1, keepdims=True)
    return x * lax.rsqrt(ms + EPS) * g


def _mem_kv_kernel(mem_ref, g_ref, w_ref, mk_ref, mv_ref):
    h = _rmsnorm(mem_ref[...], g_ref[...]).astype(_BF16)
    kv = jnp.dot(h, w_ref[...], preferred_element_type=_F32)
    mk_ref[...] = kv[:, :MEM_WIDTH].astype(_BF16)
    mv_ref[...] = kv[:, MEM_WIDTH:].astype(_BF16)


def _mem_kv(mem, g, w_bf16):
    B, M, D = mem.shape
    return pl.pallas_call(
        _mem_kv_kernel,
        grid=(B,),
        in_specs=[
            pl.BlockSpec((None, M, D), lambda b: (b, 0, 0)),
            pl.BlockSpec((1, D), lambda b: (0, 0)),
            pl.BlockSpec((D, 2 * MEM_WIDTH), lambda b: (0, 0)),
        ],
        out_specs=[
            pl.BlockSpec((None, M, MEM_WIDTH), lambda b: (b, 0, 0)),
            pl.BlockSpec((None, M, MEM_WIDTH), lambda b: (b, 0, 0)),
        ],
        out_shape=[jax.ShapeDtypeStruct((B, M, MEM_WIDTH), _BF16)] * 2,
        compiler_params=pltpu.CompilerParams(dimension_semantics=("arbitrary",)),
        name="mem_kv",
    )(mem, g, w_bf16)


def _in_proj_kernel(x_ref, g_ref, w_ref,
                    q_ref, k_ref, vt_ref, sga_ref, rest_ref, bias_ref,
                    kmean_sc, *, tiles_per_batch, n_blocks):
    tm = x_ref.shape[0]
    blocks_per_tile = tm // MOBA_BLOCK
    t = pl.program_id(0)
    first_block = lax.rem(t, tiles_per_batch) * blocks_per_tile

    @pl.when(t == 0)
    def _():
        kmean_sc[...] = jnp.zeros_like(kmean_sc)

    h = _rmsnorm(x_ref[...], g_ref[...]).astype(_BF16)

    k = jnp.dot(h, w_ref[:, _K0:_K0 + MOBA_WIDTH], preferred_element_type=_F32)
    k_ref[...] = k.astype(_BF16)
    for b in range(blocks_per_tile):
        kmean_sc[pl.ds(first_block + b, 1), :] = jnp.mean(
            k[b * MOBA_BLOCK:(b + 1) * MOBA_BLOCK], axis=0, keepdims=True)

    q = jnp.dot(h, w_ref[:, _Q0:_Q0 + MOBA_WIDTH], preferred_element_type=_F32)
    q_ref[...] = (q * (SCALE * LOG2_E)).astype(_BF16)

    n_idx = lax.broadcasted_iota(jnp.int32, (n_blocks, tm), 0)
    col = lax.broadcasted_iota(jnp.int32, (n_blocks, tm), 1)
    own_block = first_block + col // MOBA_BLOCK
    is_past = n_idx < own_block
    for hd in range(N_MOBA_HEADS):
        cols = slice(hd * HEAD_DIM, (hd + 1) * HEAD_DIM)
        gate = jnp.where(is_past, _nt_dot_f32(kmean_sc[:, cols], q[:, cols]), _NEG_INF)
        keep = None
        for _ in range(MOBA_TOPK):
            best = jnp.max(gate, axis=0, keepdims=True)
            first = jnp.min(jnp.where(gate == best, n_idx, n_blocks), axis=0, keepdims=True)
            pick = n_idx == first
            keep = pick if keep is None else keep | pick
            gate = jnp.where(pick, _NEG_INF, gate)
        bias_ref[hd] = jnp.where(keep & is_past, 0.0, _NEG_INF)

    v = jnp.dot(h, w_ref[:, _V0:_V0 + MOBA_WIDTH], preferred_element_type=_F32)
    for b in range(blocks_per_tile):
        vt_ref[b] = v[b * MOBA_BLOCK:(b + 1) * MOBA_BLOCK].T.astype(_BF16)

    ga = jnp.dot(h, w_ref[:, _GA0:_GA0 + MOBA_WIDTH], preferred_element_type=_F32)
    sga_ref[...] = _silu(ga).astype(_BF16)

    rest = jnp.dot(h, w_ref[:, _REST0:_REST0 + _REST_W], preferred_element_type=_F32)
    u0, gp0, qm0, gm0 = 0, POOL_WIDTH, 2 * POOL_WIDTH, 2 * POOL_WIDTH + MEM_WIDTH
    rest_ref[:, u0:gp0] = rest[:, u0:gp0].astype(_BF16)
    rest_ref[:, gp0:qm0] = _silu(rest[:, gp0:qm0]).astype(_BF16)
    rest_ref[:, qm0:gm0] = (rest[:, qm0:gm0] * SCALE).astype(_BF16)
    rest_ref[:, gm0:] = _silu(rest[:, gm0:]).astype(_BF16)


def _in_proj(xf, g, w_bf16, *, batch, seq):
    N, D = xf.shape
    tm = IN_PROJ_ROWS
    tiles_per_batch = seq // tm
    blocks_per_tile = tm // MOBA_BLOCK
    n_blocks = seq // MOBA_BLOCK
    resident = pl.Buffered(1)
    kern = functools.partial(_in_proj_kernel, tiles_per_batch=tiles_per_batch, n_blocks=n_blocks)
    return pl.pallas_call(
        kern,
        grid=(N // tm,),
        in_specs=[
            pl.BlockSpec((tm, D), lambda t: (t, 0)),
            pl.BlockSpec((1, D), lambda t: (0, 0)),
            pl.BlockSpec(w_bf16.shape, lambda t: (0, 0), pipeline_mode=resident),
        ],
        out_specs=[
            pl.BlockSpec((tm, MOBA_WIDTH), lambda t: (t, 0)),
            pl.BlockSpec((tm, MOBA_WIDTH), lambda t: (t, 0)),
            pl.BlockSpec((None, blocks_per_tile, MOBA_WIDTH, MOBA_BLOCK),
                         lambda t: (t // tiles_per_batch, t % tiles_per_batch, 0, 0)),
            pl.BlockSpec((tm, MOBA_WIDTH), lambda t: (t, 0)),
            pl.BlockSpec((tm, _REST_W), lambda t: (t, 0)),
            pl.BlockSpec((None, N_MOBA_HEADS, n_blocks, tm),
                         lambda t: (t // tiles_per_batch, 0, 0, t % tiles_per_batch)),
        ],
        out_shape=[
            jax.ShapeDtypeStruct((N, MOBA_WIDTH), _BF16),
            jax.ShapeDtypeStruct((N, MOBA_WIDTH), _BF16),
            jax.ShapeDtypeStruct((batch, n_blocks, MOBA_WIDTH, MOBA_BLOCK), _BF16),
            jax.ShapeDtypeStruct((N, MOBA_WIDTH), _BF16),
            jax.ShapeDtypeStruct((N, _REST_W), _BF16),
            jax.ShapeDtypeStruct((batch, N_MOBA_HEADS, n_blocks, seq), _F32),
        ],
        scratch_shapes=[pltpu.VMEM((n_blocks, MOBA_WIDTH), _F32)],
        compiler_params=pltpu.CompilerParams(
            dimension_semantics=("arbitrary",), vmem_limit_bytes=V7X_VMEM_LIMIT_BYTES),
        name="in_proj",
    )(xf, g, w_bf16)


def _moba_kernel(q_ref, k_ref, vt_ref, bias_ref, sga_ref, o_ref, s_sc, smax_sc, m_sc, acc_sc):
    tq = q_ref.shape[0]
    i = pl.program_id(1)
    ones_rows = jnp.ones((MOBA_SUM_ROWS, MOBA_BLOCK), _BF16)
    zero_bias = jnp.zeros((1, tq), _F32)

    def head_cols(hd):
        return slice(hd * HEAD_DIM, (hd + 1) * HEAD_DIM)

    def produce(j, slot, mask):
        start = pl.multiple_of(j * MOBA_BLOCK, MOBA_BLOCK)
        for hd in range(N_MOBA_HEADS):
            s = mask(_nt_dot(k_ref[pl.ds(start, MOBA_BLOCK), head_cols(hd)],
                             q_ref[:, head_cols(hd)]))
            s_sc[slot, hd] = s
            smax_sc[slot, pl.ds(hd, 1), :] = jnp.max(s, axis=0, keepdims=True)

    def consume(blocks):
        for hd in range(N_MOBA_HEADS):
            row = pl.ds(hd, 1)
            biases = [zero_bias if own else bias_ref[hd, pl.ds(j, 1), :]
                      for j, _, own in blocks]
            m = m_sc[row, :]
            m_new = m
            for (_, slot, _), bias in zip(blocks, biases):
                m_new = jnp.maximum(m_new, smax_sc[slot, row, :] + bias)
            alpha = jnp.exp2(m - m_new)
            m_sc[row, :] = m_new
            pv = None
            for (j, slot, _), bias in zip(blocks, biases):
                p = jnp.exp2(s_sc[slot, hd] - (m_new - bias))
                v_and_ones = jnp.concatenate([vt_ref[j, head_cols(hd), :], ones_rows], axis=0)
                d = jnp.dot(v_and_ones, p.astype(_BF16), preferred_element_type=_F32)
                pv = d if pv is None else pv + d
            acc_sc[hd] = alpha * acc_sc[hd] + pv

    m_sc[...] = jnp.full_like(m_sc, _NEG_INF)
    acc_sc[...] = jnp.zeros_like(acc_sc)

    n_pairs = jnp.maximum((i + 1) // 2, 1)

    def produce_pair(pair, slot_set):
        produce(2 * pair, 2 * slot_set, lambda s: s)
        produce(2 * pair + 1, 2 * slot_set + 1, lambda s: s)

    def consume_pair(pair, slot_set):
        consume([(2 * pair, 2 * slot_set, False), (2 * pair + 1, 2 * slot_set + 1, False)])

    kpos = lax.broadcasted_iota(jnp.int32, (MOBA_BLOCK, tq), 0)
    qpos = lax.broadcasted_iota(jnp.int32, (MOBA_BLOCK, tq), 1)
    causal = kpos <= qpos
    produce(i, 4, lambda s: jnp.where(causal, s, _NEG_INF))
    produce_pair(0, 0)
    consume([(i, 4, True)])

    def body(it, carry):
        produce_pair(2 * it + 1, 1)
        consume_pair(2 * it, 0)
        produce_pair(2 * it + 2, 0)
        consume_pair(2 * it + 1, 1)
        return carry

    n_full = (n_pairs - 1) // 2
    lax.fori_loop(0, n_full, body, 0)

    @pl.when(n_pairs - 2 * n_full == 2)
    def _():
        produce_pair(2 * n_full + 1, 1)
        consume_pair(2 * n_full, 0)
        consume_pair(2 * n_full + 1, 1)

    @pl.when(n_pairs - 2 * n_full == 1)
    def _():
        consume_pair(2 * n_full, 0)

    for hd in range(N_MOBA_HEADS):
        l = acc_sc[hd, HEAD_DIM:HEAD_DIM + 1, :]
        o = (acc_sc[hd, 0:HEAD_DIM, :] / l).T
        o_ref[:, head_cols(hd)] = (o * sga_ref[:, head_cols(hd)].astype(_F32)).astype(_BF16)


def _moba(q, k, vt, bias, sga, *, batch, seq):
    N = q.shape[0]
    n_blocks = seq // MOBA_BLOCK
    tq = MOBA_BLOCK
    k3 = k.reshape(batch, seq, MOBA_WIDTH)
    row_block = lambda b, i: (b * n_blocks + i, 0)
    per_batch = pl.Buffered(1)
    return pl.pallas_call(
        _moba_kernel,
        grid=(batch, n_blocks),
        in_specs=[
            pl.BlockSpec((tq, MOBA_WIDTH), row_block),
            pl.BlockSpec((None, seq, MOBA_WIDTH), lambda b, i: (b, 0, 0), pipeline_mode=per_batch),
            pl.BlockSpec((None, n_blocks, MOBA_WIDTH, MOBA_BLOCK), lambda b, i: (b, 0, 0, 0),
                         pipeline_mode=per_batch),
            pl.BlockSpec((None, N_MOBA_HEADS, n_blocks, tq), lambda b, i: (b, 0, 0, i)),
            pl.BlockSpec((tq, MOBA_WIDTH), row_block),
        ],
        out_specs=pl.BlockSpec((tq, MOBA_WIDTH), row_block),
        out_shape=jax.ShapeDtypeStruct((N, MOBA_WIDTH), _BF16),
        scratch_shapes=[
            pltpu.VMEM((5, N_MOBA_HEADS, MOBA_BLOCK, tq), _F32),
            pltpu.VMEM((5, N_MOBA_HEADS, tq), _F32),
            pltpu.VMEM((N_MOBA_HEADS, tq), _F32),
            pltpu.VMEM((N_MOBA_HEADS, HEAD_DIM + MOBA_SUM_ROWS, tq), _F32),
        ],
        compiler_params=pltpu.CompilerParams(
            dimension_semantics=("arbitrary", "arbitrary"), vmem_limit_bytes=V7X_VMEM_LIMIT_BYTES),
        name="moba",
    )(q, k3, vt, bias, sga)


def _pool_mem_kernel(rest_ref, halo_ref, mk_ref, mv_ref, wp_ref, ps_ref, o_ref, *, tiles_per_batch):
    tm = rest_ref.shape[0]
    tb = lax.rem(pl.program_id(0), tiles_per_batch)
    u0, gp0, qm0, gm0 = 0, POOL_WIDTH, 2 * POOL_WIDTH, 2 * POOL_WIDTH + MEM_WIDTH

    u = rest_ref[:, u0:gp0].astype(_F32)
    halo = jnp.where(tb != 0, halo_ref[...].astype(_F32), 0.0)
    pos = tb * tm + lax.broadcasted_iota(jnp.int32, (tm, POOL_GROUP_DIM), 0)
    for g, w in enumerate(POOL_WINDOWS):
        cols = slice(g * POOL_GROUP_DIM, (g + 1) * POOL_GROUP_DIM)
        u_g = u[:, cols]
        run = jnp.concatenate([halo[:, cols], u_g], axis=0)
        span = 1
        while span < w:
            run = run + pltpu.roll(run, span, axis=0)
            span *= 2
        win = run[POOL_HALO:]
        cnt = jnp.minimum(pos + 1, w).astype(_F32)
        pooled = win / cnt - u_g
        mixed = jnp.dot(pooled.astype(_BF16), wp_ref[g], preferred_element_type=_F32)
        gate = rest_ref[:, gp0 + g * POOL_GROUP_DIM:gp0 + (g + 1) * POOL_GROUP_DIM].astype(_F32)
        o_ref[:, cols] = (mixed * ps_ref[:, cols] * gate).astype(_BF16)

    for hd in range(N_MEM_HEADS):
        cols = slice(hd * HEAD_DIM, (hd + 1) * HEAD_DIM)
        qm = rest_ref[:, qm0 + hd * HEAD_DIM:qm0 + (hd + 1) * HEAD_DIM]
        s = _nt_dot(qm, mk_ref[:, cols])
        m = jnp.max(s, axis=-1, keepdims=True)
        e = jnp.exp(s - m)
        l = jnp.sum(e, axis=-1, keepdims=True)
        o = jnp.dot(e.astype(_BF16), mv_ref[:, cols], preferred_element_type=_F32) / l
        gate = rest_ref[:, gm0 + hd * HEAD_DIM:gm0 + (hd + 1) * HEAD_DIM].astype(_F32)
        o_ref[:, POOL_WIDTH + hd * HEAD_DIM:POOL_WIDTH + (hd + 1) * HEAD_DIM] = (o * gate).astype(_BF16)


def _pool_mem(rest, mk, mv, wp_bf16, pool_scale, *, seq):
    N = rest.shape[0]
    tm = POOL_MEM_ROWS
    tiles_per_batch = seq // tm
    halo_blocks_per_tile = tm // POOL_HALO
    M = mk.shape[1]
    kern = functools.partial(_pool_mem_kernel, tiles_per_batch=tiles_per_batch)
    return pl.pallas_call(
        kern,
        grid=(N // tm,),
        in_specs=[
            pl.BlockSpec((tm, _REST_W), lambda t: (t, 0)),
            pl.BlockSpec((POOL_HALO, POOL_WIDTH),
                         lambda t: (jnp.maximum(t * halo_blocks_per_tile - 1, 0), 0)),
            pl.BlockSpec((None, M, MEM_WIDTH), lambda t: (t // tiles_per_batch, 0, 0)),
            pl.BlockSpec((None, M, MEM_WIDTH), lambda t: (t // tiles_per_batch, 0, 0)),
            pl.BlockSpec(wp_bf16.shape, lambda t: (0, 0, 0)),
            pl.BlockSpec((1, POOL_WIDTH), lambda t: (0, 0)),
        ],
        out_specs=pl.BlockSpec((tm, POOL_WIDTH + MEM_WIDTH), lambda t: (t, 0)),
        out_shape=jax.ShapeDtypeStruct((N, POOL_WIDTH + MEM_WIDTH), _BF16),
        compiler_params=pltpu.CompilerParams(dimension_semantics=("arbitrary",)),
        name="pool_mem",
    )(rest, rest, mk, mv, wp_bf16, pool_scale)


def _out_proj_kernel(ya_ref, ypm_ref, x_ref, w_ref, fg_ref, o_ref, *, final_norm):
    for r0 in range(0, x_ref.shape[0], OUT_PROJ_SUBROWS):
        rows = slice(r0, r0 + OUT_PROJ_SUBROWS)
        y = jnp.dot(ya_ref[rows, :], w_ref[0:MOBA_WIDTH, :], preferred_element_type=_F32)
        y = y + jnp.dot(ypm_ref[rows, :], w_ref[MOBA_WIDTH:, :], preferred_element_type=_F32)
        r = x_ref[rows, :] + y
        o_ref[rows, :] = _rmsnorm(r, fg_ref[...]) if final_norm else r


def _out_proj(ya, ypm, xf, w_bf16, final_g, *, final_norm):
    N, D = xf.shape
    tm = OUT_PROJ_ROWS
    kern = functools.partial(_out_proj_kernel, final_norm=final_norm)
    return pl.pallas_call(
        kern,
        grid=(N // tm,),
        in_specs=[
            pl.BlockSpec((tm, MOBA_WIDTH), lambda t: (t, 0)),
            pl.BlockSpec((tm, POOL_WIDTH + MEM_WIDTH), lambda t: (t, 0)),
            pl.BlockSpec((tm, D), lambda t: (t, 0)),
            pl.BlockSpec(w_bf16.shape, lambda t: (0, 0), pipeline_mode=pl.Buffered(1)),
            pl.BlockSpec((1, D), lambda t: (0, 0)),
        ],
        out_specs=pl.BlockSpec((tm, D), lambda t: (t, 0)),
        out_shape=jax.ShapeDtypeStruct((N, D), _F32),
        compiler_params=pltpu.CompilerParams(
            dimension_semantics=("arbitrary",), vmem_limit_bytes=V7X_VMEM_LIMIT_BYTES),
        name="out_proj",
    )(ya, ypm, xf, w_bf16, final_g)


def _layer(xf, mem, norm_g, mem_norm_g, w_in, w_mem_kv, w_pool, pool_scale, w_out, final_g,
           *, batch, seq, final_norm):
    mk, mv = _mem_kv(mem, mem_norm_g[None, :], w_mem_kv.astype(_BF16))
    q, k, vt, sga, rest, bias = _in_proj(xf, norm_g[None, :], w_in.astype(_BF16), batch=batch, seq=seq)
    ya = _moba(q, k, vt, bias, sga, batch=batch, seq=seq)
    ypm = _pool_mem(rest, mk, mv, w_pool.astype(_BF16), pool_scale[None, :], seq=seq)
    return _out_proj(ya, ypm, xf, w_out.astype(_BF16), final_g[None, :], final_norm=final_norm)


def kernel(x, mem, norm_g, mem_norm_g, w_in, w_mem_kv, w_pool, pool_scale, w_out, final_norm_g):
    batch, seq, d_model = x.shape
    depth = norm_g.shape[0]
    assert seq % MOBA_BLOCK == 0 and seq % IN_PROJ_ROWS == 0
    assert seq % POOL_MEM_ROWS == 0 and seq % OUT_PROJ_ROWS == 0
    assert w_in.shape[2] == 4 * MOBA_WIDTH + 2 * POOL_WIDTH + 2 * MEM_WIDTH
    xf = x.reshape(batch * seq, d_model)
    for l in range(depth):
        xf = _layer(xf, mem, norm_g[l], mem_norm_g[l], w_in[l], w_mem_kv[l], w_pool[l],
                    pool_scale[l], w_out[l], final_norm_g,
                    batch=batch, seq=seq, final_norm=(l == depth - 1))
    return xf.reshape(batch, seq, d_model)
```

```python
import functools

import jax
import jax.numpy as jnp
from jax import lax
from jax.experimental import pallas as pl
from jax.experimental.pallas import tpu as pltpu

HEAD_DIM = 128
N_MOBA_HEADS = 8
MOBA_WIDTH = N_MOBA_HEADS * HEAD_DIM
N_MEM_HEADS = 4
MEM_WIDTH = N_MEM_HEADS * HEAD_DIM
POOL_WINDOWS = (2, 4, 8, 16)
POOL_GROUP_DIM = 128
POOL_WIDTH = len(POOL_WINDOWS) * POOL_GROUP_DIM
POOL_HALO = 16
MOBA_BLOCK = 256
MOBA_TOPK = 3
EPS = 1e-6
SCALE = HEAD_DIM ** -0.5
LOG2_E = 1.4426950408889634
MOBA_SUM_ROWS = 16

_Q0, _K0, _V0, _GA0 = 0, MOBA_WIDTH, 2 * MOBA_WIDTH, 3 * MOBA_WIDTH
_REST0 = 4 * MOBA_WIDTH
_REST_W = 2 * POOL_WIDTH + 2 * MEM_WIDTH

V7X_VMEM_LIMIT_BYTES = 56 * 1024 * 1024

IN_PROJ_ROWS = 512
POOL_MEM_ROWS = 512
OUT_PROJ_ROWS = 512
OUT_PROJ_SUBROWS = 256

_F32 = jnp.float32
_BF16 = jnp.bfloat16
_NEG_INF = float("-inf")


def _nt_dot(a, b):
    return lax.dot_general(a, b, (((1,), (1,)), ((), ())), preferred_element_type=_F32)


def _split_bf16(a):
    hi = a.astype(_BF16)
    lo = (a - hi.astype(_F32)).astype(_BF16)
    return hi, lo


def _nt_dot_f32(a, b):
    a_hi, a_lo = _split_bf16(a)
    b_hi, b_lo = _split_bf16(b)
    return _nt_dot(a_hi, b_hi) + (_nt_dot(a_hi, b_lo) + _nt_dot(a_lo, b_hi))


def _silu(a):
    return a * (1.0 / (1.0 + jnp.exp(-a)))


def _rmsnorm(x, g):
    ms = jnp.mean(x * x, axis=-1, keepdims=True)
    return x * lax.rsqrt(ms + EPS) * g


def _mem_kv_kernel(mem_ref, g_ref, w_ref, mk_ref, mv_ref):
    h = _rmsnorm(mem_ref[...], g_ref[...]).astype(_BF16)
    kv = jnp.dot(h, w_ref[...], preferred_element_type=_F32)
    mk_ref[...] = kv[:, :MEM_WIDTH].astype(_BF16)
    mv_ref[...] = kv[:, MEM_WIDTH:].astype(_BF16)


def _mem_kv(mem, g, w_bf16):
    B, M, D = mem.shape
    return pl.pallas_call(
        _mem_kv_kernel,
        grid=(B,),
        in_specs=[
            pl.BlockSpec((None, M, D), lambda b: (b, 0, 0)),
            pl.BlockSpec((1, D), lambda b: (0, 0)),
            pl.BlockSpec((D, 2 * MEM_WIDTH), lambda b: (0, 0)),
        ],
        out_specs=[
            pl.BlockSpec((None, M, MEM_WIDTH), lambda b: (b, 0, 0)),
            pl.BlockSpec((None, M, MEM_WIDTH), lambda b: (b, 0, 0)),
        ],
        out_shape=[jax.ShapeDtypeStruct((B, M, MEM_WIDTH), _BF16)] * 2,
        compiler_params=pltpu.CompilerParams(dimension_semantics=("arbitrary",)),
        name="mem_kv",
    )(mem, g, w_bf16)


def _in_proj_kernel(x_ref, g_ref, w_ref,
                    q_ref, k_ref, vt_ref, sga_ref, rest_ref, bias_ref,
                    kmean_sc, *, tiles_per_batch, n_blocks):
    tm = x_ref.shape[0]
    blocks_per_tile = tm // MOBA_BLOCK
    t = pl.program_id(0)
    first_block = lax.rem(t, tiles_per_batch) * blocks_per_tile

    @pl.when(t == 0)
    def _():
        kmean_sc[...] = jnp.zeros_like(kmean_sc)

    h = _rmsnorm(x_ref[...], g_ref[...]).astype(_BF16)

    k = jnp.dot(h, w_ref[:, _K0:_K0 + MOBA_WIDTH], preferred_element_type=_F32)
    k_ref[...] = k.astype(_BF16)
    for b in range(blocks_per_tile):
        kmean_sc[pl.ds(first_block + b, 1), :] = jnp.mean(
            k[b * MOBA_BLOCK:(b + 1) * MOBA_BLOCK], axis=0, keepdims=True)

    q = jnp.dot(h, w_ref[:, _Q0:_Q0 + MOBA_WIDTH], preferred_element_type=_F32)
    q_ref[...] = (q * (SCALE * LOG2_E)).astype(_BF16)

    n_idx = lax.broadcasted_iota(jnp.int32, (n_blocks, tm), 0)
    col = lax.broadcasted_iota(jnp.int32, (n_blocks, tm), 1)
    own_block = first_block + col // MOBA_BLOCK
    is_past = n_idx < own_block
    for hd in range(N_MOBA_HEADS):
        cols = slice(hd * HEAD_DIM, (hd + 1) * HEAD_DIM)
        gate = jnp.where(is_past, _nt_dot_f32(kmean_sc[:, cols], q[:, cols]), _NEG_INF)
        keep = None
        for _ in range(MOBA_TOPK):
            best = jnp.max(gate, axis=0, keepdims=True)
            first = jnp.min(jnp.where(gate == best, n_idx, n_blocks), axis=0, keepdims=True)
            pick = n_idx == first
            keep = pick if keep is None else keep | pick
            gate = jnp.where(pick, _NEG_INF, gate)
        bias_ref[hd] = jnp.where(keep & is_past, 0.0, _NEG_INF)

    v = jnp.dot(h, w_ref[:, _V0:_V0 + MOBA_WIDTH], preferred_element_type=_F32)
    for b in range(blocks_per_tile):
        vt_ref[b] = v[b * MOBA_BLOCK:(b + 1) * MOBA_BLOCK].T.astype(_BF16)

    ga = jnp.dot(h, w_ref[:, _GA0:_GA0 + MOBA_WIDTH], preferred_element_type=_F32)
    sga_ref[...] = _silu(ga).astype(_BF16)

    rest = jnp.dot(h, w_ref[:, _REST0:_REST0 + _REST_W], preferred_element_type=_F32)
    u0, gp0, qm0, gm0 = 0, POOL_WIDTH, 2 * POOL_WIDTH, 2 * POOL_WIDTH + MEM_WIDTH
    rest_ref[:, u0:gp0] = rest[:, u0:gp0].astype(_BF16)
    rest_ref[:, gp0:qm0] = _silu(rest[:, gp0:qm0]).astype(_BF16)
    rest_ref[:, qm0:gm0] = (rest[:, qm0:gm0] * SCALE).astype(_BF16)
    rest_ref[:, gm0:] = _silu(rest[:, gm0:]).astype(_BF16)


def _in_proj(xf, g, w_bf16, *, batch, seq):
    N, D = xf.shape
    tm = IN_PROJ_ROWS
    tiles_per_batch = seq // tm
    blocks_per_tile = tm // MOBA_BLOCK
    n_blocks = seq // MOBA_BLOCK
    resident = pl.Buffered(1)
    kern = functools.partial(_in_proj_kernel, tiles_per_batch=tiles_per_batch, n_blocks=n_blocks)
    return pl.pallas_call(
        kern,
        grid=(N // tm,),
        in_specs=[
            pl.BlockSpec((tm, D), lambda t: (t, 0)),
            pl.BlockSpec((1, D), lambda t: (0, 0)),
            pl.BlockSpec(w_bf16.shape, lambda t: (0, 0), pipeline_mode=resident),
        ],
        out_specs=[
            pl.BlockSpec((tm, MOBA_WIDTH), lambda t: (t, 0)),
            pl.BlockSpec((tm, MOBA_WIDTH), lambda t: (t, 0)),
            pl.BlockSpec((None, blocks_per_tile, MOBA_WIDTH, MOBA_BLOCK),
                         lambda t: (t // tiles_per_batch, t % tiles_per_batch, 0, 0)),
            pl.BlockSpec((tm, MOBA_WIDTH), lambda t: (t, 0)),
            pl.BlockSpec((tm, _REST_W), lambda t: (t, 0)),
            pl.BlockSpec((None, N_MOBA_HEADS, n_blocks, tm),
                         lambda t: (t // tiles_per_batch, 0, 0, t % tiles_per_batch)),
        ],
        out_shape=[
            jax.ShapeDtypeStruct((N, MOBA_WIDTH), _BF16),
            jax.ShapeDtypeStruct((N, MOBA_WIDTH), _BF16),
            jax.ShapeDtypeStruct((batch, n_blocks, MOBA_WIDTH, MOBA_BLOCK), _BF16),
            jax.ShapeDtypeStruct((N, MOBA_WIDTH), _BF16),
            jax.ShapeDtypeStruct((N, _REST_W), _BF16),
            jax.ShapeDtypeStruct((batch, N_MOBA_HEADS, n_blocks, seq), _F32),
        ],
        scratch_shapes=[pltpu.VMEM((n_blocks, MOBA_WIDTH), _F32)],
        compiler_params=pltpu.CompilerParams(
            dimension_semantics=("arbitrary",), vmem_limit_bytes=V7X_VMEM_LIMIT_BYTES),
        name="in_proj",
    )(xf, g, w_bf16)


def _moba_kernel(q_ref, k_ref, vt_ref, bias_ref, sga_ref, o_ref, qt_sc, s_sc, smax_sc, m_sc, acc_sc):
    tq = q_ref.shape[0]
    i = pl.program_id(1)
    ones_rows = jnp.ones((MOBA_SUM_ROWS, MOBA_BLOCK), _BF16)
    zero_bias = jnp.zeros((1, tq), _F32)

    def head_cols(hd):
        return slice(hd * HEAD_DIM, (hd + 1) * HEAD_DIM)

    def produce_head(hd, j, slot, mask):
        start = pl.multiple_of(j * MOBA_BLOCK, MOBA_BLOCK)
        s = mask(jnp.dot(k_ref[pl.ds(start, MOBA_BLOCK), head_cols(hd)], qt_sc[hd],
                         preferred_element_type=_F32))
        s_sc[slot, hd] = s
        smax_sc[slot, pl.ds(hd, 1), :] = jnp.max(s, axis=0, keepdims=True)

    def consume_head(hd, blocks):
        row = pl.ds(hd, 1)
        biases = [zero_bias if own else bias_ref[hd, pl.ds(j, 1), :]
                  for j, _, own in blocks]
        m = m_sc[row, :]
        m_new = m
        for (_, slot, _), bias in zip(blocks, biases):
            m_new = jnp.maximum(m_new, smax_sc[slot, row, :] + bias)
        alpha = jnp.exp2(m - m_new)
        m_sc[row, :] = m_new
        pv = None
        for (j, slot, _), bias in zip(blocks, biases):
            p = jnp.exp2(s_sc[slot, hd] - (m_new - bias))
            v_and_ones = jnp.concatenate([vt_ref[j, head_cols(hd), :], ones_rows], axis=0)
            d = jnp.dot(v_and_ones, p.astype(_BF16), preferred_element_type=_F32)
            pv = d if pv is None else pv + d
        acc_sc[hd] = alpha * acc_sc[hd] + pv

    m_sc[...] = jnp.full_like(m_sc, _NEG_INF)
    acc_sc[...] = jnp.zeros_like(acc_sc)
    for hd in range(N_MOBA_HEADS):
        qt_sc[hd] = q_ref[:, head_cols(hd)].T

    n_pairs = jnp.maximum((i + 1) // 2, 1)

    def pair_blocks(pair):
        return [(2 * pair, 0, False), (2 * pair + 1, 1, False)]

    kpos = lax.broadcasted_iota(jnp.int32, (MOBA_BLOCK, tq), 0)
    qpos = lax.broadcasted_iota(jnp.int32, (MOBA_BLOCK, tq), 1)
    causal = kpos <= qpos
    for hd in range(N_MOBA_HEADS):
        produce_head(hd, i, 2, lambda s: jnp.where(causal, s, _NEG_INF))
        produce_head(hd, 0, 0, lambda s: s)
        produce_head(hd, 1, 1, lambda s: s)
    for hd in range(N_MOBA_HEADS):
        consume_head(hd, [(i, 2, True)])

    def body(pair, carry):
        for hd in range(N_MOBA_HEADS):
            consume_head(hd, pair_blocks(pair - 1))
            produce_head(hd, 2 * pair, 0, lambda s: s)
            produce_head(hd, 2 * pair + 1, 1, lambda s: s)
        return carry

    lax.fori_loop(1, n_pairs, body, 0)
    for hd in range(N_MOBA_HEADS):
        consume_head(hd, pair_blocks(n_pairs - 1))

    for hd in range(N_MOBA_HEADS):
        l = acc_sc[hd, HEAD_DIM:HEAD_DIM + 1, :]
        o = (acc_sc[hd, 0:HEAD_DIM, :] / l).T
        o_ref[:, head_cols(hd)] = (o * sga_ref[:, head_cols(hd)].astype(_F32)).astype(_BF16)


def _moba(q, k, vt, bias, sga, *, batch, seq):
    N = q.shape[0]
    n_blocks = seq // MOBA_BLOCK
    tq = MOBA_BLOCK
    k3 = k.reshape(batch, seq, MOBA_WIDTH)
    row_block = lambda b, i: (b * n_blocks + i, 0)
    per_batch = pl.Buffered(1)
    return pl.pallas_call(
        _moba_kernel,
        grid=(batch, n_blocks),
        in_specs=[
            pl.BlockSpec((tq, MOBA_WIDTH), row_block),
            pl.BlockSpec((None, seq, MOBA_WIDTH), lambda b, i: (b, 0, 0), pipeline_mode=per_batch),
            pl.BlockSpec((None, n_blocks, MOBA_WIDTH, MOBA_BLOCK), lambda b, i: (b, 0, 0, 0),
                         pipeline_mode=per_batch),
            pl.BlockSpec((None, N_MOBA_HEADS, n_blocks, tq), lambda b, i: (b, 0, 0, i)),
            pl.BlockSpec((tq, MOBA_WIDTH), row_block),
        ],
        out_specs=pl.BlockSpec((tq, MOBA_WIDTH), row_block),
        out_shape=jax.ShapeDtypeStruct((N, MOBA_WIDTH), _BF16),
        scratch_shapes=[
            pltpu.VMEM((N_MOBA_HEADS, HEAD_DIM, tq), _BF16),
            pltpu.VMEM((3, N_MOBA_HEADS, MOBA_BLOCK, tq), _F32),
            pltpu.VMEM((3, N_MOBA_HEADS, tq), _F32),
            pltpu.VMEM((N_MOBA_HEADS, tq), _F32),
            pltpu.VMEM((N_MOBA_HEADS, HEAD_DIM + MOBA_SUM_ROWS, tq), _F32),
        ],
        compiler_params=pltpu.CompilerParams(
            dimension_semantics=("arbitrary", "arbitrary"), vmem_limit_bytes=V7X_VMEM_LIMIT_BYTES),
        name="moba",
    )(q, k3, vt, bias, sga)


def _pool_mem_kernel(rest_ref, halo_ref, mk_ref, mv_ref, wp_ref, ps_ref, o_ref, *, tiles_per_batch):
    tm = rest_ref.shape[0]
    tb = lax.rem(pl.program_id(0), tiles_per_batch)
    u0, gp0, qm0, gm0 = 0, POOL_WIDTH, 2 * POOL_WIDTH, 2 * POOL_WIDTH + MEM_WIDTH

    u = rest_ref[:, u0:gp0].astype(_F32)
    halo = jnp.where(tb != 0, halo_ref[...].astype(_F32), 0.0)
    pos = tb * tm + lax.broadcasted_iota(jnp.int32, (tm, POOL_GROUP_DIM), 0)
    for g, w in enumerate(POOL_WINDOWS):
        cols = slice(g * POOL_GROUP_DIM, (g + 1) * POOL_GROUP_DIM)
        u_g = u[:, cols]
        run = jnp.concatenate([halo[:, cols], u_g], axis=0)
        span = 1
        while span < w:
            run = run + pltpu.roll(run, span, axis=0)
            span *= 2
        win = run[POOL_HALO:]
        cnt = jnp.minimum(pos + 1, w).astype(_F32)
        pooled = win / cnt - u_g
        mixed = jnp.dot(pooled.astype(_BF16), wp_ref[g], preferred_element_type=_F32)
        gate = rest_ref[:, gp0 + g * POOL_GROUP_DIM:gp0 + (g + 1) * POOL_GROUP_DIM].astype(_F32)
        o_ref[:, cols] = (mixed * ps_ref[:, cols] * gate).astype(_BF16)

    for hd in range(N_MEM_HEADS):
        cols = slice(hd * HEAD_DIM, (hd + 1) * HEAD_DIM)
        qm = rest_ref[:, qm0 + hd * HEAD_DIM:qm0 + (hd + 1) * HEAD_DIM]
        s = _nt_dot(qm, mk_ref[:, cols])
        m = jnp.max(s, axis=-1, keepdims=True)
        e = jnp.exp(s - m)
        l = jnp.sum(e, axis=-1, keepdims=True)
        o = jnp.dot(e.astype(_BF16), mv_ref[:, cols], preferred_element_type=_F32) / l
        gate = rest_ref[:, gm0 + hd * HEAD_DIM:gm0 + (hd + 1) * HEAD_DIM].astype(_F32)
        o_ref[:, POOL_WIDTH + hd * HEAD_DIM:POOL_WIDTH + (hd + 1) * HEAD_DIM] = (o * gate).astype(_BF16)


def _pool_mem(rest, mk, mv, wp_bf16, pool_scale, *, seq):
    N = rest.shape[0]
    tm = POOL_MEM_ROWS
    tiles_per_batch = seq // tm
    halo_blocks_per_tile = tm // POOL_HALO
    M = mk.shape[1]
    kern = functools.partial(_pool_mem_kernel, tiles_per_batch=tiles_per_batch)
    return pl.pallas_call(
        kern,
        grid=(N // tm,),
        in_specs=[
            pl.BlockSpec((tm, _REST_W), lambda t: (t, 0)),
            pl.BlockSpec((POOL_HALO, POOL_WIDTH),
                         lambda t: (jnp.maximum(t * halo_blocks_per_tile - 1, 0), 0)),
            pl.BlockSpec((None, M, MEM_WIDTH), lambda t: (t // tiles_per_batch, 0, 0)),
            pl.BlockSpec((None, M, MEM_WIDTH), lambda t: (t // tiles_per_batch, 0, 0)),
            pl.BlockSpec(wp_bf16.shape, lambda t: (0, 0, 0)),
            pl.BlockSpec((1, POOL_WIDTH), lambda t: (0, 0)),
        ],
        out_specs=pl.BlockSpec((tm, POOL_WIDTH + MEM_WIDTH), lambda t: (t, 0)),
        out_shape=jax.ShapeDtypeStruct((N, POOL_WIDTH + MEM_WIDTH), _BF16),
        compiler_params=pltpu.CompilerParams(dimension_semantics=("arbitrary",)),
        name="pool_mem",
    )(rest, rest, mk, mv, wp_bf16, pool_scale)


def _out_proj_kernel(ya_ref, ypm_ref, x_ref, w_ref, fg_ref, o_ref, *, final_norm):
    for r0 in range(0, x_ref.shape[0], OUT_PROJ_SUBROWS):
        rows = slice(r0, r0 + OUT_PROJ_SUBROWS)
        y = jnp.dot(ya_ref[rows, :], w_ref[0:MOBA_WIDTH, :], preferred_element_type=_F32)
        y = y + jnp.dot(ypm_ref[rows, :], w_ref[MOBA_WIDTH:, :], preferred_element_type=_F32)
        r = x_ref[rows, :] + y
        o_ref[rows, :] = _rmsnorm(r, fg_ref[...]) if final_norm else r


def _out_proj(ya, ypm, xf, w_bf16, final_g, *, final_norm):
    N, D = xf.shape
    tm = OUT_PROJ_ROWS
    kern = functools.partial(_out_proj_kernel, final_norm=final_norm)
    return pl.pallas_call(
        kern,
        grid=(N // tm,),
        in_specs=[
            pl.BlockSpec((tm, MOBA_WIDTH), lambda t: (t, 0)),
            pl.BlockSpec((tm, POOL_WIDTH + MEM_WIDTH), lambda t: (t, 0)),
            pl.BlockSpec((tm, D), lambda t: (t, 0)),
            pl.BlockSpec(w_bf16.shape, lambda t: (0, 0), pipeline_mode=pl.Buffered(1)),
            pl.BlockSpec((1, D), lambda t: (0, 0)),
        ],
        out_specs=pl.BlockSpec((tm, D), lambda t: (t, 0)),
        out_shape=jax.ShapeDtypeStruct((N, D), _F32),
        compiler_params=pltpu.CompilerParams(
            dimension_semantics=("arbitrary",), vmem_limit_bytes=V7X_VMEM_LIMIT_BYTES),
        name="out_proj",
    )(ya, ypm, xf, w_bf16, final_g)


def _layer(xf, mem, norm_g, mem_norm_g, w_in, w_mem_kv, w_pool, pool_scale, w_out, final_g,
           *, batch, seq, final_norm):
    mk, mv = _mem_kv(mem, mem_norm_g[None, :], w_mem_kv.astype(_BF16))
    q, k, vt, sga, rest, bias = _in_proj(xf, norm_g[None, :], w_in.astype(_BF16), batch=batch, seq=seq)
    ya = _moba(q, k, vt, bias, sga, batch=batch, seq=seq)
    ypm = _pool_mem(rest, mk, mv, w_pool.astype(_BF16), pool_scale[None, :], seq=seq)
    return _out_proj(ya, ypm, xf, w_out.astype(_BF16), final_g[None, :], final_norm=final_norm)


def kernel(x, mem, norm_g, mem_norm_g, w_in, w_mem_kv, w_pool, pool_scale, w_out, final_norm_g):
    batch, seq, d_model = x.shape
    depth = norm_g.shape[0]
    assert seq % MOBA_BLOCK == 0 and seq % IN_PROJ_ROWS == 0
    assert seq % POOL_MEM_ROWS == 0 and seq % OUT_PROJ_ROWS == 0
    assert w_in.shape[2] == 4 * MOBA_WIDTH + 2 * POOL_WIDTH + 2 * MEM_WIDTH
    xf = x.reshape(batch * seq, d_model)
    for l in range(depth):
        xf = _layer(xf, mem, norm_g[l], mem_norm_g[l], w_in[l], w_mem_kv[l], w_pool[l],
                    pool_scale[l], w_out[l], final_norm_g,
                    batch=batch, seq=seq, final_norm=(l == depth - 1))
    return xf.reshape(batch, seq, d_model)
```

```python
import functools

import jax
import jax.numpy as jnp
from jax import lax
from jax.experimental import pallas as pl
from jax.experimental.pallas import tpu as pltpu

HEAD_DIM = 128
N_MOBA_HEADS = 8
MOBA_WIDTH = N_MOBA_HEADS * HEAD_DIM
N_MEM_HEADS = 4
MEM_WIDTH = N_MEM_HEADS * HEAD_DIM
POOL_WINDOWS = (2, 4, 8, 16)
POOL_GROUP_DIM = 128
POOL_WIDTH = len(POOL_WINDOWS) * POOL_GROUP_DIM
POOL_HALO = 16
MOBA_BLOCK = 256
MOBA_TOPK = 3
EPS = 1e-6
SCALE = HEAD_DIM ** -0.5
LOG2_E = 1.4426950408889634
MOBA_SUM_ROWS = 16

_Q0, _K0, _V0, _GA0 = 0, MOBA_WIDTH, 2 * MOBA_WIDTH, 3 * MOBA_WIDTH
_REST0 = 4 * MOBA_WIDTH
_REST_W = 2 * POOL_WIDTH + 2 * MEM_WIDTH

V7X_VMEM_LIMIT_BYTES = 56 * 1024 * 1024

IN_PROJ_ROWS = 512
POOL_MEM_ROWS = 512
OUT_PROJ_ROWS = 512
OUT_PROJ_SUBROWS = 256

_F32 = jnp.float32
_BF16 = jnp.bfloat16
_NEG_INF = float("-inf")


def _nt_dot(a, b):
    return lax.dot_general(a, b, (((1,), (1,)), ((), ())), preferred_element_type=_F32)


def _split_bf16(a):
    hi = a.astype(_BF16)
    lo = (a - hi.astype(_F32)).astype(_BF16)
    return hi, lo


def _nt_dot_f32(a, b):
    a_hi, a_lo = _split_bf16(a)
    b_hi, b_lo = _split_bf16(b)
    return _nt_dot(a_hi, b_hi) + (_nt_dot(a_hi, b_lo) + _nt_dot(a_lo, b_hi))


def _silu(a):
    return a * (1.0 / (1.0 + jnp.exp(-a)))


def _rmsnorm(x, g):
    ms = jnp.mean(x * x, axis=-1, keepdims=True)
    return x * lax.rsqrt(ms + EPS) * g


def _mem_kv_kernel(mem_ref, g_ref, w_ref, mk_ref, mv_ref):
    h = _rmsnorm(mem_ref[...], g_ref[...]).astype(_BF16)
    kv = jnp.dot(h, w_ref[...], preferred_element_type=_F32)
    mk_ref[...] = kv[:, :MEM_WIDTH].astype(_BF16)
    mv_ref[...] = kv[:, MEM_WIDTH:].astype(_BF16)


def _mem_kv(mem, g, w_bf16):
    B, M, D = mem.shape
    return pl.pallas_call(
        _mem_kv_kernel,
        grid=(B,),
        in_specs=[
            pl.BlockSpec((None, M, D), lambda b: (b, 0, 0)),
            pl.BlockSpec((1, D), lambda b: (0, 0)),
            pl.BlockSpec((D, 2 * MEM_WIDTH), lambda b: (0, 0)),
        ],
        out_specs=[
            pl.BlockSpec((None, M, MEM_WIDTH), lambda b: (b, 0, 0)),
            pl.BlockSpec((None, M, MEM_WIDTH), lambda b: (b, 0, 0)),
        ],
        out_shape=[jax.ShapeDtypeStruct((B, M, MEM_WIDTH), _BF16)] * 2,
        compiler_params=pltpu.CompilerParams(dimension_semantics=("arbitrary",)),
        name="mem_kv",
    )(mem, g, w_bf16)


def _in_proj_kernel(x_ref, g_ref, w_ref,
                    q_ref, k_ref, vt_ref, sga_ref, rest_ref, bias_ref,
                    kmean_sc, *, tiles_per_batch, n_blocks):
    tm = x_ref.shape[0]
    blocks_per_tile = tm // MOBA_BLOCK
    t = pl.program_id(0)
    first_block = lax.rem(t, tiles_per_batch) * blocks_per_tile

    @pl.when(t == 0)
    def _():
        kmean_sc[...] = jnp.zeros_like(kmean_sc)

    h = _rmsnorm(x_ref[...], g_ref[...]).astype(_BF16)

    k = jnp.dot(h, w_ref[:, _K0:_K0 + MOBA_WIDTH], preferred_element_type=_F32)
    k_ref[...] = k.astype(_BF16)
    for b in range(blocks_per_tile):
        kmean_sc[pl.ds(first_block + b, 1), :] = jnp.mean(
            k[b * MOBA_BLOCK:(b + 1) * MOBA_BLOCK], axis=0, keepdims=True)

    q = jnp.dot(h, w_ref[:, _Q0:_Q0 + MOBA_WIDTH], preferred_element_type=_F32)
    q_ref[...] = (q * (SCALE * LOG2_E)).astype(_BF16)

    n_idx = lax.broadcasted_iota(jnp.int32, (n_blocks, tm), 0)
    col = lax.broadcasted_iota(jnp.int32, (n_blocks, tm), 1)
    own_block = first_block + col // MOBA_BLOCK
    is_past = n_idx < own_block
    for hd in range(N_MOBA_HEADS):
        cols = slice(hd * HEAD_DIM, (hd + 1) * HEAD_DIM)
        gate = jnp.where(is_past, _nt_dot_f32(kmean_sc[:, cols], q[:, cols]), _NEG_INF)
        keep = None
        for _ in range(MOBA_TOPK):
            best = jnp.max(gate, axis=0, keepdims=True)
            first = jnp.min(jnp.where(gate == best, n_idx, n_blocks), axis=0, keepdims=True)
            pick = n_idx == first
            keep = pick if keep is None else keep | pick
            gate = jnp.where(pick, _NEG_INF, gate)
        bias_ref[hd] = jnp.where(keep & is_past, 0.0, _NEG_INF)

    v = jnp.dot(h, w_ref[:, _V0:_V0 + MOBA_WIDTH], preferred_element_type=_F32)
    for b in range(blocks_per_tile):
        vt_ref[b] = v[b * MOBA_BLOCK:(b + 1) * MOBA_BLOCK].T.astype(_BF16)

    ga = jnp.dot(h, w_ref[:, _GA0:_GA0 + MOBA_WIDTH], preferred_element_type=_F32)
    sga_ref[...] = _silu(ga).astype(_BF16)

    rest = jnp.dot(h, w_ref[:, _REST0:_REST0 + _REST_W], preferred_element_type=_F32)
    u0, gp0, qm0, gm0 = 0, POOL_WIDTH, 2 * POOL_WIDTH, 2 * POOL_WIDTH + MEM_WIDTH
    rest_ref[:, u0:gp0] = rest[:, u0:gp0].astype(_BF16)
    rest_ref[:, gp0:qm0] = _silu(rest[:, gp0:qm0]).astype(_BF16)
    rest_ref[:, qm0:gm0] = (rest[:, qm0:gm0] * SCALE).astype(_BF16)
    rest_ref[:, gm0:] = _silu(rest[:, gm0:]).astype(_BF16)


def _in_proj(xf, g, w_bf16, *, batch, seq):
    N, D = xf.shape
    tm = IN_PROJ_ROWS
    tiles_per_batch = seq // tm
    blocks_per_tile = tm // MOBA_BLOCK
    n_blocks = seq // MOBA_BLOCK
    resident = pl.Buffered(1)
    kern = functools.partial(_in_proj_kernel, tiles_per_batch=tiles_per_batch, n_blocks=n_blocks)
    return pl.pallas_call(
        kern,
        grid=(N // tm,),
        in_specs=[
            pl.BlockSpec((tm, D), lambda t: (t, 0)),
            pl.BlockSpec((1, D), lambda t: (0, 0)),
            pl.BlockSpec(w_bf16.shape, lambda t: (0, 0), pipeline_mode=resident),
        ],
        out_specs=[
            pl.BlockSpec((tm, MOBA_WIDTH), lambda t: (t, 0)),
            pl.BlockSpec((tm, MOBA_WIDTH), lambda t: (t, 0)),
            pl.BlockSpec((None, blocks_per_tile, MOBA_WIDTH, MOBA_BLOCK),
                         lambda t: (t // tiles_per_batch, t % tiles_per_batch, 0, 0)),
            pl.BlockSpec((tm, MOBA_WIDTH), lambda t: (t, 0)),
            pl.BlockSpec((tm, _REST_W), lambda t: (t, 0)),
            pl.BlockSpec((None, N_MOBA_HEADS, n_blocks, tm),
                         lambda t: (t // tiles_per_batch, 0, 0, t % tiles_per_batch)),
        ],
        out_shape=[
            jax.ShapeDtypeStruct((N, MOBA_WIDTH), _BF16),
            jax.ShapeDtypeStruct((N, MOBA_WIDTH), _BF16),
            jax.ShapeDtypeStruct((batch, n_blocks, MOBA_WIDTH, MOBA_BLOCK), _BF16),
            jax.ShapeDtypeStruct((N, MOBA_WIDTH), _BF16),
            jax.ShapeDtypeStruct((N, _REST_W), _BF16),
            jax.ShapeDtypeStruct((batch, N_MOBA_HEADS, n_blocks, seq), _F32),
        ],
        scratch_shapes=[pltpu.VMEM((n_blocks, MOBA_WIDTH), _F32)],
        compiler_params=pltpu.CompilerParams(
            dimension_semantics=("arbitrary",), vmem_limit_bytes=V7X_VMEM_LIMIT_BYTES),
        name="in_proj",
    )(xf, g, w_bf16)


def _moba_kernel(q_ref, k_hbm, vt_hbm, bias_ref, sga_ref, o_ref,
                 k_ref, vt_ref, k_sem, vt_sem, qt_sc, s_sc, smax_sc, m_sc, acc_sc):
    tq = q_ref.shape[0]
    b = pl.program_id(0)
    i = pl.program_id(1)
    n_blocks = vt_ref.shape[0]
    ones_rows = jnp.ones((MOBA_SUM_ROWS, MOBA_BLOCK), _BF16)
    zero_bias = jnp.zeros((1, tq), _F32)

    def head_cols(hd):
        return slice(hd * HEAD_DIM, (hd + 1) * HEAD_DIM)

    def kv_block_copies(j):
        rows = pl.ds(pl.multiple_of(j * MOBA_BLOCK, MOBA_BLOCK), MOBA_BLOCK)
        return (pltpu.make_async_copy(k_hbm.at[b, rows, :], k_ref.at[rows, :], k_sem.at[j % 2]),
                pltpu.make_async_copy(vt_hbm.at[b, j], vt_ref.at[j], vt_sem.at[j % 2]))

    def start_kv_block(j):
        for copy in kv_block_copies(j):
            copy.start()

    def wait_kv_block(j):
        for copy in kv_block_copies(j):
            copy.wait()

    @pl.when(i == 0)
    def _():
        start_kv_block(0)
        start_kv_block(1)
        wait_kv_block(0)
        wait_kv_block(1)

    @pl.when((i > 0) & (i + 1 < n_blocks))
    def _():
        wait_kv_block(i + 1)

    @pl.when(i + 2 < n_blocks)
    def _():
        start_kv_block(i + 2)

    def produce_head(hd, j, slot, mask):
        start = pl.multiple_of(j * MOBA_BLOCK, MOBA_BLOCK)
        s = mask(jnp.dot(k_ref[pl.ds(start, MOBA_BLOCK), head_cols(hd)], qt_sc[hd],
                         preferred_element_type=_F32))
        s_sc[slot, hd] = s
        smax_sc[slot, pl.ds(hd, 1), :] = jnp.max(s, axis=0, keepdims=True)

    def consume_head(hd, blocks):
        row = pl.ds(hd, 1)
        biases = [zero_bias if own else bias_ref[hd, pl.ds(j, 1), :]
                  for j, _, own in blocks]
        m = m_sc[row, :]
        m_new = m
        for (_, slot, _), bias in zip(blocks, biases):
            m_new = jnp.maximum(m_new, smax_sc[slot, row, :] + bias)
        alpha = jnp.exp2(m - m_new)
        m_sc[row, :] = m_new
        pv = None
        for (j, slot, _), bias in zip(blocks, biases):
            p = jnp.exp2(s_sc[slot, hd] - (m_new - bias))
            v_and_ones = jnp.concatenate([vt_ref[j, head_cols(hd), :], ones_rows], axis=0)
            d = jnp.dot(v_and_ones, p.astype(_BF16), preferred_element_type=_F32)
            pv = d if pv is None else pv + d
        acc_sc[hd] = alpha * acc_sc[hd] + pv

    m_sc[...] = jnp.full_like(m_sc, _NEG_INF)
    acc_sc[...] = jnp.zeros_like(acc_sc)
    for hd in range(N_MOBA_HEADS):
        qt_sc[hd] = q_ref[:, head_cols(hd)].T

    n_pairs = jnp.maximum((i + 1) // 2, 1)

    def pair_blocks(pair):
        return [(2 * pair, 0, False), (2 * pair + 1, 1, False)]

    kpos = lax.broadcasted_iota(jnp.int32, (MOBA_BLOCK, tq), 0)
    qpos = lax.broadcasted_iota(jnp.int32, (MOBA_BLOCK, tq), 1)
    causal = kpos <= qpos
    for hd in range(N_MOBA_HEADS):
        produce_head(hd, i, 2, lambda s: jnp.where(causal, s, _NEG_INF))
        produce_head(hd, 0, 0, lambda s: s)
        produce_head(hd, 1, 1, lambda s: s)
    for hd in range(N_MOBA_HEADS):
        consume_head(hd, [(i, 2, True)])

    def step(pair):
        for hd in range(N_MOBA_HEADS):
            consume_head(hd, pair_blocks(pair - 1))
            produce_head(hd, 2 * pair, 0, lambda s: s)
            produce_head(hd, 2 * pair + 1, 1, lambda s: s)

    def body(it, carry):
        step(2 * it + 1)
        step(2 * it + 2)
        return carry

    n_steps = n_pairs - 1
    lax.fori_loop(0, n_steps // 2, body, 0)

    @pl.when(n_steps % 2 == 1)
    def _():
        step(n_steps)

    for hd in range(N_MOBA_HEADS):
        consume_head(hd, pair_blocks(n_pairs - 1))

    for hd in range(N_MOBA_HEADS):
        l = acc_sc[hd, HEAD_DIM:HEAD_DIM + 1, :]
        o = (acc_sc[hd, 0:HEAD_DIM, :] / l).T
        o_ref[:, head_cols(hd)] = (o * sga_ref[:, head_cols(hd)].astype(_F32)).astype(_BF16)


def _moba(q, k, vt, bias, sga, *, batch, seq):
    N = q.shape[0]
    n_blocks = seq // MOBA_BLOCK
    tq = MOBA_BLOCK
    k3 = k.reshape(batch, seq, MOBA_WIDTH)
    row_block = lambda b, i: (b * n_blocks + i, 0)
    return pl.pallas_call(
        _moba_kernel,
        grid=(batch, n_blocks),
        in_specs=[
            pl.BlockSpec((tq, MOBA_WIDTH), row_block),
            pl.BlockSpec(memory_space=pl.ANY),
            pl.BlockSpec(memory_space=pl.ANY),
            pl.BlockSpec((None, N_MOBA_HEADS, n_blocks, tq), lambda b, i: (b, 0, 0, i)),
            pl.BlockSpec((tq, MOBA_WIDTH), row_block),
        ],
        out_specs=pl.BlockSpec((tq, MOBA_WIDTH), row_block),
        out_shape=jax.ShapeDtypeStruct((N, MOBA_WIDTH), _BF16),
        scratch_shapes=[
            pltpu.VMEM((seq, MOBA_WIDTH), _BF16),
            pltpu.VMEM((n_blocks, MOBA_WIDTH, MOBA_BLOCK), _BF16),
            pltpu.SemaphoreType.DMA((2,)),
            pltpu.SemaphoreType.DMA((2,)),
            pltpu.VMEM((N_MOBA_HEADS, HEAD_DIM, tq), _BF16),
            pltpu.VMEM((3, N_MOBA_HEADS, MOBA_BLOCK, tq), _F32),
            pltpu.VMEM((3, N_MOBA_HEADS, tq), _F32),
            pltpu.VMEM((N_MOBA_HEADS, tq), _F32),
            pltpu.VMEM((N_MOBA_HEADS, HEAD_DIM + MOBA_SUM_ROWS, tq), _F32),
        ],
        compiler_params=pltpu.CompilerParams(
            dimension_semantics=("arbitrary", "arbitrary"), vmem_limit_bytes=V7X_VMEM_LIMIT_BYTES),
        name="moba",
    )(q, k3, vt, bias, sga)


def _pool_mem_kernel(rest_ref, halo_ref, mk_ref, mv_ref, wp_ref, ps_ref, o_ref, *, tiles_per_batch):
    tm = rest_ref.shape[0]
    tb = lax.rem(pl.program_id(0), tiles_per_batch)
    u0, gp0, qm0, gm0 = 0, POOL_WIDTH, 2 * POOL_WIDTH, 2 * POOL_WIDTH + MEM_WIDTH

    u = rest_ref[:, u0:gp0].astype(_F32)
    halo = jnp.where(tb != 0, halo_ref[...].astype(_F32), 0.0)
    pos = tb * tm + lax.broadcasted_iota(jnp.int32, (tm, POOL_GROUP_DIM), 0)
    for g, w in enumerate(POOL_WINDOWS):
        cols = slice(g * POOL_GROUP_DIM, (g + 1) * POOL_GROUP_DIM)
        u_g = u[:, cols]
        run = jnp.concatenate([halo[:, cols], u_g], axis=0)
        span = 1
        while span < w:
            run = run + pltpu.roll(run, span, axis=0)
            span *= 2
        win = run[POOL_HALO:]
        cnt = jnp.minimum(pos + 1, w).astype(_F32)
        pooled = win / cnt - u_g
        mixed = jnp.dot(pooled.astype(_BF16), wp_ref[g], preferred_element_type=_F32)
        gate = rest_ref[:, gp0 + g * POOL_GROUP_DIM:gp0 + (g + 1) * POOL_GROUP_DIM].astype(_F32)
        o_ref[:, cols] = (mixed * ps_ref[:, cols] * gate).astype(_BF16)

    for hd in range(N_MEM_HEADS):
        cols = slice(hd * HEAD_DIM, (hd + 1) * HEAD_DIM)
        qm = rest_ref[:, qm0 + hd * HEAD_DIM:qm0 + (hd + 1) * HEAD_DIM]
        s = _nt_dot(qm, mk_ref[:, cols])
        m = jnp.max(s, axis=-1, keepdims=True)
        e = jnp.exp(s - m)
        l = jnp.sum(e, axis=-1, keepdims=True)
        o = jnp.dot(e.astype(_BF16), mv_ref[:, cols], preferred_element_type=_F32) / l
        gate = rest_ref[:, gm0 + hd * HEAD_DIM:gm0 + (hd + 1) * HEAD_DIM].astype(_F32)
        o_ref[:, POOL_WIDTH + hd * HEAD_DIM:POOL_WIDTH + (hd + 1) * HEAD_DIM] = (o * gate).astype(_BF16)


def _pool_mem(rest, mk, mv, wp_bf16, pool_scale, *, seq):
    N = rest.shape[0]
    tm = POOL_MEM_ROWS
    tiles_per_batch = seq // tm
    halo_blocks_per_tile = tm // POOL_HALO
    M = mk.shape[1]
    kern = functools.partial(_pool_mem_kernel, tiles_per_batch=tiles_per_batch)
    return pl.pallas_call(
        kern,
        grid=(N // tm,),
        in_specs=[
            pl.BlockSpec((tm, _REST_W), lambda t: (t, 0)),
            pl.BlockSpec((POOL_HALO, POOL_WIDTH),
                         lambda t: (jnp.maximum(t * halo_blocks_per_tile - 1, 0), 0)),
            pl.BlockSpec((None, M, MEM_WIDTH), lambda t: (t // tiles_per_batch, 0, 0)),
            pl.BlockSpec((None, M, MEM_WIDTH), lambda t: (t // tiles_per_batch, 0, 0)),
            pl.BlockSpec(wp_bf16.shape, lambda t: (0, 0, 0)),
            pl.BlockSpec((1, POOL_WIDTH), lambda t: (0, 0)),
        ],
        out_specs=pl.BlockSpec((tm, POOL_WIDTH + MEM_WIDTH), lambda t: (t, 0)),
        out_shape=jax.ShapeDtypeStruct((N, POOL_WIDTH + MEM_WIDTH), _BF16),
        compiler_params=pltpu.CompilerParams(dimension_semantics=("arbitrary",)),
        name="pool_mem",
    )(rest, rest, mk, mv, wp_bf16, pool_scale)


def _out_proj_kernel(ya_ref, ypm_ref, x_ref, w_ref, fg_ref, o_ref, *, final_norm):
    for r0 in range(0, x_ref.shape[0], OUT_PROJ_SUBROWS):
        rows = slice(r0, r0 + OUT_PROJ_SUBROWS)
        y = jnp.dot(ya_ref[rows, :], w_ref[0:MOBA_WIDTH, :], preferred_element_type=_F32)
        y = y + jnp.dot(ypm_ref[rows, :], w_ref[MOBA_WIDTH:, :], preferred_element_type=_F32)
        r = x_ref[rows, :] + y
        o_ref[rows, :] = _rmsnorm(r, fg_ref[...]) if final_norm else r


def _out_proj(ya, ypm, xf, w_bf16, final_g, *, final_norm):
    N, D = xf.shape
    tm = OUT_PROJ_ROWS
    kern = functools.partial(_out_proj_kernel, final_norm=final_norm)
    return pl.pallas_call(
        kern,
        grid=(N // tm,),
        in_specs=[
            pl.BlockSpec((tm, MOBA_WIDTH), lambda t: (t, 0)),
            pl.BlockSpec((tm, POOL_WIDTH + MEM_WIDTH), lambda t: (t, 0)),
            pl.BlockSpec((tm, D), lambda t: (t, 0)),
            pl.BlockSpec(w_bf16.shape, lambda t: (0, 0), pipeline_mode=pl.Buffered(1)),
            pl.BlockSpec((1, D), lambda t: (0, 0)),
        ],
        out_specs=pl.BlockSpec((tm, D), lambda t: (t, 0)),
        out_shape=jax.ShapeDtypeStruct((N, D), _F32),
        compiler_params=pltpu.CompilerParams(
            dimension_semantics=("arbitrary",), vmem_limit_bytes=V7X_VMEM_LIMIT_BYTES),
        name="out_proj",
    )(ya, ypm, xf, w_bf16, final_g)


def _layer(xf, mem, norm_g, mem_norm_g, w_in, w_mem_kv, w_pool, pool_scale, w_out, final_g,
           *, batch, seq, final_norm):
    mk, mv = _mem_kv(mem, mem_norm_g[None, :], w_mem_kv.astype(_BF16))
    q, k, vt, sga, rest, bias = _in_proj(xf, norm_g[None, :], w_in.astype(_BF16), batch=batch, seq=seq)
    ya = _moba(q, k, vt, bias, sga, batch=batch, seq=seq)
    ypm = _pool_mem(rest, mk, mv, w_pool.astype(_BF16), pool_scale[None, :], seq=seq)
    return _out_proj(ya, ypm, xf, w_out.astype(_BF16), final_g[None, :], final_norm=final_norm)


def kernel(x, mem, norm_g, mem_norm_g, w_in, w_mem_kv, w_pool, pool_scale, w_out, final_norm_g):
    batch, seq, d_model = x.shape
    depth = norm_g.shape[0]
    assert seq % MOBA_BLOCK == 0 and seq % IN_PROJ_ROWS == 0
    assert seq % POOL_MEM_ROWS == 0 and seq % OUT_PROJ_ROWS == 0
    assert w_in.shape[2] == 4 * MOBA_WIDTH + 2 * POOL_WIDTH + 2 * MEM_WIDTH
    xf = x.reshape(batch * seq, d_model)
    for l in range(depth):
        xf = _layer(xf, mem, norm_g[l], mem_norm_g[l], w_in[l], w_mem_kv[l], w_pool[l],
                    pool_scale[l], w_out[l], final_norm_g,
                    batch=batch, seq=seq, final_norm=(l == depth - 1))
    return xf.reshape(batch, seq, d_model)
```

```python
import functools

import jax
import jax.numpy as jnp
from jax import lax
from jax.experimental import pallas as pl
from jax.experimental.pallas import tpu as pltpu

HEAD_DIM = 128
N_MOBA_HEADS = 8
MOBA_WIDTH = N_MOBA_HEADS * HEAD_DIM
N_MEM_HEADS = 4
MEM_WIDTH = N_MEM_HEADS * HEAD_DIM
POOL_WINDOWS = (2, 4, 8, 16)
POOL_GROUP_DIM = 128
POOL_WIDTH = len(POOL_WINDOWS) * POOL_GROUP_DIM
POOL_HALO = 16
MOBA_BLOCK = 256
MOBA_TOPK = 3
EPS = 1e-6
SCALE = HEAD_DIM ** -0.5
LOG2_E = 1.4426950408889634
MOBA_SUM_ROWS = 16

_Q0, _K0, _V0, _GA0 = 0, MOBA_WIDTH, 2 * MOBA_WIDTH, 3 * MOBA_WIDTH
_REST0 = 4 * MOBA_WIDTH
_REST_W = 2 * POOL_WIDTH + 2 * MEM_WIDTH

V7X_VMEM_LIMIT_BYTES = 56 * 1024 * 1024

IN_PROJ_ROWS = 512
POOL_MEM_ROWS = 512
OUT_PROJ_ROWS = 512
OUT_PROJ_SUBROWS = 256

_F32 = jnp.float32
_BF16 = jnp.bfloat16
_NEG_INF = float("-inf")


def _nt_dot(a, b):
    return lax.dot_general(a, b, (((1,), (1,)), ((), ())), preferred_element_type=_F32)


def _split_bf16(a):
    hi = a.astype(_BF16)
    lo = (a - hi.astype(_F32)).astype(_BF16)
    return hi, lo


def _nt_dot_f32(a, b):
    a_hi, a_lo = _split_bf16(a)
    b_hi, b_lo = _split_bf16(b)
    return _nt_dot(a_hi, b_hi) + (_nt_dot(a_hi, b_lo) + _nt_dot(a_lo, b_hi))


def _silu(a):
    return a * (1.0 / (1.0 + jnp.exp(-a)))


def _rmsnorm(x, g):
    ms = jnp.mean(x * x, axis=-1, keepdims=True)
    return x * lax.rsqrt(ms + EPS) * g


def _mem_kv_kernel(mem_ref, g_ref, w_ref, mk_ref, mv_ref):
    h = _rmsnorm(mem_ref[...], g_ref[...]).astype(_BF16)
    kv = jnp.dot(h, w_ref[...], preferred_element_type=_F32)
    mk_ref[...] = kv[:, :MEM_WIDTH].astype(_BF16)
    mv_ref[...] = kv[:, MEM_WIDTH:].astype(_BF16)


def _mem_kv(mem, g, w_bf16):
    B, M, D = mem.shape
    return pl.pallas_call(
        _mem_kv_kernel,
        grid=(B,),
        in_specs=[
            pl.BlockSpec((None, M, D), lambda b: (b, 0, 0)),
            pl.BlockSpec((1, D), lambda b: (0, 0)),
            pl.BlockSpec((D, 2 * MEM_WIDTH), lambda b: (0, 0)),
        ],
        out_specs=[
            pl.BlockSpec((None, M, MEM_WIDTH), lambda b: (b, 0, 0)),
            pl.BlockSpec((None, M, MEM_WIDTH), lambda b: (b, 0, 0)),
        ],
        out_shape=[jax.ShapeDtypeStruct((B, M, MEM_WIDTH), _BF16)] * 2,
        compiler_params=pltpu.CompilerParams(dimension_semantics=("arbitrary",)),
        name="mem_kv",
    )(mem, g, w_bf16)


def _in_proj_kernel(x_ref, g_ref, w_ref,
                    q_ref, k_ref, vt_ref, sga_ref, rest_ref, bias_ref,
                    kmean_sc, *, tiles_per_batch, n_blocks):
    tm = x_ref.shape[0]
    blocks_per_tile = tm // MOBA_BLOCK
    t = pl.program_id(0)
    first_block = lax.rem(t, tiles_per_batch) * blocks_per_tile

    @pl.when(t == 0)
    def _():
        kmean_sc[...] = jnp.zeros_like(kmean_sc)

    h = _rmsnorm(x_ref[...], g_ref[...]).astype(_BF16)

    k = jnp.dot(h, w_ref[:, _K0:_K0 + MOBA_WIDTH], preferred_element_type=_F32)
    k_ref[...] = k.astype(_BF16)
    for b in range(blocks_per_tile):
        kmean_sc[pl.ds(first_block + b, 1), :] = jnp.mean(
            k[b * MOBA_BLOCK:(b + 1) * MOBA_BLOCK], axis=0, keepdims=True)

    q = jnp.dot(h, w_ref[:, _Q0:_Q0 + MOBA_WIDTH], preferred_element_type=_F32)
    q_ref[...] = (q * (SCALE * LOG2_E)).astype(_BF16)

    n_idx = lax.broadcasted_iota(jnp.int32, (n_blocks, tm), 0)
    col = lax.broadcasted_iota(jnp.int32, (n_blocks, tm), 1)
    own_block = first_block + col // MOBA_BLOCK
    is_past = n_idx < own_block
    for hd in range(N_MOBA_HEADS):
        cols = slice(hd * HEAD_DIM, (hd + 1) * HEAD_DIM)
        gate = jnp.where(is_past, _nt_dot_f32(kmean_sc[:, cols], q[:, cols]), _NEG_INF)
        keep = None
        for _ in range(MOBA_TOPK):
            best = jnp.max(gate, axis=0, keepdims=True)
            first = jnp.min(jnp.where(gate == best, n_idx, n_blocks), axis=0, keepdims=True)
            pick = n_idx == first
            keep = pick if keep is None else keep | pick
            gate = jnp.where(pick, _NEG_INF, gate)
        bias_ref[hd] = jnp.where(keep & is_past, 0.0, _NEG_INF)

    v = jnp.dot(h, w_ref[:, _V0:_V0 + MOBA_WIDTH], preferred_element_type=_F32)
    for b in range(blocks_per_tile):
        vt_ref[b] = v[b * MOBA_BLOCK:(b + 1) * MOBA_BLOCK].T.astype(_BF16)

    ga = jnp.dot(h, w_ref[:, _GA0:_GA0 + MOBA_WIDTH], preferred_element_type=_F32)
    sga_ref[...] = _silu(ga).astype(_BF16)

    rest = jnp.dot(h, w_ref[:, _REST0:_REST0 + _REST_W], preferred_element_type=_F32)
    u0, gp0, qm0, gm0 = 0, POOL_WIDTH, 2 * POOL_WIDTH, 2 * POOL_WIDTH + MEM_WIDTH
    rest_ref[:, u0:gp0] = rest[:, u0:gp0].astype(_BF16)
    rest_ref[:, gp0:qm0] = _silu(rest[:, gp0:qm0]).astype(_BF16)
    rest_ref[:, qm0:gm0] = (rest[:, qm0:gm0] * SCALE).astype(_BF16)
    rest_ref[:, gm0:] = _silu(rest[:, gm0:]).astype(_BF16)


def _in_proj(xf, g, w_bf16, *, batch, seq):
    N, D = xf.shape
    tm = IN_PROJ_ROWS
    tiles_per_batch = seq // tm
    blocks_per_tile = tm // MOBA_BLOCK
    n_blocks = seq // MOBA_BLOCK
    resident = pl.Buffered(1)
    kern = functools.partial(_in_proj_kernel, tiles_per_batch=tiles_per_batch, n_blocks=n_blocks)
    return pl.pallas_call(
        kern,
        grid=(N // tm,),
        in_specs=[
            pl.BlockSpec((tm, D), lambda t: (t, 0)),
            pl.BlockSpec((1, D), lambda t: (0, 0)),
            pl.BlockSpec(w_bf16.shape, lambda t: (0, 0), pipeline_mode=resident),
        ],
        out_specs=[
            pl.BlockSpec((tm, MOBA_WIDTH), lambda t: (t, 0)),
            pl.BlockSpec((tm, MOBA_WIDTH), lambda t: (t, 0)),
            pl.BlockSpec((None, blocks_per_tile, MOBA_WIDTH, MOBA_BLOCK),
                         lambda t: (t // tiles_per_batch, t % tiles_per_batch, 0, 0)),
            pl.BlockSpec((tm, MOBA_WIDTH), lambda t: (t, 0)),
            pl.BlockSpec((tm, _REST_W), lambda t: (t, 0)),
            pl.BlockSpec((None, N_MOBA_HEADS, n_blocks, tm),
                         lambda t: (t // tiles_per_batch, 0, 0, t % tiles_per_batch)),
        ],
        out_shape=[
            jax.ShapeDtypeStruct((N, MOBA_WIDTH), _BF16),
            jax.ShapeDtypeStruct((N, MOBA_WIDTH), _BF16),
            jax.ShapeDtypeStruct((batch, n_blocks, MOBA_WIDTH, MOBA_BLOCK), _BF16),
            jax.ShapeDtypeStruct((N, MOBA_WIDTH), _BF16),
            jax.ShapeDtypeStruct((N, _REST_W), _BF16),
            jax.ShapeDtypeStruct((batch, N_MOBA_HEADS, n_blocks, seq), _F32),
        ],
        scratch_shapes=[pltpu.VMEM((n_blocks, MOBA_WIDTH), _F32)],
        compiler_params=pltpu.CompilerParams(
            dimension_semantics=("arbitrary",), vmem_limit_bytes=V7X_VMEM_LIMIT_BYTES),
        name="in_proj",
    )(xf, g, w_bf16)


def _moba_kernel(q_ref, qn_ref, k_hbm, vt_hbm, bias_ref, sga_ref, o_ref,
                 k_ref, vt_ref, k_sem, vt_sem, qt_sc, s_sc, smax_sc, m_sc, acc_sc):
    tq = q_ref.shape[0]
    b = pl.program_id(0)
    i = pl.program_id(1)
    n_blocks = vt_ref.shape[0]
    ones_rows = jnp.ones((MOBA_SUM_ROWS, MOBA_BLOCK), _BF16)

    def head_cols(hd):
        return slice(hd * HEAD_DIM, (hd + 1) * HEAD_DIM)

    def kv_block_copies(j):
        rows = pl.ds(pl.multiple_of(j * MOBA_BLOCK, MOBA_BLOCK), MOBA_BLOCK)
        return (pltpu.make_async_copy(k_hbm.at[b, rows, :], k_ref.at[rows, :], k_sem.at[j % 2]),
                pltpu.make_async_copy(vt_hbm.at[b, j], vt_ref.at[j], vt_sem.at[j % 2]))

    def start_kv_block(j):
        for copy in kv_block_copies(j):
            copy.start()

    def wait_kv_block(j):
        for copy in kv_block_copies(j):
            copy.wait()

    @pl.when(i == 0)
    def _():
        start_kv_block(0)
        start_kv_block(1)
        wait_kv_block(0)
        wait_kv_block(1)

    @pl.when((i > 0) & (i + 1 < n_blocks))
    def _():
        wait_kv_block(i + 1)

    @pl.when(i + 2 < n_blocks)
    def _():
        start_kv_block(i + 2)

    def produce_head(hd, j, slot, mask=lambda s: s, *, qt):
        start = pl.multiple_of(j * MOBA_BLOCK, MOBA_BLOCK)
        s = mask(jnp.dot(k_ref[pl.ds(start, MOBA_BLOCK), head_cols(hd)], qt[hd],
                         preferred_element_type=_F32))
        s_sc[slot, hd] = s
        smax_sc[slot, pl.ds(hd, 1), :] = jnp.max(s, axis=0, keepdims=True)

    kpos = lax.broadcasted_iota(jnp.int32, (MOBA_BLOCK, tq), 0)
    qpos = lax.broadcasted_iota(jnp.int32, (MOBA_BLOCK, tq), 1)
    causal = kpos <= qpos

    def produce_first_pair_head(hd, own_block, qt):
        produce_head(hd, own_block, 0, lambda s: jnp.where(causal, s, _NEG_INF), qt=qt)
        produce_head(hd, 0, 1, qt=qt)

    n_pairs = (i + 2) // 2

    def pair_blocks(pair):
        return jnp.where(pair == 0, i, 2 * pair - 1), 2 * pair

    def consume_head(hd, pair):
        row = pl.ds(hd, 1)
        blocks = pair_blocks(pair)
        biases = (jnp.where(pair == 0, 0.0, bias_ref[hd, pl.ds(blocks[0], 1), :]),
                  bias_ref[hd, pl.ds(blocks[1], 1), :])
        m = m_sc[row, :]
        m_new = m
        for slot, bias in enumerate(biases):
            m_new = jnp.maximum(m_new, smax_sc[slot, row, :] + bias)
        alpha = jnp.exp2(m - m_new)
        m_sc[row, :] = m_new
        pv = None
        for slot, (j, bias) in enumerate(zip(blocks, biases)):
            p = jnp.exp2(s_sc[slot, hd] - (m_new - bias))
            v_and_ones = jnp.concatenate([vt_ref[j, head_cols(hd), :], ones_rows], axis=0)
            d = jnp.dot(v_and_ones, p.astype(_BF16), preferred_element_type=_F32)
            pv = d if pv is None else pv + d
        acc_sc[hd] = alpha * acc_sc[hd] + pv

    def reset_head(hd):
        m_sc[pl.ds(hd, 1), :] = jnp.full((1, tq), _NEG_INF, _F32)
        acc_sc[hd] = jnp.zeros(acc_sc.shape[1:], _F32)

    qt_cur, qt_next = qt_sc.at[i % 2], qt_sc.at[(i + 1) % 2]

    @pl.when(i == 0)
    def _():
        for hd in range(N_MOBA_HEADS):
            reset_head(hd)
            qt_cur[hd] = q_ref[:, head_cols(hd)].T
            produce_first_pair_head(hd, i, qt_cur)

    def step(pair):
        for hd in range(N_MOBA_HEADS):
            consume_head(hd, pair - 1)
            produce_head(hd, 2 * pair - 1, 0, qt=qt_cur)
            produce_head(hd, 2 * pair, 1, qt=qt_cur)

    def body(it, carry):
        step(2 * it + 1)
        step(2 * it + 2)
        return carry

    n_steps = n_pairs - 1
    lax.fori_loop(0, n_steps // 2, body, 0)

    @pl.when(n_steps % 2 == 1)
    def _():
        step(n_steps)

    next_own = jnp.minimum(i + 1, n_blocks - 1)
    for hd in range(N_MOBA_HEADS):
        qt_next[hd] = qn_ref[:, head_cols(hd)].T
    for hd in range(N_MOBA_HEADS):
        consume_head(hd, n_pairs - 1)
        produce_first_pair_head(hd, next_own, qt_next)

    for hd in range(N_MOBA_HEADS):
        l = acc_sc[hd, HEAD_DIM:HEAD_DIM + 1, :]
        o = (acc_sc[hd, 0:HEAD_DIM, :] / l).T
        o_ref[:, head_cols(hd)] = (o * sga_ref[:, head_cols(hd)].astype(_F32)).astype(_BF16)
        reset_head(hd)


def _moba(q, k, vt, bias, sga, *, batch, seq):
    N = q.shape[0]
    n_blocks = seq // MOBA_BLOCK
    tq = MOBA_BLOCK
    k3 = k.reshape(batch, seq, MOBA_WIDTH)
    row_block = lambda b, i: (b * n_blocks + i, 0)
    next_row_block = lambda b, i: (b * n_blocks + jnp.minimum(i + 1, n_blocks - 1), 0)
    return pl.pallas_call(
        _moba_kernel,
        grid=(batch, n_blocks),
        in_specs=[
            pl.BlockSpec((tq, MOBA_WIDTH), row_block),
            pl.BlockSpec((tq, MOBA_WIDTH), next_row_block),
            pl.BlockSpec(memory_space=pl.ANY),
            pl.BlockSpec(memory_space=pl.ANY),
            pl.BlockSpec((None, N_MOBA_HEADS, n_blocks, tq), lambda b, i: (b, 0, 0, i)),
            pl.BlockSpec((tq, MOBA_WIDTH), row_block),
        ],
        out_specs=pl.BlockSpec((tq, MOBA_WIDTH), row_block),
        out_shape=jax.ShapeDtypeStruct((N, MOBA_WIDTH), _BF16),
        scratch_shapes=[
            pltpu.VMEM((seq, MOBA_WIDTH), _BF16),
            pltpu.VMEM((n_blocks, MOBA_WIDTH, MOBA_BLOCK), _BF16),
            pltpu.SemaphoreType.DMA((2,)),
            pltpu.SemaphoreType.DMA((2,)),
            pltpu.VMEM((2, N_MOBA_HEADS, HEAD_DIM, tq), _BF16),
            pltpu.VMEM((2, N_MOBA_HEADS, MOBA_BLOCK, tq), _F32),
            pltpu.VMEM((2, N_MOBA_HEADS, tq), _F32),
            pltpu.VMEM((N_MOBA_HEADS, tq), _F32),
            pltpu.VMEM((N_MOBA_HEADS, HEAD_DIM + MOBA_SUM_ROWS, tq), _F32),
        ],
        compiler_params=pltpu.CompilerParams(
            dimension_semantics=("arbitrary", "arbitrary"), vmem_limit_bytes=V7X_VMEM_LIMIT_BYTES),
        name="moba",
    )(q, q, k3, vt, bias, sga)


def _pool_mem_kernel(rest_ref, halo_ref, mk_ref, mv_ref, wp_ref, ps_ref, o_ref, *, tiles_per_batch):
    tm = rest_ref.shape[0]
    tb = lax.rem(pl.program_id(0), tiles_per_batch)
    u0, gp0, qm0, gm0 = 0, POOL_WIDTH, 2 * POOL_WIDTH, 2 * POOL_WIDTH + MEM_WIDTH

    u = rest_ref[:, u0:gp0].astype(_F32)
    halo = jnp.where(tb != 0, halo_ref[...].astype(_F32), 0.0)
    pos = tb * tm + lax.broadcasted_iota(jnp.int32, (tm, POOL_GROUP_DIM), 0)
    for g, w in enumerate(POOL_WINDOWS):
        cols = slice(g * POOL_GROUP_DIM, (g + 1) * POOL_GROUP_DIM)
        u_g = u[:, cols]
        run = jnp.concatenate([halo[:, cols], u_g], axis=0)
        span = 1
        while span < w:
            run = run + pltpu.roll(run, span, axis=0)
            span *= 2
        win = run[POOL_HALO:]
        cnt = jnp.minimum(pos + 1, w).astype(_F32)
        pooled = win / cnt - u_g
        mixed = jnp.dot(pooled.astype(_BF16), wp_ref[g], preferred_element_type=_F32)
        gate = rest_ref[:, gp0 + g * POOL_GROUP_DIM:gp0 + (g + 1) * POOL_GROUP_DIM].astype(_F32)
        o_ref[:, cols] = (mixed * ps_ref[:, cols] * gate).astype(_BF16)

    for hd in range(N_MEM_HEADS):
        cols = slice(hd * HEAD_DIM, (hd + 1) * HEAD_DIM)
        qm = rest_ref[:, qm0 + hd * HEAD_DIM:qm0 + (hd + 1) * HEAD_DIM]
        s = _nt_dot(qm, mk_ref[:, cols])
        m = jnp.max(s, axis=-1, keepdims=True)
        e = jnp.exp(s - m)
        l = jnp.sum(e, axis=-1, keepdims=True)
        o = jnp.dot(e.astype(_BF16), mv_ref[:, cols], preferred_element_type=_F32) / l
        gate = rest_ref[:, gm0 + hd * HEAD_DIM:gm0 + (hd + 1) * HEAD_DIM].astype(_F32)
        o_ref[:, POOL_WIDTH + hd * HEAD_DIM:POOL_WIDTH + (hd + 1) * HEAD_DIM] = (o * gate).astype(_BF16)


def _pool_mem(rest, mk, mv, wp_bf16, pool_scale, *, seq):
    N = rest.shape[0]
    tm = POOL_MEM_ROWS
    tiles_per_batch = seq // tm
    halo_blocks_per_tile = tm // POOL_HALO
    M = mk.shape[1]
    kern = functools.partial(_pool_mem_kernel, tiles_per_batch=tiles_per_batch)
    return pl.pallas_call(
        kern,
        grid=(N // tm,),
        in_specs=[
            pl.BlockSpec((tm, _REST_W), lambda t: (t, 0)),
            pl.BlockSpec((POOL_HALO, POOL_WIDTH),
                         lambda t: (jnp.maximum(t * halo_blocks_per_tile - 1, 0), 0)),
            pl.BlockSpec((None, M, MEM_WIDTH), lambda t: (t // tiles_per_batch, 0, 0)),
            pl.BlockSpec((None, M, MEM_WIDTH), lambda t: (t // tiles_per_batch, 0, 0)),
            pl.BlockSpec(wp_bf16.shape, lambda t: (0, 0, 0)),
            pl.BlockSpec((1, POOL_WIDTH), lambda t: (0, 0)),
        ],
        out_specs=pl.BlockSpec((tm, POOL_WIDTH + MEM_WIDTH), lambda t: (t, 0)),
        out_shape=jax.ShapeDtypeStruct((N, POOL_WIDTH + MEM_WIDTH), _BF16),
        compiler_params=pltpu.CompilerParams(dimension_semantics=("arbitrary",)),
        name="pool_mem",
    )(rest, rest, mk, mv, wp_bf16, pool_scale)


def _out_proj_kernel(ya_ref, ypm_ref, x_ref, w_ref, fg_ref, o_ref, *, final_norm):
    for r0 in range(0, x_ref.shape[0], OUT_PROJ_SUBROWS):
        rows = slice(r0, r0 + OUT_PROJ_SUBROWS)
        y = jnp.dot(ya_ref[rows, :], w_ref[0:MOBA_WIDTH, :], preferred_element_type=_F32)
        y = y + jnp.dot(ypm_ref[rows, :], w_ref[MOBA_WIDTH:, :], preferred_element_type=_F32)
        r = x_ref[rows, :] + y
        o_ref[rows, :] = _rmsnorm(r, fg_ref[...]) if final_norm else r


def _out_proj(ya, ypm, xf, w_bf16, final_g, *, final_norm):
    N, D = xf.shape
    tm = OUT_PROJ_ROWS
    kern = functools.partial(_out_proj_kernel, final_norm=final_norm)
    return pl.pallas_call(
        kern,
        grid=(N // tm,),
        in_specs=[
            pl.BlockSpec((tm, MOBA_WIDTH), lambda t: (t, 0)),
            pl.BlockSpec((tm, POOL_WIDTH + MEM_WIDTH), lambda t: (t, 0)),
            pl.BlockSpec((tm, D), lambda t: (t, 0)),
            pl.BlockSpec(w_bf16.shape, lambda t: (0, 0), pipeline_mode=pl.Buffered(1)),
            pl.BlockSpec((1, D), lambda t: (0, 0)),
        ],
        out_specs=pl.BlockSpec((tm, D), lambda t: (t, 0)),
        out_shape=jax.ShapeDtypeStruct((N, D), _F32),
        compiler_params=pltpu.CompilerParams(
            dimension_semantics=("arbitrary",), vmem_limit_bytes=V7X_VMEM_LIMIT_BYTES),
        name="out_proj",
    )(ya, ypm, xf, w_bf16, final_g)


def _layer(xf, mem, norm_g, mem_norm_g, w_in, w_mem_kv, w_pool, pool_scale, w_out, final_g,
           *, batch, seq, final_norm):
    mk, mv = _mem_kv(mem, mem_norm_g[None, :], w_mem_kv.astype(_BF16))
    q, k, vt, sga, rest, bias = _in_proj(xf, norm_g[None, :], w_in.astype(_BF16), batch=batch, seq=seq)
    ya = _moba(q, k, vt, bias, sga, batch=batch, seq=seq)
    ypm = _pool_mem(rest, mk, mv, w_pool.astype(_BF16), pool_scale[None, :], seq=seq)
    return _out_proj(ya, ypm, xf, w_out.astype(_BF16), final_g[None, :], final_norm=final_norm)


def kernel(x, mem, norm_g, mem_norm_g, w_in, w_mem_kv, w_pool, pool_scale, w_out, final_norm_g):
    batch, seq, d_model = x.shape
    depth = norm_g.shape[0]
    assert seq % MOBA_BLOCK == 0 and seq % IN_PROJ_ROWS == 0
    assert seq % POOL_MEM_ROWS == 0 and seq % OUT_PROJ_ROWS == 0
    assert w_in.shape[2] == 4 * MOBA_WIDTH + 2 * POOL_WIDTH + 2 * MEM_WIDTH
    xf = x.reshape(batch * seq, d_model)
    for l in range(depth):
        xf = _layer(xf, mem, norm_g[l], mem_norm_g[l], w_in[l], w_mem_kv[l], w_pool[l],
                    pool_scale[l], w_out[l], final_norm_g,
                    batch=batch, seq=seq, final_norm=(l == depth - 1))
    return xf.reshape(batch, seq, d_model)
```

```python
import functools

import jax
import jax.numpy as jnp
from jax import lax
from jax.experimental import pallas as pl
from jax.experimental.pallas import tpu as pltpu

HEAD_DIM = 128
N_MOBA_HEADS = 8
MOBA_WIDTH = N_MOBA_HEADS * HEAD_DIM
N_MEM_HEADS = 4
MEM_WIDTH = N_MEM_HEADS * HEAD_DIM
POOL_WINDOWS = (2, 4, 8, 16)
POOL_GROUP_DIM = 128
POOL_WIDTH = len(POOL_WINDOWS) * POOL_GROUP_DIM
POOL_HALO = 16
MOBA_BLOCK = 256
MOBA_TOPK = 3
EPS = 1e-6
SCALE = HEAD_DIM ** -0.5
LOG2_E = 1.4426950408889634
MOBA_SUM_ROWS = 16
MOBA_STEPS_PER_ITERATION = 4

_Q0, _K0, _V0, _GA0 = 0, MOBA_WIDTH, 2 * MOBA_WIDTH, 3 * MOBA_WIDTH
_REST0 = 4 * MOBA_WIDTH
_REST_W = 2 * POOL_WIDTH + 2 * MEM_WIDTH

V7X_VMEM_LIMIT_BYTES = 56 * 1024 * 1024

IN_PROJ_ROWS = 512
POOL_MEM_ROWS = 512
OUT_PROJ_ROWS = 512
OUT_PROJ_SUBROWS = 256

_F32 = jnp.float32
_BF16 = jnp.bfloat16
_NEG_INF = float("-inf")


def _nt_dot(a, b):
    return lax.dot_general(a, b, (((1,), (1,)), ((), ())), preferred_element_type=_F32)


def _split_bf16(a):
    hi = a.astype(_BF16)
    lo = (a - hi.astype(_F32)).astype(_BF16)
    return hi, lo


def _nt_dot_f32(a, b):
    a_hi, a_lo = _split_bf16(a)
    b_hi, b_lo = _split_bf16(b)
    return _nt_dot(a_hi, b_hi) + (_nt_dot(a_hi, b_lo) + _nt_dot(a_lo, b_hi))


def _silu(a):
    return a * (1.0 / (1.0 + jnp.exp(-a)))


def _rmsnorm(x, g):
    ms = jnp.mean(x * x, axis=-1, keepdims=True)
    return x * lax.rsqrt(ms + EPS) * g


def _mem_kv_kernel(mem_ref, g_ref, w_ref, mk_ref, mv_ref):
    h = _rmsnorm(mem_ref[...], g_ref[...]).astype(_BF16)
    kv = jnp.dot(h, w_ref[...], preferred_element_type=_F32)
    mk_ref[...] = kv[:, :MEM_WIDTH].astype(_BF16)
    mv_ref[...] = kv[:, MEM_WIDTH:].astype(_BF16)


def _mem_kv(mem, g, w_bf16):
    B, M, D = mem.shape
    return pl.pallas_call(
        _mem_kv_kernel,
        grid=(B,),
        in_specs=[
            pl.BlockSpec((None, M, D), lambda b: (b, 0, 0)),
            pl.BlockSpec((1, D), lambda b: (0, 0)),
            pl.BlockSpec((D, 2 * MEM_WIDTH), lambda b: (0, 0)),
        ],
        out_specs=[
            pl.BlockSpec((None, M, MEM_WIDTH), lambda b: (b, 0, 0)),
            pl.BlockSpec((None, M, MEM_WIDTH), lambda b: (b, 0, 0)),
        ],
        out_shape=[jax.ShapeDtypeStruct((B, M, MEM_WIDTH), _BF16)] * 2,
        compiler_params=pltpu.CompilerParams(dimension_semantics=("arbitrary",)),
        name="mem_kv",
    )(mem, g, w_bf16)


def _in_proj_kernel(x_ref, g_ref, w_ref,
                    q_ref, k_ref, vt_ref, sga_ref, rest_ref, bias_ref,
                    kmean_sc, *, tiles_per_batch, n_blocks):
    tm = x_ref.shape[0]
    blocks_per_tile = tm // MOBA_BLOCK
    t = pl.program_id(0)
    first_block = lax.rem(t, tiles_per_batch) * blocks_per_tile

    @pl.when(t == 0)
    def _():
        kmean_sc[...] = jnp.zeros_like(kmean_sc)

    h = _rmsnorm(x_ref[...], g_ref[...]).astype(_BF16)

    def project(col0, width):
        return jnp.dot(h, w_ref[:, col0:col0 + width], preferred_element_type=_F32)

    k = project(_K0, MOBA_WIDTH)
    k_ref[...] = k.astype(_BF16)
    for b in range(blocks_per_tile):
        kmean_sc[pl.ds(first_block + b, 1), :] = jnp.mean(
            k[b * MOBA_BLOCK:(b + 1) * MOBA_BLOCK], axis=0, keepdims=True)

    q = project(_Q0, MOBA_WIDTH)
    q_ref[...] = (q * (SCALE * LOG2_E)).astype(_BF16)

    v = project(_V0, MOBA_WIDTH)
    for b in range(blocks_per_tile):
        vt_ref[b] = v[b * MOBA_BLOCK:(b + 1) * MOBA_BLOCK].T.astype(_BF16)

    ga = project(_GA0, MOBA_WIDTH)
    sga_ref[...] = _silu(ga).astype(_BF16)

    n_idx = lax.broadcasted_iota(jnp.int32, (n_blocks, tm), 0)
    col = lax.broadcasted_iota(jnp.int32, (n_blocks, tm), 1)
    own_block = first_block + col // MOBA_BLOCK
    is_past = n_idx < own_block
    for hd in range(N_MOBA_HEADS):
        cols = slice(hd * HEAD_DIM, (hd + 1) * HEAD_DIM)
        gate = jnp.where(is_past, _nt_dot_f32(kmean_sc[:, cols], q[:, cols]), _NEG_INF)
        keep = None
        for _ in range(MOBA_TOPK):
            best = jnp.max(gate, axis=0, keepdims=True)
            first = jnp.min(jnp.where(gate == best, n_idx, n_blocks), axis=0, keepdims=True)
            pick = n_idx == first
            keep = pick if keep is None else keep | pick
            gate = jnp.where(pick, _NEG_INF, gate)
        bias_ref[hd] = jnp.where(keep & is_past, 0.0, _NEG_INF)

    rest = project(_REST0, _REST_W)
    u0, gp0, qm0, gm0 = 0, POOL_WIDTH, 2 * POOL_WIDTH, 2 * POOL_WIDTH + MEM_WIDTH
    rest_ref[:, u0:gp0] = rest[:, u0:gp0].astype(_BF16)
    rest_ref[:, gp0:qm0] = _silu(rest[:, gp0:qm0]).astype(_BF16)
    rest_ref[:, qm0:gm0] = (rest[:, qm0:gm0] * SCALE).astype(_BF16)
    rest_ref[:, gm0:] = _silu(rest[:, gm0:]).astype(_BF16)


def _in_proj(xf, g, w_bf16, *, batch, seq):
    N, D = xf.shape
    tm = IN_PROJ_ROWS
    tiles_per_batch = seq // tm
    blocks_per_tile = tm // MOBA_BLOCK
    n_blocks = seq // MOBA_BLOCK
    resident = pl.Buffered(1)
    kern = functools.partial(_in_proj_kernel, tiles_per_batch=tiles_per_batch, n_blocks=n_blocks)
    return pl.pallas_call(
        kern,
        grid=(N // tm,),
        in_specs=[
            pl.BlockSpec((tm, D), lambda t: (t, 0)),
            pl.BlockSpec((1, D), lambda t: (0, 0)),
            pl.BlockSpec(w_bf16.shape, lambda t: (0, 0), pipeline_mode=resident),
        ],
        out_specs=[
            pl.BlockSpec((tm, MOBA_WIDTH), lambda t: (t, 0)),
            pl.BlockSpec((tm, MOBA_WIDTH), lambda t: (t, 0)),
            pl.BlockSpec((None, blocks_per_tile, MOBA_WIDTH, MOBA_BLOCK),
                         lambda t: (t // tiles_per_batch, t % tiles_per_batch, 0, 0)),
            pl.BlockSpec((tm, MOBA_WIDTH), lambda t: (t, 0)),
            pl.BlockSpec((tm, _REST_W), lambda t: (t, 0)),
            pl.BlockSpec((None, N_MOBA_HEADS, n_blocks, tm),
                         lambda t: (t // tiles_per_batch, 0, 0, t % tiles_per_batch)),
        ],
        out_shape=[
            jax.ShapeDtypeStruct((N, MOBA_WIDTH), _BF16),
            jax.ShapeDtypeStruct((N, MOBA_WIDTH), _BF16),
            jax.ShapeDtypeStruct((batch, n_blocks, MOBA_WIDTH, MOBA_BLOCK), _BF16),
            jax.ShapeDtypeStruct((N, MOBA_WIDTH), _BF16),
            jax.ShapeDtypeStruct((N, _REST_W), _BF16),
            jax.ShapeDtypeStruct((batch, N_MOBA_HEADS, n_blocks, seq), _F32),
        ],
        scratch_shapes=[pltpu.VMEM((n_blocks, MOBA_WIDTH), _F32)],
        compiler_params=pltpu.CompilerParams(
            dimension_semantics=("arbitrary",), vmem_limit_bytes=V7X_VMEM_LIMIT_BYTES),
        name="in_proj",
    )(xf, g, w_bf16)


def _moba_kernel(q_ref, qn_ref, k_hbm, vt_hbm, bias_ref, sga_ref, o_ref,
                 k_ref, vt_ref, k_sem, vt_sem, qt_sc, s_sc, smax_sc, m_sc, acc_sc):
    tq = q_ref.shape[0]
    b = pl.program_id(0)
    i = pl.program_id(1)
    n_blocks = vt_ref.shape[0]
    ones_rows = jnp.ones((MOBA_SUM_ROWS, MOBA_BLOCK), _BF16)

    def head_cols(hd):
        return slice(hd * HEAD_DIM, (hd + 1) * HEAD_DIM)

    def kv_block_copies(j):
        rows = pl.ds(pl.multiple_of(j * MOBA_BLOCK, MOBA_BLOCK), MOBA_BLOCK)
        return (pltpu.make_async_copy(k_hbm.at[b, rows, :], k_ref.at[rows, :], k_sem.at[j % 2]),
                pltpu.make_async_copy(vt_hbm.at[b, j], vt_ref.at[j], vt_sem.at[j % 2]))

    def start_kv_block(j):
        for copy in kv_block_copies(j):
            copy.start()

    def wait_kv_block(j):
        for copy in kv_block_copies(j):
            copy.wait()

    @pl.when(i == 0)
    def _():
        start_kv_block(0)
        start_kv_block(1)
        wait_kv_block(0)
        wait_kv_block(1)

    @pl.when((i > 0) & (i + 1 < n_blocks))
    def _():
        wait_kv_block(i + 1)

    @pl.when(i + 2 < n_blocks)
    def _():
        start_kv_block(i + 2)

    def produce_head(hd, j, slot, mask=lambda s: s, *, qt):
        start = pl.multiple_of(j * MOBA_BLOCK, MOBA_BLOCK)
        s = mask(jnp.dot(k_ref[pl.ds(start, MOBA_BLOCK), head_cols(hd)], qt[hd],
                         preferred_element_type=_F32))
        s_sc[slot, hd] = s
        smax_sc[slot, pl.ds(hd, 1), :] = jnp.max(s, axis=0, keepdims=True)

    kpos = lax.broadcasted_iota(jnp.int32, (MOBA_BLOCK, tq), 0)
    qpos = lax.broadcasted_iota(jnp.int32, (MOBA_BLOCK, tq), 1)
    causal = kpos <= qpos

    def produce_first_pair_head(hd, own_block, qt):
        produce_head(hd, own_block, 0, lambda s: jnp.where(causal, s, _NEG_INF), qt=qt)
        produce_head(hd, 0, 1, qt=qt)

    n_pairs = (i + 2) // 2

    def pair_blocks(pair):
        return jnp.where(pair == 0, i, 2 * pair - 1), 2 * pair

    def consume_head(hd, pair):
        row = pl.ds(hd, 1)
        blocks = pair_blocks(pair)
        biases = (jnp.where(pair == 0, 0.0, bias_ref[hd, pl.ds(blocks[0], 1), :]),
                  bias_ref[hd, pl.ds(blocks[1], 1), :])
        m = m_sc[row, :]
        m_new = m
        for slot, bias in enumerate(biases):
            m_new = jnp.maximum(m_new, smax_sc[slot, row, :] + bias)
        alpha = jnp.exp2(m - m_new)
        m_sc[row, :] = m_new
        pv = None
        for slot, (j, bias) in enumerate(zip(blocks, biases)):
            p = jnp.exp2(s_sc[slot, hd] - (m_new - bias))
            v_and_ones = jnp.concatenate([vt_ref[j, head_cols(hd), :], ones_rows], axis=0)
            d = jnp.dot(v_and_ones, p.astype(_BF16), preferred_element_type=_F32)
            pv = d if pv is None else pv + d
        acc_sc[hd] = alpha * acc_sc[hd] + pv

    def reset_head(hd):
        m_sc[pl.ds(hd, 1), :] = jnp.full((1, tq), _NEG_INF, _F32)
        acc_sc[hd] = jnp.zeros(acc_sc.shape[1:], _F32)

    qt_cur, qt_next = qt_sc.at[i % 2], qt_sc.at[(i + 1) % 2]

    @pl.when(i == 0)
    def _():
        for hd in range(N_MOBA_HEADS):
            reset_head(hd)
            qt_cur[hd] = q_ref[:, head_cols(hd)].T
            produce_first_pair_head(hd, i, qt_cur)

    def step(pair):
        for hd in range(N_MOBA_HEADS):
            consume_head(hd, pair - 1)
            produce_head(hd, 2 * pair - 1, 0, qt=qt_cur)
            produce_head(hd, 2 * pair, 1, qt=qt_cur)

    def body(it, carry):
        for u in range(MOBA_STEPS_PER_ITERATION):
            step(MOBA_STEPS_PER_ITERATION * it + 1 + u)
        return carry

    n_steps = n_pairs - 1
    n_iterations = n_steps // MOBA_STEPS_PER_ITERATION
    lax.fori_loop(0, n_iterations, body, 0)

    done = MOBA_STEPS_PER_ITERATION * n_iterations
    chunk = MOBA_STEPS_PER_ITERATION // 2
    while chunk >= 1:
        take = ((n_steps - done) // chunk) % 2 == 1

        @pl.when(take)
        def _(done=done, chunk=chunk):
            for u in range(chunk):
                step(done + 1 + u)

        done = done + jnp.where(take, chunk, 0)
        chunk //= 2

    next_own = jnp.minimum(i + 1, n_blocks - 1)
    for hd in range(N_MOBA_HEADS):
        qt_next[hd] = qn_ref[:, head_cols(hd)].T
    for hd in range(N_MOBA_HEADS):
        consume_head(hd, n_pairs - 1)
        produce_first_pair_head(hd, next_own, qt_next)

    for hd in range(N_MOBA_HEADS):
        l = acc_sc[hd, HEAD_DIM:HEAD_DIM + 1, :]
        o = (acc_sc[hd, 0:HEAD_DIM, :] / l).T
        o_ref[:, head_cols(hd)] = (o * sga_ref[:, head_cols(hd)].astype(_F32)).astype(_BF16)
        reset_head(hd)


def _moba(q, k, vt, bias, sga, *, batch, seq):
    N = q.shape[0]
    n_blocks = seq // MOBA_BLOCK
    tq = MOBA_BLOCK
    k3 = k.reshape(batch, seq, MOBA_WIDTH)
    row_block = lambda b, i: (b * n_blocks + i, 0)
    next_row_block = lambda b, i: (b * n_blocks + jnp.minimum(i + 1, n_blocks - 1), 0)
    return pl.pallas_call(
        _moba_kernel,
        grid=(batch, n_blocks),
        in_specs=[
            pl.BlockSpec((tq, MOBA_WIDTH), row_block),
            pl.BlockSpec((tq, MOBA_WIDTH), next_row_block),
            pl.BlockSpec(memory_space=pl.ANY),
            pl.BlockSpec(memory_space=pl.ANY),
            pl.BlockSpec((None, N_MOBA_HEADS, n_blocks, tq), lambda b, i: (b, 0, 0, i)),
            pl.BlockSpec((tq, MOBA_WIDTH), row_block),
        ],
        out_specs=pl.BlockSpec((tq, MOBA_WIDTH), row_block),
        out_shape=jax.ShapeDtypeStruct((N, MOBA_WIDTH), _BF16),
        scratch_shapes=[
            pltpu.VMEM((seq, MOBA_WIDTH), _BF16),
            pltpu.VMEM((n_blocks, MOBA_WIDTH, MOBA_BLOCK), _BF16),
            pltpu.SemaphoreType.DMA((2,)),
            pltpu.SemaphoreType.DMA((2,)),
            pltpu.VMEM((2, N_MOBA_HEADS, HEAD_DIM, tq), _BF16),
            pltpu.VMEM((2, N_MOBA_HEADS, MOBA_BLOCK, tq), _F32),
            pltpu.VMEM((2, N_MOBA_HEADS, tq), _F32),
            pltpu.VMEM((N_MOBA_HEADS, tq), _F32),
            pltpu.VMEM((N_MOBA_HEADS, HEAD_DIM + MOBA_SUM_ROWS, tq), _F32),
        ],
        compiler_params=pltpu.CompilerParams(
            dimension_semantics=("arbitrary", "arbitrary"), vmem_limit_bytes=V7X_VMEM_LIMIT_BYTES),
        name="moba",
    )(q, q, k3, vt, bias, sga)


def _pool_mem_kernel(rest_ref, halo_ref, mk_ref, mv_ref, wp_ref, ps_ref, o_ref, *, tiles_per_batch):
    tm = rest_ref.shape[0]
    tb = lax.rem(pl.program_id(0), tiles_per_batch)
    u0, gp0, qm0, gm0 = 0, POOL_WIDTH, 2 * POOL_WIDTH, 2 * POOL_WIDTH + MEM_WIDTH

    u = rest_ref[:, u0:gp0].astype(_F32)
    halo = jnp.where(tb != 0, halo_ref[...].astype(_F32), 0.0)
    pos = tb * tm + lax.broadcasted_iota(jnp.int32, (tm, POOL_GROUP_DIM), 0)
    for g, w in enumerate(POOL_WINDOWS):
        cols = slice(g * POOL_GROUP_DIM, (g + 1) * POOL_GROUP_DIM)
        u_g = u[:, cols]
        run = jnp.concatenate([halo[:, cols], u_g], axis=0)
        span = 1
        while span < w:
            run = run + pltpu.roll(run, span, axis=0)
            span *= 2
        win = run[POOL_HALO:]
        cnt = jnp.minimum(pos + 1, w).astype(_F32)
        pooled = win / cnt - u_g
        mixed = jnp.dot(pooled.astype(_BF16), wp_ref[g], preferred_element_type=_F32)
        gate = rest_ref[:, gp0 + g * POOL_GROUP_DIM:gp0 + (g + 1) * POOL_GROUP_DIM].astype(_F32)
        o_ref[:, cols] = (mixed * ps_ref[:, cols] * gate).astype(_BF16)

    for hd in range(N_MEM_HEADS):
        cols = slice(hd * HEAD_DIM, (hd + 1) * HEAD_DIM)
        qm = rest_ref[:, qm0 + hd * HEAD_DIM:qm0 + (hd + 1) * HEAD_DIM]
        s = _nt_dot(qm, mk_ref[:, cols])
        m = jnp.max(s, axis=-1, keepdims=True)
        e = jnp.exp(s - m)
        l = jnp.sum(e, axis=-1, keepdims=True)
        o = jnp.dot(e.astype(_BF16), mv_ref[:, cols], preferred_element_type=_F32) / l
        gate = rest_ref[:, gm0 + hd * HEAD_DIM:gm0 + (hd + 1) * HEAD_DIM].astype(_F32)
        o_ref[:, POOL_WIDTH + hd * HEAD_DIM:POOL_WIDTH + (hd + 1) * HEAD_DIM] = (o * gate).astype(_BF16)


def _pool_mem(rest, mk, mv, wp_bf16, pool_scale, *, seq):
    N = rest.shape[0]
    tm = POOL_MEM_ROWS
    tiles_per_batch = seq // tm
    halo_blocks_per_tile = tm // POOL_HALO
    M = mk.shape[1]
    kern = functools.partial(_pool_mem_kernel, tiles_per_batch=tiles_per_batch)
    return pl.pallas_call(
        kern,
        grid=(N // tm,),
        in_specs=[
            pl.BlockSpec((tm, _REST_W), lambda t: (t, 0)),
            pl.BlockSpec((POOL_HALO, POOL_WIDTH),
                         lambda t: (jnp.maximum(t * halo_blocks_per_tile - 1, 0), 0)),
            pl.BlockSpec((None, M, MEM_WIDTH), lambda t: (t // tiles_per_batch, 0, 0)),
            pl.BlockSpec((None, M, MEM_WIDTH), lambda t: (t // tiles_per_batch, 0, 0)),
            pl.BlockSpec(wp_bf16.shape, lambda t: (0, 0, 0)),
            pl.BlockSpec((1, POOL_WIDTH), lambda t: (0, 0)),
        ],
        out_specs=pl.BlockSpec((tm, POOL_WIDTH + MEM_WIDTH), lambda t: (t, 0)),
        out_shape=jax.ShapeDtypeStruct((N, POOL_WIDTH + MEM_WIDTH), _BF16),
        compiler_params=pltpu.CompilerParams(dimension_semantics=("arbitrary",)),
        name="pool_mem",
    )(rest, rest, mk, mv, wp_bf16, pool_scale)


def _out_proj_kernel(ya_ref, ypm_ref, x_ref, w_ref, fg_ref, o_ref, *, final_norm):
    for r0 in range(0, x_ref.shape[0], OUT_PROJ_SUBROWS):
        rows = slice(r0, r0 + OUT_PROJ_SUBROWS)
        y = jnp.dot(ya_ref[rows, :], w_ref[0:MOBA_WIDTH, :], preferred_element_type=_F32)
        y = y + jnp.dot(ypm_ref[rows, :], w_ref[MOBA_WIDTH:, :], preferred_element_type=_F32)
        r = x_ref[rows, :] + y
        o_ref[rows, :] = _rmsnorm(r, fg_ref[...]) if final_norm else r


def _out_proj(ya, ypm, xf, w_bf16, final_g, *, final_norm):
    N, D = xf.shape
    tm = OUT_PROJ_ROWS
    kern = functools.partial(_out_proj_kernel, final_norm=final_norm)
    return pl.pallas_call(
        kern,
        grid=(N // tm,),
        in_specs=[
            pl.BlockSpec((tm, MOBA_WIDTH), lambda t: (t, 0)),
            pl.BlockSpec((tm, POOL_WIDTH + MEM_WIDTH), lambda t: (t, 0)),
            pl.BlockSpec((tm, D), lambda t: (t, 0)),
            pl.BlockSpec(w_bf16.shape, lambda t: (0, 0), pipeline_mode=pl.Buffered(1)),
            pl.BlockSpec((1, D), lambda t: (0, 0)),
        ],
        out_specs=pl.BlockSpec((tm, D), lambda t: (t, 0)),
        out_shape=jax.ShapeDtypeStruct((N, D), _F32),
        compiler_params=pltpu.CompilerParams(
            dimension_semantics=("arbitrary",), vmem_limit_bytes=V7X_VMEM_LIMIT_BYTES),
        name="out_proj",
    )(ya, ypm, xf, w_bf16, final_g)


def _layer(xf, mem, norm_g, mem_norm_g, w_in, w_mem_kv, w_pool, pool_scale, w_out, final_g,
           *, batch, seq, final_norm):
    mk, mv = _mem_kv(mem, mem_norm_g[None, :], w_mem_kv.astype(_BF16))
    q, k, vt, sga, rest, bias = _in_proj(xf, norm_g[None, :], w_in.astype(_BF16), batch=batch, seq=seq)
    ya = _moba(q, k, vt, bias, sga, batch=batch, seq=seq)
    ypm = _pool_mem(rest, mk, mv, w_pool.astype(_BF16), pool_scale[None, :], seq=seq)
    return _out_proj(ya, ypm, xf, w_out.astype(_BF16), final_g[None, :], final_norm=final_norm)


def kernel(x, mem, norm_g, mem_norm_g, w_in, w_mem_kv, w_pool, pool_scale, w_out, final_norm_g):
    batch, seq, d_model = x.shape
    depth = norm_g.shape[0]
    assert seq % MOBA_BLOCK == 0 and seq % IN_PROJ_ROWS == 0
    assert seq % POOL_MEM_ROWS == 0 and seq % OUT_PROJ_ROWS == 0
    assert w_in.shape[2] == 4 * MOBA_WIDTH + 2 * POOL_WIDTH + 2 * MEM_WIDTH
    xf = x.reshape(batch * seq, d_model)
    for l in range(depth):
        xf = _layer(xf, mem, norm_g[l], mem_norm_g[l], w_in[l], w_mem_kv[l], w_pool[l],
                    pool_scale[l], w_out[l], final_norm_g,
                    batch=batch, seq=seq, final_norm=(l == depth - 1))
    return xf.reshape(batch, seq, d_model)
```

```python
import functools

import jax
import jax.numpy as jnp
from jax import lax
from jax.experimental import pallas as pl
from jax.experimental.pallas import tpu as pltpu

HEAD_DIM = 128
N_MOBA_HEADS = 8
MOBA_WIDTH = N_MOBA_HEADS * HEAD_DIM
N_MEM_HEADS = 4
MEM_WIDTH = N_MEM_HEADS * HEAD_DIM
POOL_WINDOWS = (2, 4, 8, 16)
POOL_GROUP_DIM = 128
POOL_WIDTH = len(POOL_WINDOWS) * POOL_GROUP_DIM
POOL_HALO = 16
MOBA_BLOCK = 256
MOBA_TOPK = 3
EPS = 1e-6
SCALE = HEAD_DIM ** -0.5
LOG2_E = 1.4426950408889634
MOBA_SUM_ROWS = 16
MOBA_STEPS_PER_ITERATION = 4

_Q0, _K0, _V0, _GA0 = 0, MOBA_WIDTH, 2 * MOBA_WIDTH, 3 * MOBA_WIDTH
_REST0 = 4 * MOBA_WIDTH
_REST_W = 2 * POOL_WIDTH + 2 * MEM_WIDTH

V7X_VMEM_LIMIT_BYTES = 56 * 1024 * 1024

IN_PROJ_ROWS = 512
POOL_MEM_ROWS = 512
OUT_PROJ_ROWS = 512
OUT_PROJ_SUBROWS = 256
WEIGHT_STAGE_COLS = 256

_F32 = jnp.float32
_BF16 = jnp.bfloat16
_NEG_INF = float("-inf")


def _nt_dot(a, b):
    return lax.dot_general(a, b, (((1,), (1,)), ((), ())), preferred_element_type=_F32)


def _split_bf16(a):
    hi = a.astype(_BF16)
    lo = (a - hi.astype(_F32)).astype(_BF16)
    return hi, lo


def _nt_dot_f32(a, b):
    a_hi, a_lo = _split_bf16(a)
    b_hi, b_lo = _split_bf16(b)
    return _nt_dot(a_hi, b_hi) + (_nt_dot(a_hi, b_lo) + _nt_dot(a_lo, b_hi))


def _silu(a):
    return a * (1.0 / (1.0 + jnp.exp(-a)))


def _rmsnorm(x, g):
    ms = jnp.mean(x * x, axis=-1, keepdims=True)
    return x * lax.rsqrt(ms + EPS) * g


def _load_weight_as_bf16(w_hbm, w_sc, stage_sc, sem):
    n_chunks = w_sc.shape[1] // WEIGHT_STAGE_COLS

    def chunk_copy(c):
        cols = slice(c * WEIGHT_STAGE_COLS, (c + 1) * WEIGHT_STAGE_COLS)
        return pltpu.make_async_copy(w_hbm.at[:, cols], stage_sc.at[c % 2], sem.at[c % 2])

    chunk_copy(0).start()
    for c in range(n_chunks):
        if c + 1 < n_chunks:
            chunk_copy(c + 1).start()
        chunk_copy(c).wait()
        w_sc[:, c * WEIGHT_STAGE_COLS:(c + 1) * WEIGHT_STAGE_COLS] = stage_sc[c % 2].astype(_BF16)


def _mem_kv_kernel(mem_ref, g_ref, w_ref, mk_ref, mv_ref):
    h = _rmsnorm(mem_ref[...], g_ref[...]).astype(_BF16)
    kv = jnp.dot(h, w_ref[...], preferred_element_type=_F32)
    mk_ref[...] = kv[:, :MEM_WIDTH].astype(_BF16)
    mv_ref[...] = kv[:, MEM_WIDTH:].astype(_BF16)


def _mem_kv(mem, g, w_bf16):
    B, M, D = mem.shape
    return pl.pallas_call(
        _mem_kv_kernel,
        grid=(B,),
        in_specs=[
            pl.BlockSpec((None, M, D), lambda b: (b, 0, 0)),
            pl.BlockSpec((1, D), lambda b: (0, 0)),
            pl.BlockSpec((D, 2 * MEM_WIDTH), lambda b: (0, 0)),
        ],
        out_specs=[
            pl.BlockSpec((None, M, MEM_WIDTH), lambda b: (b, 0, 0)),
            pl.BlockSpec((None, M, MEM_WIDTH), lambda b: (b, 0, 0)),
        ],
        out_shape=[jax.ShapeDtypeStruct((B, M, MEM_WIDTH), _BF16)] * 2,
        compiler_params=pltpu.CompilerParams(dimension_semantics=("arbitrary",)),
        name="mem_kv",
    )(mem, g, w_bf16)


def _in_proj_kernel(x_ref, g_ref, w_hbm,
                    q_ref, k_ref, vt_ref, sga_ref, rest_ref, bias_ref,
                    w_ref, stage_sc, w_sem, kmean_sc, *, tiles_per_batch, n_blocks):
    tm = x_ref.shape[0]
    blocks_per_tile = tm // MOBA_BLOCK
    t = pl.program_id(0)
    first_block = lax.rem(t, tiles_per_batch) * blocks_per_tile

    @pl.when(t == 0)
    def _():
        _load_weight_as_bf16(w_hbm, w_ref, stage_sc, w_sem)
        kmean_sc[...] = jnp.zeros_like(kmean_sc)

    h = _rmsnorm(x_ref[...], g_ref[...]).astype(_BF16)

    def project(col0, width):
        return jnp.dot(h, w_ref[:, col0:col0 + width], preferred_element_type=_F32)

    k = project(_K0, MOBA_WIDTH)
    k_ref[...] = k.astype(_BF16)
    for b in range(blocks_per_tile):
        kmean_sc[pl.ds(first_block + b, 1), :] = jnp.mean(
            k[b * MOBA_BLOCK:(b + 1) * MOBA_BLOCK], axis=0, keepdims=True)

    q = project(_Q0, MOBA_WIDTH)
    q_ref[...] = (q * (SCALE * LOG2_E)).astype(_BF16)

    v = project(_V0, MOBA_WIDTH)
    for b in range(blocks_per_tile):
        vt_ref[b] = v[b * MOBA_BLOCK:(b + 1) * MOBA_BLOCK].T.astype(_BF16)

    ga = project(_GA0, MOBA_WIDTH)
    sga_ref[...] = _silu(ga).astype(_BF16)

    n_idx = lax.broadcasted_iota(jnp.int32, (n_blocks, tm), 0)
    col = lax.broadcasted_iota(jnp.int32, (n_blocks, tm), 1)
    own_block = first_block + col // MOBA_BLOCK
    is_past = n_idx < own_block
    for hd in range(N_MOBA_HEADS):
        cols = slice(hd * HEAD_DIM, (hd + 1) * HEAD_DIM)
        gate = jnp.where(is_past, _nt_dot_f32(kmean_sc[:, cols], q[:, cols]), _NEG_INF)
        keep = None
        for _ in range(MOBA_TOPK):
            best = jnp.max(gate, axis=0, keepdims=True)
            first = jnp.min(jnp.where(gate == best, n_idx, n_blocks), axis=0, keepdims=True)
            pick = n_idx == first
            keep = pick if keep is None else keep | pick
            gate = jnp.where(pick, _NEG_INF, gate)
        bias_ref[hd] = jnp.where(keep & is_past, 0.0, _NEG_INF)

    rest = project(_REST0, _REST_W)
    u0, gp0, qm0, gm0 = 0, POOL_WIDTH, 2 * POOL_WIDTH, 2 * POOL_WIDTH + MEM_WIDTH
    rest_ref[:, u0:gp0] = rest[:, u0:gp0].astype(_BF16)
    rest_ref[:, gp0:qm0] = _silu(rest[:, gp0:qm0]).astype(_BF16)
    rest_ref[:, qm0:gm0] = (rest[:, qm0:gm0] * SCALE).astype(_BF16)
    rest_ref[:, gm0:] = _silu(rest[:, gm0:]).astype(_BF16)


def _in_proj(xf, g, w, *, batch, seq):
    N, D = xf.shape
    tm = IN_PROJ_ROWS
    tiles_per_batch = seq // tm
    blocks_per_tile = tm // MOBA_BLOCK
    n_blocks = seq // MOBA_BLOCK
    kern = functools.partial(_in_proj_kernel, tiles_per_batch=tiles_per_batch, n_blocks=n_blocks)
    return pl.pallas_call(
        kern,
        grid=(N // tm,),
        in_specs=[
            pl.BlockSpec((tm, D), lambda t: (t, 0)),
            pl.BlockSpec((1, D), lambda t: (0, 0)),
            pl.BlockSpec(memory_space=pl.ANY),
        ],
        out_specs=[
            pl.BlockSpec((tm, MOBA_WIDTH), lambda t: (t, 0)),
            pl.BlockSpec((tm, MOBA_WIDTH), lambda t: (t, 0)),
            pl.BlockSpec((None, blocks_per_tile, MOBA_WIDTH, MOBA_BLOCK),
                         lambda t: (t // tiles_per_batch, t % tiles_per_batch, 0, 0)),
            pl.BlockSpec((tm, MOBA_WIDTH), lambda t: (t, 0)),
            pl.BlockSpec((tm, _REST_W), lambda t: (t, 0)),
            pl.BlockSpec((None, N_MOBA_HEADS, n_blocks, tm),
                         lambda t: (t // tiles_per_batch, 0, 0, t % tiles_per_batch)),
        ],
        out_shape=[
            jax.ShapeDtypeStruct((N, MOBA_WIDTH), _BF16),
            jax.ShapeDtypeStruct((N, MOBA_WIDTH), _BF16),
            jax.ShapeDtypeStruct((batch, n_blocks, MOBA_WIDTH, MOBA_BLOCK), _BF16),
            jax.ShapeDtypeStruct((N, MOBA_WIDTH), _BF16),
            jax.ShapeDtypeStruct((N, _REST_W), _BF16),
            jax.ShapeDtypeStruct((batch, N_MOBA_HEADS, n_blocks, seq), _F32),
        ],
        scratch_shapes=[
            pltpu.VMEM(w.shape, _BF16),
            pltpu.VMEM((2, D, WEIGHT_STAGE_COLS), _F32),
            pltpu.SemaphoreType.DMA((2,)),
            pltpu.VMEM((n_blocks, MOBA_WIDTH), _F32),
        ],
        compiler_params=pltpu.CompilerParams(
            dimension_semantics=("arbitrary",), vmem_limit_bytes=V7X_VMEM_LIMIT_BYTES),
        name="in_proj",
    )(xf, g, w)


def _moba_kernel(q_ref, qn_ref, k_hbm, vt_hbm, bias_ref, sga_ref, o_ref,
                 k_ref, vt_ref, k_sem, vt_sem, qt_sc, s_sc, smax_sc, m_sc, acc_sc):
    tq = q_ref.shape[0]
    b = pl.program_id(0)
    i = pl.program_id(1)
    n_blocks = vt_ref.shape[0]
    ones_rows = jnp.ones((MOBA_SUM_ROWS, MOBA_BLOCK), _BF16)

    def head_cols(hd):
        return slice(hd * HEAD_DIM, (hd + 1) * HEAD_DIM)

    def kv_block_copies(j):
        rows = pl.ds(pl.multiple_of(j * MOBA_BLOCK, MOBA_BLOCK), MOBA_BLOCK)
        return (pltpu.make_async_copy(k_hbm.at[b, rows, :], k_ref.at[rows, :], k_sem.at[j % 2]),
                pltpu.make_async_copy(vt_hbm.at[b, j], vt_ref.at[j], vt_sem.at[j % 2]))

    def start_kv_block(j):
        for copy in kv_block_copies(j):
            copy.start()

    def wait_kv_block(j):
        for copy in kv_block_copies(j):
            copy.wait()

    @pl.when(i == 0)
    def _():
        start_kv_block(0)
        start_kv_block(1)
        wait_kv_block(0)
        wait_kv_block(1)

    @pl.when((i > 0) & (i + 1 < n_blocks))
    def _():
        wait_kv_block(i + 1)

    @pl.when(i + 2 < n_blocks)
    def _():
        start_kv_block(i + 2)

    def produce_head(hd, j, slot, mask=lambda s: s, *, qt):
        start = pl.multiple_of(j * MOBA_BLOCK, MOBA_BLOCK)
        s = mask(jnp.dot(k_ref[pl.ds(start, MOBA_BLOCK), head_cols(hd)], qt[hd],
                         preferred_element_type=_F32))
        s_sc[slot, hd] = s
        smax_sc[slot, pl.ds(hd, 1), :] = jnp.max(s, axis=0, keepdims=True)

    kpos = lax.broadcasted_iota(jnp.int32, (MOBA_BLOCK, tq), 0)
    qpos = lax.broadcasted_iota(jnp.int32, (MOBA_BLOCK, tq), 1)
    causal = kpos <= qpos

    def produce_first_pair_head(hd, own_block, qt):
        produce_head(hd, own_block, 0, lambda s: jnp.where(causal, s, _NEG_INF), qt=qt)
        produce_head(hd, 0, 1, qt=qt)

    n_pairs = (i + 2) // 2

    def pair_blocks(pair):
        return jnp.where(pair == 0, i, 2 * pair - 1), 2 * pair

    def consume_head(hd, pair):
        row = pl.ds(hd, 1)
        blocks = pair_blocks(pair)
        biases = (jnp.where(pair == 0, 0.0, bias_ref[hd, pl.ds(blocks[0], 1), :]),
                  bias_ref[hd, pl.ds(blocks[1], 1), :])
        m = m_sc[row, :]
        m_new = m
        for slot, bias in enumerate(biases):
            m_new = jnp.maximum(m_new, smax_sc[slot, row, :] + bias)
        alpha = jnp.exp2(m - m_new)
        m_sc[row, :] = m_new
        pv = None
        for slot, (j, bias) in enumerate(zip(blocks, biases)):
            p = jnp.exp2(s_sc[slot, hd] - (m_new - bias))
            v_and_ones = jnp.concatenate([vt_ref[j, head_cols(hd), :], ones_rows], axis=0)
            d = jnp.dot(v_and_ones, p.astype(_BF16), preferred_element_type=_F32)
            pv = d if pv is None else pv + d
        acc_sc[hd] = alpha * acc_sc[hd] + pv

    def reset_head(hd):
        m_sc[pl.ds(hd, 1), :] = jnp.full((1, tq), _NEG_INF, _F32)
        acc_sc[hd] = jnp.zeros(acc_sc.shape[1:], _F32)

    qt_cur, qt_next = qt_sc.at[i % 2], qt_sc.at[(i + 1) % 2]

    @pl.when(i == 0)
    def _():
        for hd in range(N_MOBA_HEADS):
            reset_head(hd)
            qt_cur[hd] = q_ref[:, head_cols(hd)].T
            produce_first_pair_head(hd, i, qt_cur)

    def step(pair):
        for hd in range(N_MOBA_HEADS):
            consume_head(hd, pair - 1)
            produce_head(hd, 2 * pair - 1, 0, qt=qt_cur)
            produce_head(hd, 2 * pair, 1, qt=qt_cur)

    def body(it, carry):
        for u in range(MOBA_STEPS_PER_ITERATION):
            step(MOBA_STEPS_PER_ITERATION * it + 1 + u)
        return carry

    n_steps = n_pairs - 1
    n_iterations = n_steps // MOBA_STEPS_PER_ITERATION
    lax.fori_loop(0, n_iterations, body, 0)

    done = MOBA_STEPS_PER_ITERATION * n_iterations
    chunk = MOBA_STEPS_PER_ITERATION // 2
    while chunk >= 1:
        take = ((n_steps - done) // chunk) % 2 == 1

        @pl.when(take)
        def _(done=done, chunk=chunk):
            for u in range(chunk):
                step(done + 1 + u)

        done = done + jnp.where(take, chunk, 0)
        chunk //= 2

    next_own = jnp.minimum(i + 1, n_blocks - 1)
    for hd in range(N_MOBA_HEADS):
        qt_next[hd] = qn_ref[:, head_cols(hd)].T
    for hd in range(N_MOBA_HEADS):
        consume_head(hd, n_pairs - 1)
        produce_first_pair_head(hd, next_own, qt_next)

    for hd in range(N_MOBA_HEADS):
        l = acc_sc[hd, HEAD_DIM:HEAD_DIM + 1, :]
        o = (acc_sc[hd, 0:HEAD_DIM, :] / l).T
        o_ref[:, head_cols(hd)] = (o * sga_ref[:, head_cols(hd)].astype(_F32)).astype(_BF16)
        reset_head(hd)


def _moba(q, k, vt, bias, sga, *, batch, seq):
    N = q.shape[0]
    n_blocks = seq // MOBA_BLOCK
    tq = MOBA_BLOCK
    k3 = k.reshape(batch, seq, MOBA_WIDTH)
    row_block = lambda b, i: (b * n_blocks + i, 0)
    next_row_block = lambda b, i: (b * n_blocks + jnp.minimum(i + 1, n_blocks - 1), 0)
    return pl.pallas_call(
        _moba_kernel,
        grid=(batch, n_blocks),
        in_specs=[
            pl.BlockSpec((tq, MOBA_WIDTH), row_block),
            pl.BlockSpec((tq, MOBA_WIDTH), next_row_block),
            pl.BlockSpec(memory_space=pl.ANY),
            pl.BlockSpec(memory_space=pl.ANY),
            pl.BlockSpec((None, N_MOBA_HEADS, n_blocks, tq), lambda b, i: (b, 0, 0, i)),
            pl.BlockSpec((tq, MOBA_WIDTH), row_block),
        ],
        out_specs=pl.BlockSpec((tq, MOBA_WIDTH), row_block),
        out_shape=jax.ShapeDtypeStruct((N, MOBA_WIDTH), _BF16),
        scratch_shapes=[
            pltpu.VMEM((seq, MOBA_WIDTH), _BF16),
            pltpu.VMEM((n_blocks, MOBA_WIDTH, MOBA_BLOCK), _BF16),
            pltpu.SemaphoreType.DMA((2,)),
            pltpu.SemaphoreType.DMA((2,)),
            pltpu.VMEM((2, N_MOBA_HEADS, HEAD_DIM, tq), _BF16),
            pltpu.VMEM((2, N_MOBA_HEADS, MOBA_BLOCK, tq), _F32),
            pltpu.VMEM((2, N_MOBA_HEADS, tq), _F32),
            pltpu.VMEM((N_MOBA_HEADS, tq), _F32),
            pltpu.VMEM((N_MOBA_HEADS, HEAD_DIM + MOBA_SUM_ROWS, tq), _F32),
        ],
        compiler_params=pltpu.CompilerParams(
            dimension_semantics=("arbitrary", "arbitrary"), vmem_limit_bytes=V7X_VMEM_LIMIT_BYTES),
        name="moba",
    )(q, q, k3, vt, bias, sga)


def _pool_mem_kernel(rest_ref, halo_ref, mk_ref, mv_ref, wp_ref, ps_ref, o_ref, *, tiles_per_batch):
    tm = rest_ref.shape[0]
    tb = lax.rem(pl.program_id(0), tiles_per_batch)
    u0, gp0, qm0, gm0 = 0, POOL_WIDTH, 2 * POOL_WIDTH, 2 * POOL_WIDTH + MEM_WIDTH

    u = rest_ref[:, u0:gp0].astype(_F32)
    halo = jnp.where(tb != 0, halo_ref[...].astype(_F32), 0.0)
    pos = tb * tm + lax.broadcasted_iota(jnp.int32, (tm, POOL_GROUP_DIM), 0)
    for g, w in enumerate(POOL_WINDOWS):
        cols = slice(g * POOL_GROUP_DIM, (g + 1) * POOL_GROUP_DIM)
        u_g = u[:, cols]
        run = jnp.concatenate([halo[:, cols], u_g], axis=0)
        span = 1
        while span < w:
            run = run + pltpu.roll(run, span, axis=0)
            span *= 2
        win = run[POOL_HALO:]
        cnt = jnp.minimum(pos + 1, w).astype(_F32)
        pooled = win / cnt - u_g
        mixed = jnp.dot(pooled.astype(_BF16), wp_ref[g], preferred_element_type=_F32)
        gate = rest_ref[:, gp0 + g * POOL_GROUP_DIM:gp0 + (g + 1) * POOL_GROUP_DIM].astype(_F32)
        o_ref[:, cols] = (mixed * ps_ref[:, cols] * gate).astype(_BF16)

    for hd in range(N_MEM_HEADS):
        cols = slice(hd * HEAD_DIM, (hd + 1) * HEAD_DIM)
        qm = rest_ref[:, qm0 + hd * HEAD_DIM:qm0 + (hd + 1) * HEAD_DIM]
        s = _nt_dot(qm, mk_ref[:, cols])
        m = jnp.max(s, axis=-1, keepdims=True)
        e = jnp.exp(s - m)
        l = jnp.sum(e, axis=-1, keepdims=True)
        o = jnp.dot(e.astype(_BF16), mv_ref[:, cols], preferred_element_type=_F32) / l
        gate = rest_ref[:, gm0 + hd * HEAD_DIM:gm0 + (hd + 1) * HEAD_DIM].astype(_F32)
        o_ref[:, POOL_WIDTH + hd * HEAD_DIM:POOL_WIDTH + (hd + 1) * HEAD_DIM] = (o * gate).astype(_BF16)


def _pool_mem(rest, mk, mv, wp_bf16, pool_scale, *, seq):
    N = rest.shape[0]
    tm = POOL_MEM_ROWS
    tiles_per_batch = seq // tm
    halo_blocks_per_tile = tm // POOL_HALO
    M = mk.shape[1]
    kern = functools.partial(_pool_mem_kernel, tiles_per_batch=tiles_per_batch)
    return pl.pallas_call(
        kern,
        grid=(N // tm,),
        in_specs=[
            pl.BlockSpec((tm, _REST_W), lambda t: (t, 0)),
            pl.BlockSpec((POOL_HALO, POOL_WIDTH),
                         lambda t: (jnp.maximum(t * halo_blocks_per_tile - 1, 0), 0)),
            pl.BlockSpec((None, M, MEM_WIDTH), lambda t: (t // tiles_per_batch, 0, 0)),
            pl.BlockSpec((None, M, MEM_WIDTH), lambda t: (t // tiles_per_batch, 0, 0)),
            pl.BlockSpec(wp_bf16.shape, lambda t: (0, 0, 0)),
            pl.BlockSpec((1, POOL_WIDTH), lambda t: (0, 0)),
        ],
        out_specs=pl.BlockSpec((tm, POOL_WIDTH + MEM_WIDTH), lambda t: (t, 0)),
        out_shape=jax.ShapeDtypeStruct((N, POOL_WIDTH + MEM_WIDTH), _BF16),
        compiler_params=pltpu.CompilerParams(dimension_semantics=("arbitrary",)),
        name="pool_mem",
    )(rest, rest, mk, mv, wp_bf16, pool_scale)


def _out_proj_kernel(ya_ref, ypm_ref, x_ref, w_hbm, fg_ref, o_ref, w_ref, stage_sc, w_sem,
                     *, final_norm):
    @pl.when(pl.program_id(0) == 0)
    def _():
        _load_weight_as_bf16(w_hbm, w_ref, stage_sc, w_sem)

    for r0 in range(0, x_ref.shape[0], OUT_PROJ_SUBROWS):
        rows = slice(r0, r0 + OUT_PROJ_SUBROWS)
        y = jnp.dot(ya_ref[rows, :], w_ref[0:MOBA_WIDTH, :], preferred_element_type=_F32)
        y = y + jnp.dot(ypm_ref[rows, :], w_ref[MOBA_WIDTH:, :], preferred_element_type=_F32)
        r = x_ref[rows, :] + y
        o_ref[rows, :] = _rmsnorm(r, fg_ref[...]) if final_norm else r


def _out_proj(ya, ypm, xf, w, final_g, *, final_norm):
    N, D = xf.shape
    tm = OUT_PROJ_ROWS
    kern = functools.partial(_out_proj_kernel, final_norm=final_norm)
    return pl.pallas_call(
        kern,
        grid=(N // tm,),
        in_specs=[
            pl.BlockSpec((tm, MOBA_WIDTH), lambda t: (t, 0)),
            pl.BlockSpec((tm, POOL_WIDTH + MEM_WIDTH), lambda t: (t, 0)),
            pl.BlockSpec((tm, D), lambda t: (t, 0)),
            pl.BlockSpec(memory_space=pl.ANY),
            pl.BlockSpec((1, D), lambda t: (0, 0)),
        ],
        out_specs=pl.BlockSpec((tm, D), lambda t: (t, 0)),
        out_shape=jax.ShapeDtypeStruct((N, D), _F32),
        scratch_shapes=[
            pltpu.VMEM(w.shape, _BF16),
            pltpu.VMEM((2, w.shape[0], WEIGHT_STAGE_COLS), _F32),
            pltpu.SemaphoreType.DMA((2,)),
        ],
        compiler_params=pltpu.CompilerParams(
            dimension_semantics=("arbitrary",), vmem_limit_bytes=V7X_VMEM_LIMIT_BYTES),
        name="out_proj",
    )(ya, ypm, xf, w, final_g)


def _layer(xf, mem, norm_g, mem_norm_g, w_in, w_mem_kv, w_pool, pool_scale, w_out, final_g,
           *, batch, seq, final_norm):
    mk, mv = _mem_kv(mem, mem_norm_g[None, :], w_mem_kv.astype(_BF16))
    q, k, vt, sga, rest, bias = _in_proj(xf, norm_g[None, :], w_in, batch=batch, seq=seq)
    ya = _moba(q, k, vt, bias, sga, batch=batch, seq=seq)
    ypm = _pool_mem(rest, mk, mv, w_pool.astype(_BF16), pool_scale[None, :], seq=seq)
    return _out_proj(ya, ypm, xf, w_out, final_g[None, :], final_norm=final_norm)


def kernel(x, mem, norm_g, mem_norm_g, w_in, w_mem_kv, w_pool, pool_scale, w_out, final_norm_g):
    batch, seq, d_model = x.shape
    depth = norm_g.shape[0]
    assert seq % MOBA_BLOCK == 0 and seq % IN_PROJ_ROWS == 0
    assert seq % POOL_MEM_ROWS == 0 and seq % OUT_PROJ_ROWS == 0
    assert w_in.shape[2] == 4 * MOBA_WIDTH + 2 * POOL_WIDTH + 2 * MEM_WIDTH
    xf = x.reshape(batch * seq, d_model)
    for l in range(depth):
        xf = _layer(xf, mem, norm_g[l], mem_norm_g[l], w_in[l], w_mem_kv[l], w_pool[l],
                    pool_scale[l], w_out[l], final_norm_g,
                    batch=batch, seq=seq, final_norm=(l == depth - 1))
    return xf.reshape(batch, seq, d_model)
```

```python
import functools

import jax
import jax.numpy as jnp
from jax import lax
from jax.experimental import pallas as pl
from jax.experimental.pallas import tpu as pltpu

HEAD_DIM = 128
N_MOBA_HEADS = 8
MOBA_WIDTH = N_MOBA_HEADS * HEAD_DIM
N_MEM_HEADS = 4
MEM_WIDTH = N_MEM_HEADS * HEAD_DIM
POOL_WINDOWS = (2, 4, 8, 16)
POOL_GROUP_DIM = 128
POOL_WIDTH = len(POOL_WINDOWS) * POOL_GROUP_DIM
POOL_HALO = 16
MOBA_BLOCK = 256
MOBA_TOPK = 3
EPS = 1e-6
SCALE = HEAD_DIM ** -0.5
LOG2_E = 1.4426950408889634
MOBA_SUM_ROWS = 16
MOBA_STEPS_PER_ITERATION = 4

_Q0, _K0, _V0, _GA0 = 0, MOBA_WIDTH, 2 * MOBA_WIDTH, 3 * MOBA_WIDTH
_REST0 = 4 * MOBA_WIDTH
_REST_W = 2 * POOL_WIDTH + 2 * MEM_WIDTH

V7X_VMEM_LIMIT_BYTES = 56 * 1024 * 1024

IN_PROJ_ROWS = 512
POOL_MEM_ROWS = 512
OUT_PROJ_ROWS = 1024
OUT_PROJ_SUBROWS = 256
WEIGHT_STAGE_COLS = 256

_F32 = jnp.float32
_BF16 = jnp.bfloat16
_NEG_INF = float("-inf")


def _nt_dot(a, b):
    return lax.dot_general(a, b, (((1,), (1,)), ((), ())), preferred_element_type=_F32)


def _split_bf16(a):
    hi = a.astype(_BF16)
    lo = (a - hi.astype(_F32)).astype(_BF16)
    return hi, lo


def _nt_dot_f32(a, b):
    a_hi, a_lo = _split_bf16(a)
    b_hi, b_lo = _split_bf16(b)
    return _nt_dot(a_hi, b_hi) + (_nt_dot(a_hi, b_lo) + _nt_dot(a_lo, b_hi))


def _silu(a):
    return a * (1.0 / (1.0 + jnp.exp(-a)))


def _rmsnorm(x, g):
    ms = jnp.mean(x * x, axis=-1, keepdims=True)
    return x * lax.rsqrt(ms + EPS) * g


def _load_weight_as_bf16(w_hbm, w_sc, stage_sc, sem):
    n_chunks = w_sc.shape[1] // WEIGHT_STAGE_COLS

    def chunk_copy(c):
        cols = slice(c * WEIGHT_STAGE_COLS, (c + 1) * WEIGHT_STAGE_COLS)
        return pltpu.make_async_copy(w_hbm.at[:, cols], stage_sc.at[c % 2], sem.at[c % 2])

    chunk_copy(0).start()
    for c in range(n_chunks):
        if c + 1 < n_chunks:
            chunk_copy(c + 1).start()
        chunk_copy(c).wait()
        w_sc[:, c * WEIGHT_STAGE_COLS:(c + 1) * WEIGHT_STAGE_COLS] = stage_sc[c % 2].astype(_BF16)


def _mem_kv_kernel(mem_ref, g_ref, w_ref, mk_ref, mv_ref):
    h = _rmsnorm(mem_ref[...], g_ref[...]).astype(_BF16)
    kv = jnp.dot(h, w_ref[...], preferred_element_type=_F32)
    mk_ref[...] = kv[:, :MEM_WIDTH].astype(_BF16)
    mv_ref[...] = kv[:, MEM_WIDTH:].astype(_BF16)


def _mem_kv(mem, g, w_bf16):
    B, M, D = mem.shape
    return pl.pallas_call(
        _mem_kv_kernel,
        grid=(B,),
        in_specs=[
            pl.BlockSpec((None, M, D), lambda b: (b, 0, 0)),
            pl.BlockSpec((1, D), lambda b: (0, 0)),
            pl.BlockSpec((D, 2 * MEM_WIDTH), lambda b: (0, 0)),
        ],
        out_specs=[
            pl.BlockSpec((None, M, MEM_WIDTH), lambda b: (b, 0, 0)),
            pl.BlockSpec((None, M, MEM_WIDTH), lambda b: (b, 0, 0)),
        ],
        out_shape=[jax.ShapeDtypeStruct((B, M, MEM_WIDTH), _BF16)] * 2,
        compiler_params=pltpu.CompilerParams(dimension_semantics=("arbitrary",)),
        name="mem_kv",
    )(mem, g, w_bf16)


def _in_proj_kernel(x_ref, g_ref, w_hbm,
                    q_ref, k_ref, vt_ref, sga_ref, rest_ref, bias_ref,
                    w_ref, stage_sc, w_sem, kmean_sc, *, tiles_per_batch, n_blocks):
    tm = x_ref.shape[0]
    blocks_per_tile = tm // MOBA_BLOCK
    t = pl.program_id(0)
    first_block = lax.rem(t, tiles_per_batch) * blocks_per_tile

    @pl.when(t == 0)
    def _():
        _load_weight_as_bf16(w_hbm, w_ref, stage_sc, w_sem)
        kmean_sc[...] = jnp.zeros_like(kmean_sc)

    h = _rmsnorm(x_ref[...], g_ref[...]).astype(_BF16)

    def project(col0, width):
        return jnp.dot(h, w_ref[:, col0:col0 + width], preferred_element_type=_F32)

    k = project(_K0, MOBA_WIDTH)
    k_ref[...] = k.astype(_BF16)
    for b in range(blocks_per_tile):
        kmean_sc[pl.ds(first_block + b, 1), :] = jnp.mean(
            k[b * MOBA_BLOCK:(b + 1) * MOBA_BLOCK], axis=0, keepdims=True)

    q = project(_Q0, MOBA_WIDTH)
    q_ref[...] = (q * (SCALE * LOG2_E)).astype(_BF16)

    v = project(_V0, MOBA_WIDTH)
    for b in range(blocks_per_tile):
        vt_ref[b] = v[b * MOBA_BLOCK:(b + 1) * MOBA_BLOCK].T.astype(_BF16)

    ga = project(_GA0, MOBA_WIDTH)
    sga_ref[...] = _silu(ga).astype(_BF16)

    n_idx = lax.broadcasted_iota(jnp.int32, (n_blocks, tm), 0)
    col = lax.broadcasted_iota(jnp.int32, (n_blocks, tm), 1)
    own_block = first_block + col // MOBA_BLOCK
    is_past = n_idx < own_block
    for hd in range(N_MOBA_HEADS):
        cols = slice(hd * HEAD_DIM, (hd + 1) * HEAD_DIM)
        gate = jnp.where(is_past, _nt_dot_f32(kmean_sc[:, cols], q[:, cols]), _NEG_INF)
        keep = None
        for _ in range(MOBA_TOPK):
            best = jnp.max(gate, axis=0, keepdims=True)
            first = jnp.min(jnp.where(gate == best, n_idx, n_blocks), axis=0, keepdims=True)
            pick = n_idx == first
            keep = pick if keep is None else keep | pick
            gate = jnp.where(pick, _NEG_INF, gate)
        bias_ref[hd] = jnp.where(keep & is_past, 0.0, _NEG_INF)

    rest = project(_REST0, _REST_W)
    u0, gp0, qm0, gm0 = 0, POOL_WIDTH, 2 * POOL_WIDTH, 2 * POOL_WIDTH + MEM_WIDTH
    rest_ref[:, u0:gp0] = rest[:, u0:gp0].astype(_BF16)
    rest_ref[:, gp0:qm0] = _silu(rest[:, gp0:qm0]).astype(_BF16)
    rest_ref[:, qm0:gm0] = (rest[:, qm0:gm0] * SCALE).astype(_BF16)
    rest_ref[:, gm0:] = _silu(rest[:, gm0:]).astype(_BF16)


def _in_proj(xf, g, w, *, batch, seq):
    N, D = xf.shape
    tm = IN_PROJ_ROWS
    tiles_per_batch = seq // tm
    blocks_per_tile = tm // MOBA_BLOCK
    n_blocks = seq // MOBA_BLOCK
    kern = functools.partial(_in_proj_kernel, tiles_per_batch=tiles_per_batch, n_blocks=n_blocks)
    return pl.pallas_call(
        kern,
        grid=(N // tm,),
        in_specs=[
            pl.BlockSpec((tm, D), lambda t: (t, 0)),
            pl.BlockSpec((1, D), lambda t: (0, 0)),
            pl.BlockSpec(memory_space=pl.ANY),
        ],
        out_specs=[
            pl.BlockSpec((tm, MOBA_WIDTH), lambda t: (t, 0)),
            pl.BlockSpec((tm, MOBA_WIDTH), lambda t: (t, 0)),
            pl.BlockSpec((None, blocks_per_tile, MOBA_WIDTH, MOBA_BLOCK),
                         lambda t: (t // tiles_per_batch, t % tiles_per_batch, 0, 0)),
            pl.BlockSpec((tm, MOBA_WIDTH), lambda t: (t, 0)),
            pl.BlockSpec((tm, _REST_W), lambda t: (t, 0)),
            pl.BlockSpec((None, N_MOBA_HEADS, n_blocks, tm),
                         lambda t: (t // tiles_per_batch, 0, 0, t % tiles_per_batch)),
        ],
        out_shape=[
            jax.ShapeDtypeStruct((N, MOBA_WIDTH), _BF16),
            jax.ShapeDtypeStruct((N, MOBA_WIDTH), _BF16),
            jax.ShapeDtypeStruct((batch, n_blocks, MOBA_WIDTH, MOBA_BLOCK), _BF16),
            jax.ShapeDtypeStruct((N, MOBA_WIDTH), _BF16),
            jax.ShapeDtypeStruct((N, _REST_W), _BF16),
            jax.ShapeDtypeStruct((batch, N_MOBA_HEADS, n_blocks, seq), _F32),
        ],
        scratch_shapes=[
            pltpu.VMEM(w.shape, _BF16),
            pltpu.VMEM((2, D, WEIGHT_STAGE_COLS), _F32),
            pltpu.SemaphoreType.DMA((2,)),
            pltpu.VMEM((n_blocks, MOBA_WIDTH), _F32),
        ],
        compiler_params=pltpu.CompilerParams(
            dimension_semantics=("arbitrary",), vmem_limit_bytes=V7X_VMEM_LIMIT_BYTES),
        name="in_proj",
    )(xf, g, w)


def _moba_kernel(q_ref, qn_ref, k_hbm, vt_hbm, bias_ref, sga_ref, o_ref,
                 k_ref, vt_ref, k_sem, vt_sem, qt_sc, s_sc, smax_sc, m_sc, acc_sc):
    tq = q_ref.shape[0]
    b = pl.program_id(0)
    i = pl.program_id(1)
    n_blocks = vt_ref.shape[0]
    ones_rows = jnp.ones((MOBA_SUM_ROWS, MOBA_BLOCK), _BF16)

    def head_cols(hd):
        return slice(hd * HEAD_DIM, (hd + 1) * HEAD_DIM)

    def kv_block_copies(j):
        rows = pl.ds(pl.multiple_of(j * MOBA_BLOCK, MOBA_BLOCK), MOBA_BLOCK)
        return (pltpu.make_async_copy(k_hbm.at[b, rows, :], k_ref.at[rows, :], k_sem.at[j % 2]),
                pltpu.make_async_copy(vt_hbm.at[b, j], vt_ref.at[j], vt_sem.at[j % 2]))

    def start_kv_block(j):
        for copy in kv_block_copies(j):
            copy.start()

    def wait_kv_block(j):
        for copy in kv_block_copies(j):
            copy.wait()

    @pl.when(i == 0)
    def _():
        start_kv_block(0)
        start_kv_block(1)
        wait_kv_block(0)
        wait_kv_block(1)

    @pl.when((i > 0) & (i + 1 < n_blocks))
    def _():
        wait_kv_block(i + 1)

    @pl.when(i + 2 < n_blocks)
    def _():
        start_kv_block(i + 2)

    def produce_head(hd, j, slot, mask=lambda s: s, *, qt):
        start = pl.multiple_of(j * MOBA_BLOCK, MOBA_BLOCK)
        s = mask(jnp.dot(k_ref[pl.ds(start, MOBA_BLOCK), head_cols(hd)], qt[hd],
                         preferred_element_type=_F32))
        s_sc[slot, hd] = s
        smax_sc[slot, pl.ds(hd, 1), :] = jnp.max(s, axis=0, keepdims=True)

    kpos = lax.broadcasted_iota(jnp.int32, (MOBA_BLOCK, tq), 0)
    qpos = lax.broadcasted_iota(jnp.int32, (MOBA_BLOCK, tq), 1)
    causal = kpos <= qpos

    def produce_first_pair_head(hd, own_block, qt):
        produce_head(hd, own_block, 0, lambda s: jnp.where(causal, s, _NEG_INF), qt=qt)
        produce_head(hd, 0, 1, qt=qt)

    n_pairs = (i + 2) // 2

    def pair_blocks(pair):
        return jnp.where(pair == 0, i, 2 * pair - 1), 2 * pair

    def consume_head(hd, pair):
        row = pl.ds(hd, 1)
        blocks = pair_blocks(pair)
        biases = (jnp.where(pair == 0, 0.0, bias_ref[hd, pl.ds(blocks[0], 1), :]),
                  bias_ref[hd, pl.ds(blocks[1], 1), :])
        m = m_sc[row, :]
        m_new = m
        for slot, bias in enumerate(biases):
            m_new = jnp.maximum(m_new, smax_sc[slot, row, :] + bias)
        alpha = jnp.exp2(m - m_new)
        m_sc[row, :] = m_new
        pv = None
        for slot, (j, bias) in enumerate(zip(blocks, biases)):
            p = jnp.exp2(s_sc[slot, hd] - (m_new - bias))
            v_and_ones = jnp.concatenate([vt_ref[j, head_cols(hd), :], ones_rows], axis=0)
            d = jnp.dot(v_and_ones, p.astype(_BF16), preferred_element_type=_F32)
            pv = d if pv is None else pv + d
        acc_sc[hd] = alpha * acc_sc[hd] + pv

    def reset_head(hd):
        m_sc[pl.ds(hd, 1), :] = jnp.full((1, tq), _NEG_INF, _F32)
        acc_sc[hd] = jnp.zeros(acc_sc.shape[1:], _F32)

    qt_cur, qt_next = qt_sc.at[i % 2], qt_sc.at[(i + 1) % 2]

    @pl.when(i == 0)
    def _():
        for hd in range(N_MOBA_HEADS):
            reset_head(hd)
            qt_cur[hd] = q_ref[:, head_cols(hd)].T
            produce_first_pair_head(hd, i, qt_cur)

    def step(pair):
        for hd in range(N_MOBA_HEADS):
            consume_head(hd, pair - 1)
            produce_head(hd, 2 * pair - 1, 0, qt=qt_cur)
            produce_head(hd, 2 * pair, 1, qt=qt_cur)

    def body(it, carry):
        for u in range(MOBA_STEPS_PER_ITERATION):
            step(MOBA_STEPS_PER_ITERATION * it + 1 + u)
        return carry

    n_steps = n_pairs - 1
    n_iterations = n_steps // MOBA_STEPS_PER_ITERATION
    lax.fori_loop(0, n_iterations, body, 0)

    done = MOBA_STEPS_PER_ITERATION * n_iterations
    chunk = MOBA_STEPS_PER_ITERATION // 2
    while chunk >= 1:
        take = ((n_steps - done) // chunk) % 2 == 1

        @pl.when(take)
        def _(done=done, chunk=chunk):
            for u in range(chunk):
                step(done + 1 + u)

        done = done + jnp.where(take, chunk, 0)
        chunk //= 2

    next_own = jnp.minimum(i + 1, n_blocks - 1)
    for hd in range(N_MOBA_HEADS):
        qt_next[hd] = qn_ref[:, head_cols(hd)].T
    for hd in range(N_MOBA_HEADS):
        consume_head(hd, n_pairs - 1)
        produce_first_pair_head(hd, next_own, qt_next)

    for hd in range(N_MOBA_HEADS):
        l = acc_sc[hd, HEAD_DIM:HEAD_DIM + 1, :]
        o = (acc_sc[hd, 0:HEAD_DIM, :] / l).T
        o_ref[:, head_cols(hd)] = (o * sga_ref[:, head_cols(hd)].astype(_F32)).astype(_BF16)
        reset_head(hd)


def _moba(q, k, vt, bias, sga, *, batch, seq):
    N = q.shape[0]
    n_blocks = seq // MOBA_BLOCK
    tq = MOBA_BLOCK
    k3 = k.reshape(batch, seq, MOBA_WIDTH)
    row_block = lambda b, i: (b * n_blocks + i, 0)
    next_row_block = lambda b, i: (b * n_blocks + jnp.minimum(i + 1, n_blocks - 1), 0)
    return pl.pallas_call(
        _moba_kernel,
        grid=(batch, n_blocks),
        in_specs=[
            pl.BlockSpec((tq, MOBA_WIDTH), row_block),
            pl.BlockSpec((tq, MOBA_WIDTH), next_row_block),
            pl.BlockSpec(memory_space=pl.ANY),
            pl.BlockSpec(memory_space=pl.ANY),
            pl.BlockSpec((None, N_MOBA_HEADS, n_blocks, tq), lambda b, i: (b, 0, 0, i)),
            pl.BlockSpec((tq, MOBA_WIDTH), row_block),
        ],
        out_specs=pl.BlockSpec((tq, MOBA_WIDTH), row_block),
        out_shape=jax.ShapeDtypeStruct((N, MOBA_WIDTH), _BF16),
        scratch_shapes=[
            pltpu.VMEM((seq, MOBA_WIDTH), _BF16),
            pltpu.VMEM((n_blocks, MOBA_WIDTH, MOBA_BLOCK), _BF16),
            pltpu.SemaphoreType.DMA((2,)),
            pltpu.SemaphoreType.DMA((2,)),
            pltpu.VMEM((2, N_MOBA_HEADS, HEAD_DIM, tq), _BF16),
            pltpu.VMEM((2, N_MOBA_HEADS, MOBA_BLOCK, tq), _F32),
            pltpu.VMEM((2, N_MOBA_HEADS, tq), _F32),
            pltpu.VMEM((N_MOBA_HEADS, tq), _F32),
            pltpu.VMEM((N_MOBA_HEADS, HEAD_DIM + MOBA_SUM_ROWS, tq), _F32),
        ],
        compiler_params=pltpu.CompilerParams(
            dimension_semantics=("arbitrary", "arbitrary"), vmem_limit_bytes=V7X_VMEM_LIMIT_BYTES),
        name="moba",
    )(q, q, k3, vt, bias, sga)


def _pool_mem_kernel(rest_ref, halo_ref, mk_ref, mv_ref, wp_ref, ps_ref, o_ref, *, tiles_per_batch):
    tm = rest_ref.shape[0]
    tb = lax.rem(pl.program_id(0), tiles_per_batch)
    u0, gp0, qm0, gm0 = 0, POOL_WIDTH, 2 * POOL_WIDTH, 2 * POOL_WIDTH + MEM_WIDTH

    u = rest_ref[:, u0:gp0].astype(_F32)
    halo = jnp.where(tb != 0, halo_ref[...].astype(_F32), 0.0)
    pos = tb * tm + lax.broadcasted_iota(jnp.int32, (tm, POOL_GROUP_DIM), 0)
    for g, w in enumerate(POOL_WINDOWS):
        cols = slice(g * POOL_GROUP_DIM, (g + 1) * POOL_GROUP_DIM)
        u_g = u[:, cols]
        run = jnp.concatenate([halo[:, cols], u_g], axis=0)
        span = 1
        while span < w:
            run = run + pltpu.roll(run, span, axis=0)
            span *= 2
        win = run[POOL_HALO:]
        cnt = jnp.minimum(pos + 1, w).astype(_F32)
        pooled = win / cnt - u_g
        mixed = jnp.dot(pooled.astype(_BF16), wp_ref[g], preferred_element_type=_F32)
        gate = rest_ref[:, gp0 + g * POOL_GROUP_DIM:gp0 + (g + 1) * POOL_GROUP_DIM].astype(_F32)
        o_ref[:, cols] = (mixed * ps_ref[:, cols] * gate).astype(_BF16)

    for hd in range(N_MEM_HEADS):
        cols = slice(hd * HEAD_DIM, (hd + 1) * HEAD_DIM)
        qm = rest_ref[:, qm0 + hd * HEAD_DIM:qm0 + (hd + 1) * HEAD_DIM]
        s = _nt_dot(qm, mk_ref[:, cols])
        m = jnp.max(s, axis=-1, keepdims=True)
        e = jnp.exp(s - m)
        l = jnp.sum(e, axis=-1, keepdims=True)
        o = jnp.dot(e.astype(_BF16), mv_ref[:, cols], preferred_element_type=_F32) / l
        gate = rest_ref[:, gm0 + hd * HEAD_DIM:gm0 + (hd + 1) * HEAD_DIM].astype(_F32)
        o_ref[:, POOL_WIDTH + hd * HEAD_DIM:POOL_WIDTH + (hd + 1) * HEAD_DIM] = (o * gate).astype(_BF16)


def _pool_mem(rest, mk, mv, wp_bf16, pool_scale, *, seq):
    N = rest.shape[0]
    tm = POOL_MEM_ROWS
    tiles_per_batch = seq // tm
    halo_blocks_per_tile = tm // POOL_HALO
    M = mk.shape[1]
    kern = functools.partial(_pool_mem_kernel, tiles_per_batch=tiles_per_batch)
    return pl.pallas_call(
        kern,
        grid=(N // tm,),
        in_specs=[
            pl.BlockSpec((tm, _REST_W), lambda t: (t, 0)),
            pl.BlockSpec((POOL_HALO, POOL_WIDTH),
                         lambda t: (jnp.maximum(t * halo_blocks_per_tile - 1, 0), 0)),
            pl.BlockSpec((None, M, MEM_WIDTH), lambda t: (t // tiles_per_batch, 0, 0)),
            pl.BlockSpec((None, M, MEM_WIDTH), lambda t: (t // tiles_per_batch, 0, 0)),
            pl.BlockSpec(wp_bf16.shape, lambda t: (0, 0, 0)),
            pl.BlockSpec((1, POOL_WIDTH), lambda t: (0, 0)),
        ],
        out_specs=pl.BlockSpec((tm, POOL_WIDTH + MEM_WIDTH), lambda t: (t, 0)),
        out_shape=jax.ShapeDtypeStruct((N, POOL_WIDTH + MEM_WIDTH), _BF16),
        compiler_params=pltpu.CompilerParams(dimension_semantics=("arbitrary",)),
        name="pool_mem",
    )(rest, rest, mk, mv, wp_bf16, pool_scale)


def _out_proj_kernel(ya_ref, ypm_ref, x_ref, w_hbm, fg_ref, o_ref, w_ref, stage_sc, w_sem,
                     *, final_norm):
    @pl.when(pl.program_id(0) == 0)
    def _():
        _load_weight_as_bf16(w_hbm, w_ref, stage_sc, w_sem)

    for r0 in range(0, x_ref.shape[0], OUT_PROJ_SUBROWS):
        rows = slice(r0, r0 + OUT_PROJ_SUBROWS)
        y = jnp.dot(ya_ref[rows, :], w_ref[0:MOBA_WIDTH, :], preferred_element_type=_F32)
        y = y + jnp.dot(ypm_ref[rows, :], w_ref[MOBA_WIDTH:, :], preferred_element_type=_F32)
        r = x_ref[rows, :] + y
        o_ref[rows, :] = _rmsnorm(r, fg_ref[...]) if final_norm else r


def _out_proj(ya, ypm, xf, w, final_g, *, final_norm):
    N, D = xf.shape
    tm = OUT_PROJ_ROWS
    kern = functools.partial(_out_proj_kernel, final_norm=final_norm)
    return pl.pallas_call(
        kern,
        grid=(N // tm,),
        in_specs=[
            pl.BlockSpec((tm, MOBA_WIDTH), lambda t: (t, 0)),
            pl.BlockSpec((tm, POOL_WIDTH + MEM_WIDTH), lambda t: (t, 0)),
            pl.BlockSpec((tm, D), lambda t: (t, 0)),
            pl.BlockSpec(memory_space=pl.ANY),
            pl.BlockSpec((1, D), lambda t: (0, 0)),
        ],
        out_specs=pl.BlockSpec((tm, D), lambda t: (t, 0)),
        out_shape=jax.ShapeDtypeStruct((N, D), _F32),
        scratch_shapes=[
            pltpu.VMEM(w.shape, _BF16),
            pltpu.VMEM((2, w.shape[0], WEIGHT_STAGE_COLS), _F32),
            pltpu.SemaphoreType.DMA((2,)),
        ],
        compiler_params=pltpu.CompilerParams(
            dimension_semantics=("arbitrary",), vmem_limit_bytes=V7X_VMEM_LIMIT_BYTES),
        name="out_proj",
    )(ya, ypm, xf, w, final_g)


def _layer(xf, mem, norm_g, mem_norm_g, w_in, w_mem_kv, w_pool, pool_scale, w_out, final_g,
           *, batch, seq, final_norm):
    mk, mv = _mem_kv(mem, mem_norm_g[None, :], w_mem_kv.astype(_BF16))
    q, k, vt, sga, rest, bias = _in_proj(xf, norm_g[None, :], w_in, batch=batch, seq=seq)
    ya = _moba(q, k, vt, bias, sga, batch=batch, seq=seq)
    ypm = _pool_mem(rest, mk, mv, w_pool.astype(_BF16), pool_scale[None, :], seq=seq)
    return _out_proj(ya, ypm, xf, w_out, final_g[None, :], final_norm=final_norm)


def kernel(x, mem, norm_g, mem_norm_g, w_in, w_mem_kv, w_pool, pool_scale, w_out, final_norm_g):
    batch, seq, d_model = x.shape
    depth = norm_g.shape[0]
    assert seq % MOBA_BLOCK == 0 and seq % IN_PROJ_ROWS == 0
    assert seq % POOL_MEM_ROWS == 0 and seq % OUT_PROJ_ROWS == 0
    assert w_in.shape[2] == 4 * MOBA_WIDTH + 2 * POOL_WIDTH + 2 * MEM_WIDTH
    xf = x.reshape(batch * seq, d_model)
    for l in range(depth):
        xf = _layer(xf, mem, norm_g[l], mem_norm_g[l], w_in[l], w_mem_kv[l], w_pool[l],
                    pool_scale[l], w_out[l], final_norm_g,
                    batch=batch, seq=seq, final_norm=(l == depth - 1))
    return xf.reshape(batch, seq, d_model)
```

```python
import functools

import jax
import jax.numpy as jnp
from jax import lax
from jax.experimental import pallas as pl
from jax.experimental.pallas import tpu as pltpu

HEAD_DIM = 128
N_MOBA_HEADS = 8
MOBA_WIDTH = N_MOBA_HEADS * HEAD_DIM
N_MEM_HEADS = 4
MEM_WIDTH = N_MEM_HEADS * HEAD_DIM
POOL_WINDOWS = (2, 4, 8, 16)
POOL_GROUP_DIM = 128
POOL_WIDTH = len(POOL_WINDOWS) * POOL_GROUP_DIM
POOL_HALO = 16
MOBA_BLOCK = 256
MOBA_TOPK = 3
EPS = 1e-6
SCALE = HEAD_DIM ** -0.5
LOG2_E = 1.4426950408889634
MOBA_SUM_ROWS = 16
MOBA_STEPS_PER_ITERATION = 4

_Q0, _K0, _V0, _GA0 = 0, MOBA_WIDTH, 2 * MOBA_WIDTH, 3 * MOBA_WIDTH
_REST0 = 4 * MOBA_WIDTH
_REST_W = 2 * POOL_WIDTH + 2 * MEM_WIDTH

V7X_VMEM_LIMIT_BYTES = 56 * 1024 * 1024

IN_PROJ_ROWS = 512
POOL_MEM_ROWS = 512
OUT_PROJ_ROWS = 512
OUT_PROJ_SUBROWS = 256
WEIGHT_STAGE_COLS = 256

_F32 = jnp.float32
_BF16 = jnp.bfloat16
_NEG_INF = float("-inf")


def _nt_dot(a, b):
    return lax.dot_general(a, b, (((1,), (1,)), ((), ())), preferred_element_type=_F32)


def _split_bf16(a):
    hi = a.astype(_BF16)
    lo = (a - hi.astype(_F32)).astype(_BF16)
    return hi, lo


def _nt_dot_f32(a, b):
    m = a.shape[0]
    a_hi, a_lo = _split_bf16(a)
    b_hi, b_lo = _split_bf16(b)
    both = _nt_dot(jnp.concatenate([a_hi, a_lo], axis=0), b_hi)
    return both[:m] + (_nt_dot(a_hi, b_lo) + both[m:])


def _silu(a):
    return a * (1.0 / (1.0 + jnp.exp(-a)))


def _rmsnorm(x, g):
    ms = jnp.mean(x * x, axis=-1, keepdims=True)
    return x * lax.rsqrt(ms + EPS) * g


def _load_weight_as_bf16(w_hbm, w_sc, stage_sc, sem):
    n_chunks = w_sc.shape[1] // WEIGHT_STAGE_COLS

    def chunk_copy(c):
        cols = slice(c * WEIGHT_STAGE_COLS, (c + 1) * WEIGHT_STAGE_COLS)
        return pltpu.make_async_copy(w_hbm.at[:, cols], stage_sc.at[c % 2], sem.at[c % 2])

    chunk_copy(0).start()
    for c in range(n_chunks):
        if c + 1 < n_chunks:
            chunk_copy(c + 1).start()
        chunk_copy(c).wait()
        w_sc[:, c * WEIGHT_STAGE_COLS:(c + 1) * WEIGHT_STAGE_COLS] = stage_sc[c % 2].astype(_BF16)


def _mem_kv_kernel(mem_ref, g_ref, w_ref, mk_ref, mv_ref):
    h = _rmsnorm(mem_ref[...], g_ref[...]).astype(_BF16)
    kv = jnp.dot(h, w_ref[...], preferred_element_type=_F32)
    mk_ref[...] = kv[:, :MEM_WIDTH].astype(_BF16)
    mv_ref[...] = kv[:, MEM_WIDTH:].astype(_BF16)


def _mem_kv(mem, g, w_bf16):
    B, M, D = mem.shape
    return pl.pallas_call(
        _mem_kv_kernel,
        grid=(B,),
        in_specs=[
            pl.BlockSpec((None, M, D), lambda b: (b, 0, 0)),
            pl.BlockSpec((1, D), lambda b: (0, 0)),
            pl.BlockSpec((D, 2 * MEM_WIDTH), lambda b: (0, 0)),
        ],
        out_specs=[
            pl.BlockSpec((None, M, MEM_WIDTH), lambda b: (b, 0, 0)),
            pl.BlockSpec((None, M, MEM_WIDTH), lambda b: (b, 0, 0)),
        ],
        out_shape=[jax.ShapeDtypeStruct((B, M, MEM_WIDTH), _BF16)] * 2,
        compiler_params=pltpu.CompilerParams(dimension_semantics=("arbitrary",)),
        name="mem_kv",
    )(mem, g, w_bf16)


def _in_proj_kernel(x_ref, g_ref, w_hbm,
                    q_ref, k_ref, vt_ref, sga_ref, rest_ref, bias_ref,
                    w_ref, stage_sc, w_sem, kmean_sc, *, tiles_per_batch, n_blocks):
    tm = x_ref.shape[0]
    blocks_per_tile = tm // MOBA_BLOCK
    t = pl.program_id(0)
    first_block = lax.rem(t, tiles_per_batch) * blocks_per_tile

    @pl.when(t == 0)
    def _():
        _load_weight_as_bf16(w_hbm, w_ref, stage_sc, w_sem)
        kmean_sc[...] = jnp.zeros_like(kmean_sc)

    h = _rmsnorm(x_ref[...], g_ref[...]).astype(_BF16)

    def project(col0, width):
        return jnp.dot(h, w_ref[:, col0:col0 + width], preferred_element_type=_F32)

    k = project(_K0, MOBA_WIDTH)
    k_ref[...] = k.astype(_BF16)
    for b in range(blocks_per_tile):
        kmean_sc[pl.ds(first_block + b, 1), :] = jnp.mean(
            k[b * MOBA_BLOCK:(b + 1) * MOBA_BLOCK], axis=0, keepdims=True)

    q = project(_Q0, MOBA_WIDTH)
    q_ref[...] = (q * (SCALE * LOG2_E)).astype(_BF16)

    n_idx = lax.broadcasted_iota(jnp.int32, (n_blocks, tm), 0)
    col = lax.broadcasted_iota(jnp.int32, (n_blocks, tm), 1)
    own_block = first_block + col // MOBA_BLOCK
    is_past = n_idx < own_block

    def gate_head(hd):
        cols = slice(hd * HEAD_DIM, (hd + 1) * HEAD_DIM)
        gate = jnp.where(is_past, _nt_dot_f32(kmean_sc[:, cols], q[:, cols]), _NEG_INF)
        keep = None
        for _ in range(MOBA_TOPK):
            best = jnp.max(gate, axis=0, keepdims=True)
            first = jnp.min(jnp.where(gate == best, n_idx, n_blocks), axis=0, keepdims=True)
            pick = n_idx == first
            keep = pick if keep is None else keep | pick
            gate = jnp.where(pick, _NEG_INF, gate)
        bias_ref[hd] = jnp.where(keep & is_past, 0.0, _NEG_INF)

    chunks_per_projection = N_MOBA_HEADS // 2
    chunk_w = MOBA_WIDTH // chunks_per_projection
    for c in range(chunks_per_projection):
        cols = slice(c * chunk_w, (c + 1) * chunk_w)
        v = project(_V0 + c * chunk_w, chunk_w)
        for b in range(blocks_per_tile):
            vt_ref[b, cols, :] = v[b * MOBA_BLOCK:(b + 1) * MOBA_BLOCK].T.astype(_BF16)
        gate_head(c)
    for c in range(chunks_per_projection):
        cols = slice(c * chunk_w, (c + 1) * chunk_w)
        sga_ref[:, cols] = _silu(project(_GA0 + c * chunk_w, chunk_w)).astype(_BF16)
        gate_head(chunks_per_projection + c)

    rest = project(_REST0, _REST_W)
    u0, gp0, qm0, gm0 = 0, POOL_WIDTH, 2 * POOL_WIDTH, 2 * POOL_WIDTH + MEM_WIDTH
    rest_ref[:, u0:gp0] = rest[:, u0:gp0].astype(_BF16)
    rest_ref[:, gp0:qm0] = _silu(rest[:, gp0:qm0]).astype(_BF16)
    rest_ref[:, qm0:gm0] = (rest[:, qm0:gm0] * (SCALE * LOG2_E)).astype(_BF16)
    rest_ref[:, gm0:] = _silu(rest[:, gm0:]).astype(_BF16)


def _in_proj(xf, g, w, *, batch, seq):
    N, D = xf.shape
    tm = IN_PROJ_ROWS
    tiles_per_batch = seq // tm
    blocks_per_tile = tm // MOBA_BLOCK
    n_blocks = seq // MOBA_BLOCK
    kern = functools.partial(_in_proj_kernel, tiles_per_batch=tiles_per_batch, n_blocks=n_blocks)
    return pl.pallas_call(
        kern,
        grid=(N // tm,),
        in_specs=[
            pl.BlockSpec((tm, D), lambda t: (t, 0)),
            pl.BlockSpec((1, D), lambda t: (0, 0)),
            pl.BlockSpec(memory_space=pl.ANY),
        ],
        out_specs=[
            pl.BlockSpec((tm, MOBA_WIDTH), lambda t: (t, 0)),
            pl.BlockSpec((tm, MOBA_WIDTH), lambda t: (t, 0)),
            pl.BlockSpec((None, blocks_per_tile, MOBA_WIDTH, MOBA_BLOCK),
                         lambda t: (t // tiles_per_batch, t % tiles_per_batch, 0, 0)),
            pl.BlockSpec((tm, MOBA_WIDTH), lambda t: (t, 0)),
            pl.BlockSpec((tm, _REST_W), lambda t: (t, 0)),
            pl.BlockSpec((None, N_MOBA_HEADS, n_blocks, tm),
                         lambda t: (t // tiles_per_batch, 0, 0, t % tiles_per_batch)),
        ],
        out_shape=[
            jax.ShapeDtypeStruct((N, MOBA_WIDTH), _BF16),
            jax.ShapeDtypeStruct((N, MOBA_WIDTH), _BF16),
            jax.ShapeDtypeStruct((batch, n_blocks, MOBA_WIDTH, MOBA_BLOCK), _BF16),
            jax.ShapeDtypeStruct((N, MOBA_WIDTH), _BF16),
            jax.ShapeDtypeStruct((N, _REST_W), _BF16),
            jax.ShapeDtypeStruct((batch, N_MOBA_HEADS, n_blocks, seq), _F32),
        ],
        scratch_shapes=[
            pltpu.VMEM(w.shape, _BF16),
            pltpu.VMEM((2, D, WEIGHT_STAGE_COLS), _F32),
            pltpu.SemaphoreType.DMA((2,)),
            pltpu.VMEM((n_blocks, MOBA_WIDTH), _F32),
        ],
        compiler_params=pltpu.CompilerParams(
            dimension_semantics=("arbitrary",), vmem_limit_bytes=V7X_VMEM_LIMIT_BYTES),
        name="in_proj",
    )(xf, g, w)


def _moba_kernel(q_ref, qn_ref, k_hbm, vt_hbm, bias_ref, sga_ref, o_ref,
                 k_ref, vt_ref, k_sem, vt_sem, qt_sc, s_sc, smax_sc, m_sc, acc_sc):
    tq = q_ref.shape[0]
    b = pl.program_id(0)
    i = pl.program_id(1)
    n_blocks = vt_ref.shape[0]
    ones_rows = jnp.ones((MOBA_SUM_ROWS, MOBA_BLOCK), _BF16)

    def head_cols(hd):
        return slice(hd * HEAD_DIM, (hd + 1) * HEAD_DIM)

    def kv_block_copies(j):
        rows = pl.ds(pl.multiple_of(j * MOBA_BLOCK, MOBA_BLOCK), MOBA_BLOCK)
        return (pltpu.make_async_copy(k_hbm.at[b, rows, :], k_ref.at[rows, :], k_sem.at[j % 2]),
                pltpu.make_async_copy(vt_hbm.at[b, j], vt_ref.at[j], vt_sem.at[j % 2]))

    def start_kv_block(j):
        for copy in kv_block_copies(j):
            copy.start()

    def wait_kv_block(j):
        for copy in kv_block_copies(j):
            copy.wait()

    @pl.when(i == 0)
    def _():
        start_kv_block(0)
        start_kv_block(1)
        wait_kv_block(0)
        wait_kv_block(1)

    @pl.when((i > 0) & (i + 1 < n_blocks))
    def _():
        wait_kv_block(i + 1)

    @pl.when(i + 2 < n_blocks)
    def _():
        start_kv_block(i + 2)

    def produce_head(hd, j, slot, mask=lambda s: s, *, qt):
        start = pl.multiple_of(j * MOBA_BLOCK, MOBA_BLOCK)
        s = mask(jnp.dot(k_ref[pl.ds(start, MOBA_BLOCK), head_cols(hd)], qt[hd],
                         preferred_element_type=_F32))
        s_sc[slot, hd] = s
        smax_sc[slot, pl.ds(hd, 1), :] = jnp.max(s, axis=0, keepdims=True)

    kpos = lax.broadcasted_iota(jnp.int32, (MOBA_BLOCK, tq), 0)
    qpos = lax.broadcasted_iota(jnp.int32, (MOBA_BLOCK, tq), 1)
    causal = kpos <= qpos

    def produce_first_pair_head(hd, own_block, qt):
        produce_head(hd, own_block, 0, lambda s: jnp.where(causal, s, _NEG_INF), qt=qt)
        produce_head(hd, 0, 1, qt=qt)

    n_pairs = (i + 2) // 2

    def pair_blocks(pair):
        return jnp.where(pair == 0, i, 2 * pair - 1), 2 * pair

    def consume_head(hd, pair):
        row = pl.ds(hd, 1)
        blocks = pair_blocks(pair)
        biases = (jnp.where(pair == 0, 0.0, bias_ref[hd, pl.ds(blocks[0], 1), :]),
                  bias_ref[hd, pl.ds(blocks[1], 1), :])
        m = m_sc[row, :]
        m_new = m
        for slot, bias in enumerate(biases):
            m_new = jnp.maximum(m_new, smax_sc[slot, row, :] + bias)
        alpha = jnp.exp2(m - m_new)
        m_sc[row, :] = m_new
        pv = None
        for slot, (j, bias) in enumerate(zip(blocks, biases)):
            p = jnp.exp2(s_sc[slot, hd] - (m_new - bias))
            v_and_ones = jnp.concatenate([vt_ref[j, head_cols(hd), :], ones_rows], axis=0)
            d = jnp.dot(v_and_ones, p.astype(_BF16), preferred_element_type=_F32)
            pv = d if pv is None else pv + d
        acc_sc[hd] = alpha * acc_sc[hd] + pv

    def reset_head(hd):
        m_sc[pl.ds(hd, 1), :] = jnp.full((1, tq), _NEG_INF, _F32)
        acc_sc[hd] = jnp.zeros(acc_sc.shape[1:], _F32)

    qt_cur, qt_next = qt_sc.at[i % 2], qt_sc.at[(i + 1) % 2]

    @pl.when(i == 0)
    def _():
        for hd in range(N_MOBA_HEADS):
            reset_head(hd)
            qt_cur[hd] = q_ref[:, head_cols(hd)].T
            produce_first_pair_head(hd, i, qt_cur)

    def step(pair):
        for hd in range(N_MOBA_HEADS):
            consume_head(hd, pair - 1)
            produce_head(hd, 2 * pair - 1, 0, qt=qt_cur)
            produce_head(hd, 2 * pair, 1, qt=qt_cur)

    def body(it, carry):
        for u in range(MOBA_STEPS_PER_ITERATION):
            step(MOBA_STEPS_PER_ITERATION * it + 1 + u)
        return carry

    n_steps = n_pairs - 1
    n_iterations = n_steps // MOBA_STEPS_PER_ITERATION
    lax.fori_loop(0, n_iterations, body, 0)

    done = MOBA_STEPS_PER_ITERATION * n_iterations
    chunk = MOBA_STEPS_PER_ITERATION // 2
    while chunk >= 1:
        take = ((n_steps - done) // chunk) % 2 == 1

        @pl.when(take)
        def _(done=done, chunk=chunk):
            for u in range(chunk):
                step(done + 1 + u)

        done = done + jnp.where(take, chunk, 0)
        chunk //= 2

    next_own = jnp.minimum(i + 1, n_blocks - 1)
    for hd in range(N_MOBA_HEADS):
        qt_next[hd] = qn_ref[:, head_cols(hd)].T
    for hd in range(N_MOBA_HEADS):
        consume_head(hd, n_pairs - 1)
        produce_first_pair_head(hd, next_own, qt_next)

    for hd in range(N_MOBA_HEADS):
        l = acc_sc[hd, HEAD_DIM:HEAD_DIM + 1, :]
        o = (acc_sc[hd, 0:HEAD_DIM, :] / l).T
        o_ref[:, head_cols(hd)] = (o * sga_ref[:, head_cols(hd)].astype(_F32)).astype(_BF16)
        reset_head(hd)


def _moba(q, k, vt, bias, sga, *, batch, seq):
    N = q.shape[0]
    n_blocks = seq // MOBA_BLOCK
    tq = MOBA_BLOCK
    k3 = k.reshape(batch, seq, MOBA_WIDTH)
    row_block = lambda b, i: (b * n_blocks + i, 0)
    next_row_block = lambda b, i: (b * n_blocks + jnp.minimum(i + 1, n_blocks - 1), 0)
    return pl.pallas_call(
        _moba_kernel,
        grid=(batch, n_blocks),
        in_specs=[
            pl.BlockSpec((tq, MOBA_WIDTH), row_block),
            pl.BlockSpec((tq, MOBA_WIDTH), next_row_block),
            pl.BlockSpec(memory_space=pl.ANY),
            pl.BlockSpec(memory_space=pl.ANY),
            pl.BlockSpec((None, N_MOBA_HEADS, n_blocks, tq), lambda b, i: (b, 0, 0, i)),
            pl.BlockSpec((tq, MOBA_WIDTH), row_block),
        ],
        out_specs=pl.BlockSpec((tq, MOBA_WIDTH), row_block),
        out_shape=jax.ShapeDtypeStruct((N, MOBA_WIDTH), _BF16),
        scratch_shapes=[
            pltpu.VMEM((seq, MOBA_WIDTH), _BF16),
            pltpu.VMEM((n_blocks, MOBA_WIDTH, MOBA_BLOCK), _BF16),
            pltpu.SemaphoreType.DMA((2,)),
            pltpu.SemaphoreType.DMA((2,)),
            pltpu.VMEM((2, N_MOBA_HEADS, HEAD_DIM, tq), _BF16),
            pltpu.VMEM((2, N_MOBA_HEADS, MOBA_BLOCK, tq), _F32),
            pltpu.VMEM((2, N_MOBA_HEADS, tq), _F32),
            pltpu.VMEM((N_MOBA_HEADS, tq), _F32),
            pltpu.VMEM((N_MOBA_HEADS, HEAD_DIM + MOBA_SUM_ROWS, tq), _F32),
        ],
        compiler_params=pltpu.CompilerParams(
            dimension_semantics=("arbitrary", "arbitrary"), vmem_limit_bytes=V7X_VMEM_LIMIT_BYTES),
        name="moba",
    )(q, q, k3, vt, bias, sga)


def _pool_mem_kernel(rest_ref, halo_ref, mk_ref, mv_ref, wp_ref, ps_ref, o_ref, *, tiles_per_batch):
    tm = rest_ref.shape[0]
    tb = lax.rem(pl.program_id(0), tiles_per_batch)
    u0, gp0, qm0, gm0 = 0, POOL_WIDTH, 2 * POOL_WIDTH, 2 * POOL_WIDTH + MEM_WIDTH

    u = rest_ref[:, u0:gp0].astype(_F32)
    halo = jnp.where(tb != 0, halo_ref[...].astype(_F32), 0.0)
    pos = tb * tm + lax.broadcasted_iota(jnp.int32, (tm, POOL_GROUP_DIM), 0)
    for g, w in enumerate(POOL_WINDOWS):
        cols = slice(g * POOL_GROUP_DIM, (g + 1) * POOL_GROUP_DIM)
        u_g = u[:, cols]
        run = jnp.concatenate([halo[:, cols], u_g], axis=0)
        span = 1
        while span < w:
            run = run + pltpu.roll(run, span, axis=0)
            span *= 2
        win = run[POOL_HALO:]
        cnt = jnp.minimum(pos + 1, w).astype(_F32)
        pooled = win / cnt - u_g
        mixed = jnp.dot(pooled.astype(_BF16), wp_ref[g], preferred_element_type=_F32)
        gate = rest_ref[:, gp0 + g * POOL_GROUP_DIM:gp0 + (g + 1) * POOL_GROUP_DIM].astype(_F32)
        o_ref[:, cols] = (mixed * ps_ref[:, cols] * gate).astype(_BF16)

    for hd in range(N_MEM_HEADS):
        cols = slice(hd * HEAD_DIM, (hd + 1) * HEAD_DIM)
        qm = rest_ref[:, qm0 + hd * HEAD_DIM:qm0 + (hd + 1) * HEAD_DIM]
        s = _nt_dot(qm, mk_ref[:, cols])
        m = jnp.max(s, axis=-1, keepdims=True)
        e = jnp.exp2(s - m)
        l = jnp.sum(e, axis=-1, keepdims=True)
        o = jnp.dot(e.astype(_BF16), mv_ref[:, cols], preferred_element_type=_F32) / l
        gate = rest_ref[:, gm0 + hd * HEAD_DIM:gm0 + (hd + 1) * HEAD_DIM].astype(_F32)
        o_ref[:, POOL_WIDTH + hd * HEAD_DIM:POOL_WIDTH + (hd + 1) * HEAD_DIM] = (o * gate).astype(_BF16)


def _pool_mem(rest, mk, mv, wp_bf16, pool_scale, *, seq):
    N = rest.shape[0]
    tm = POOL_MEM_ROWS
    tiles_per_batch = seq // tm
    halo_blocks_per_tile = tm // POOL_HALO
    M = mk.shape[1]
    kern = functools.partial(_pool_mem_kernel, tiles_per_batch=tiles_per_batch)
    return pl.pallas_call(
        kern,
        grid=(N // tm,),
        in_specs=[
            pl.BlockSpec((tm, _REST_W), lambda t: (t, 0)),
            pl.BlockSpec((POOL_HALO, POOL_WIDTH),
                         lambda t: (jnp.maximum(t * halo_blocks_per_tile - 1, 0), 0)),
            pl.BlockSpec((None, M, MEM_WIDTH), lambda t: (t // tiles_per_batch, 0, 0)),
            pl.BlockSpec((None, M, MEM_WIDTH), lambda t: (t // tiles_per_batch, 0, 0)),
            pl.BlockSpec(wp_bf16.shape, lambda t: (0, 0, 0)),
            pl.BlockSpec((1, POOL_WIDTH), lambda t: (0, 0)),
        ],
        out_specs=pl.BlockSpec((tm, POOL_WIDTH + MEM_WIDTH), lambda t: (t, 0)),
        out_shape=jax.ShapeDtypeStruct((N, POOL_WIDTH + MEM_WIDTH), _BF16),
        compiler_params=pltpu.CompilerParams(dimension_semantics=("arbitrary",)),
        name="pool_mem",
    )(rest, rest, mk, mv, wp_bf16, pool_scale)


def _out_proj_kernel(ya_ref, ypm_ref, x_ref, w_hbm, fg_ref, o_ref, w_ref, stage_sc, w_sem,
                     *, final_norm):
    @pl.when(pl.program_id(0) == 0)
    def _():
        _load_weight_as_bf16(w_hbm, w_ref, stage_sc, w_sem)

    for r0 in range(0, x_ref.shape[0], OUT_PROJ_SUBROWS):
        rows = slice(r0, r0 + OUT_PROJ_SUBROWS)
        y = jnp.dot(ya_ref[rows, :], w_ref[0:MOBA_WIDTH, :], preferred_element_type=_F32)
        y = y + jnp.dot(ypm_ref[rows, :], w_ref[MOBA_WIDTH:, :], preferred_element_type=_F32)
        r = x_ref[rows, :] + y
        o_ref[rows, :] = _rmsnorm(r, fg_ref[...]) if final_norm else r


def _out_proj(ya, ypm, xf, w, final_g, *, final_norm):
    N, D = xf.shape
    tm = OUT_PROJ_ROWS
    kern = functools.partial(_out_proj_kernel, final_norm=final_norm)
    return pl.pallas_call(
        kern,
        grid=(N // tm,),
        in_specs=[
            pl.BlockSpec((tm, MOBA_WIDTH), lambda t: (t, 0)),
            pl.BlockSpec((tm, POOL_WIDTH + MEM_WIDTH), lambda t: (t, 0)),
            pl.BlockSpec((tm, D), lambda t: (t, 0)),
            pl.BlockSpec(memory_space=pl.ANY),
            pl.BlockSpec((1, D), lambda t: (0, 0)),
        ],
        out_specs=pl.BlockSpec((tm, D), lambda t: (t, 0)),
        out_shape=jax.ShapeDtypeStruct((N, D), _F32),
        scratch_shapes=[
            pltpu.VMEM(w.shape, _BF16),
            pltpu.VMEM((2, w.shape[0], WEIGHT_STAGE_COLS), _F32),
            pltpu.SemaphoreType.DMA((2,)),
        ],
        compiler_params=pltpu.CompilerParams(
            dimension_semantics=("arbitrary",), vmem_limit_bytes=V7X_VMEM_LIMIT_BYTES),
        name="out_proj",
    )(ya, ypm, xf, w, final_g)


def _layer(xf, mem, norm_g, mem_norm_g, w_in, w_mem_kv, w_pool, pool_scale, w_out, final_g,
           *, batch, seq, final_norm):
    mk, mv = _mem_kv(mem, mem_norm_g[None, :], w_mem_kv.astype(_BF16))
    q, k, vt, sga, rest, bias = _in_proj(xf, norm_g[None, :], w_in, batch=batch, seq=seq)
    ya = _moba(q, k, vt, bias, sga, batch=batch, seq=seq)
    ypm = _pool_mem(rest, mk, mv, w_pool.astype(_BF16), pool_scale[None, :], seq=seq)
    return _out_proj(ya, ypm, xf, w_out, final_g[None, :], final_norm=final_norm)


def kernel(x, mem, norm_g, mem_norm_g, w_in, w_mem_kv, w_pool, pool_scale, w_out, final_norm_g):
    batch, seq, d_model = x.shape
    depth = norm_g.shape[0]
    assert seq % MOBA_BLOCK == 0 and seq % IN_PROJ_ROWS == 0
    assert seq % POOL_MEM_ROWS == 0 and seq % OUT_PROJ_ROWS == 0
    assert w_in.shape[2] == 4 * MOBA_WIDTH + 2 * POOL_WIDTH + 2 * MEM_WIDTH
    xf = x.reshape(batch * seq, d_model)
    for l in range(depth):
        xf = _layer(xf, mem, norm_g[l], mem_norm_g[l], w_in[l], w_mem_kv[l], w_pool[l],
                    pool_scale[l], w_out[l], final_norm_g,
                    batch=batch, seq=seq, final_norm=(l == depth - 1))
    return xf.reshape(batch, seq, d_model)
```

```python
import functools

import jax
import jax.numpy as jnp
from jax import lax
from jax.experimental import pallas as pl
from jax.experimental.pallas import tpu as pltpu

HEAD_DIM = 128
N_MOBA_HEADS = 8
MOBA_WIDTH = N_MOBA_HEADS * HEAD_DIM
N_MEM_HEADS = 4
MEM_WIDTH = N_MEM_HEADS * HEAD_DIM
POOL_WINDOWS = (2, 4, 8, 16)
POOL_GROUP_DIM = 128
POOL_WIDTH = len(POOL_WINDOWS) * POOL_GROUP_DIM
POOL_HALO = 16
MOBA_BLOCK = 256
MOBA_TOPK = 3
EPS = 1e-6
SCALE = HEAD_DIM ** -0.5
LOG2_E = 1.4426950408889634
MOBA_SUM_ROWS = 16
MOBA_STEPS_PER_ITERATION = 4

_Q0, _K0, _V0, _GA0 = 0, MOBA_WIDTH, 2 * MOBA_WIDTH, 3 * MOBA_WIDTH
_REST0 = 4 * MOBA_WIDTH
_REST_W = 2 * POOL_WIDTH + 2 * MEM_WIDTH

V7X_VMEM_LIMIT_BYTES = 56 * 1024 * 1024

IN_PROJ_ROWS = 512
POOL_MEM_ROWS = 1024
OUT_PROJ_ROWS = 512
OUT_PROJ_SUBROWS = 256
WEIGHT_STAGE_COLS = 256

_F32 = jnp.float32
_BF16 = jnp.bfloat16
_NEG_INF = float("-inf")


def _nt_dot(a, b):
    return lax.dot_general(a, b, (((1,), (1,)), ((), ())), preferred_element_type=_F32)


def _split_bf16(a):
    hi = a.astype(_BF16)
    lo = (a - hi.astype(_F32)).astype(_BF16)
    return hi, lo


def _nt_dot_f32(a, b, bt_sc):
    m = a.shape[0]
    a_hi, a_lo = _split_bf16(a)
    b_hi, b_lo = _split_bf16(b)
    bt_sc[0] = b_hi.T
    bt_sc[1] = b_lo.T
    both = jnp.dot(jnp.concatenate([a_hi, a_lo], axis=0), bt_sc[0], preferred_element_type=_F32)
    return both[:m] + (jnp.dot(a_hi, bt_sc[1], preferred_element_type=_F32) + both[m:])


def _silu(a):
    return a * (1.0 / (1.0 + jnp.exp(-a)))


def _rmsnorm(x, g):
    ms = jnp.mean(x * x, axis=-1, keepdims=True)
    return x * lax.rsqrt(ms + EPS) * g


def _load_weight_as_bf16(w_hbm, w_sc, stage_sc, sem):
    n_chunks = w_sc.shape[1] // WEIGHT_STAGE_COLS

    def chunk_copy(c):
        cols = slice(c * WEIGHT_STAGE_COLS, (c + 1) * WEIGHT_STAGE_COLS)
        return pltpu.make_async_copy(w_hbm.at[:, cols], stage_sc.at[c % 2], sem.at[c % 2])

    chunk_copy(0).start()
    for c in range(n_chunks):
        if c + 1 < n_chunks:
            chunk_copy(c + 1).start()
        chunk_copy(c).wait()
        w_sc[:, c * WEIGHT_STAGE_COLS:(c + 1) * WEIGHT_STAGE_COLS] = stage_sc[c % 2].astype(_BF16)


def _mem_kv_kernel(mem_ref, g_ref, w_ref, mk_ref, mv_ref):
    h = _rmsnorm(mem_ref[...], g_ref[...]).astype(_BF16)
    kv = jnp.dot(h, w_ref[...].astype(_BF16), preferred_element_type=_F32)
    mk_ref[...] = kv[:, :MEM_WIDTH].astype(_BF16)
    mv_ref[...] = kv[:, MEM_WIDTH:].astype(_BF16)


def _mem_kv(mem, g, w):
    B, M, D = mem.shape
    return pl.pallas_call(
        _mem_kv_kernel,
        grid=(B,),
        in_specs=[
            pl.BlockSpec((None, M, D), lambda b: (b, 0, 0)),
            pl.BlockSpec((1, D), lambda b: (0, 0)),
            pl.BlockSpec((D, 2 * MEM_WIDTH), lambda b: (0, 0)),
        ],
        out_specs=[
            pl.BlockSpec((None, M, MEM_WIDTH), lambda b: (b, 0, 0)),
            pl.BlockSpec((None, M, MEM_WIDTH), lambda b: (b, 0, 0)),
        ],
        out_shape=[jax.ShapeDtypeStruct((B, M, MEM_WIDTH), _BF16)] * 2,
        compiler_params=pltpu.CompilerParams(dimension_semantics=("arbitrary",)),
        name="mem_kv",
    )(mem, g, w)


def _in_proj_kernel(x_ref, g_ref, w_hbm,
                    q_ref, k_ref, vt_ref, sga_ref, rest_ref, bias_ref,
                    w_ref, stage_sc, w_sem, kmean_sc, qt_sc, *, tiles_per_batch, n_blocks):
    tm = x_ref.shape[0]
    blocks_per_tile = tm // MOBA_BLOCK
    t = pl.program_id(0)
    first_block = lax.rem(t, tiles_per_batch) * blocks_per_tile

    @pl.when(t == 0)
    def _():
        _load_weight_as_bf16(w_hbm, w_ref, stage_sc, w_sem)
        kmean_sc[...] = jnp.zeros_like(kmean_sc)

    h = _rmsnorm(x_ref[...], g_ref[...]).astype(_BF16)

    def project(col0, width):
        return jnp.dot(h, w_ref[:, col0:col0 + width], preferred_element_type=_F32)

    k = project(_K0, MOBA_WIDTH)
    k_ref[...] = k.astype(_BF16)
    for b in range(blocks_per_tile):
        kmean_sc[pl.ds(first_block + b, 1), :] = jnp.mean(
            k[b * MOBA_BLOCK:(b + 1) * MOBA_BLOCK], axis=0, keepdims=True)

    q = project(_Q0, MOBA_WIDTH)
    q_ref[...] = (q * (SCALE * LOG2_E)).astype(_BF16)

    n_idx = lax.broadcasted_iota(jnp.int32, (n_blocks, tm), 0)
    col = lax.broadcasted_iota(jnp.int32, (n_blocks, tm), 1)
    own_block = first_block + col // MOBA_BLOCK
    is_past = n_idx < own_block

    def gate_head(hd):
        cols = slice(hd * HEAD_DIM, (hd + 1) * HEAD_DIM)
        gate = jnp.where(is_past, _nt_dot_f32(kmean_sc[:, cols], q[:, cols], qt_sc.at[hd % 2]), _NEG_INF)
        keep = None
        for _ in range(MOBA_TOPK):
            best = jnp.max(gate, axis=0, keepdims=True)
            first = jnp.min(jnp.where(gate == best, n_idx, n_blocks), axis=0, keepdims=True)
            pick = n_idx == first
            keep = pick if keep is None else keep | pick
            gate = jnp.where(pick, _NEG_INF, gate)
        bias_ref[hd] = jnp.where(keep & is_past, 0.0, _NEG_INF)

    chunks_per_projection = N_MOBA_HEADS // 2
    chunk_w = MOBA_WIDTH // chunks_per_projection
    for c in range(chunks_per_projection):
        cols = slice(c * chunk_w, (c + 1) * chunk_w)
        v = project(_V0 + c * chunk_w, chunk_w)
        for b in range(blocks_per_tile):
            vt_ref[b, cols, :] = v[b * MOBA_BLOCK:(b + 1) * MOBA_BLOCK].T.astype(_BF16)
        gate_head(c)
    for c in range(chunks_per_projection):
        cols = slice(c * chunk_w, (c + 1) * chunk_w)
        sga_ref[:, cols] = _silu(project(_GA0 + c * chunk_w, chunk_w)).astype(_BF16)
        gate_head(chunks_per_projection + c)

    rest = project(_REST0, _REST_W)
    u0, gp0, qm0, gm0 = 0, POOL_WIDTH, 2 * POOL_WIDTH, 2 * POOL_WIDTH + MEM_WIDTH
    rest_ref[:, u0:gp0] = rest[:, u0:gp0].astype(_BF16)
    rest_ref[:, gp0:qm0] = _silu(rest[:, gp0:qm0]).astype(_BF16)
    rest_ref[:, qm0:gm0] = (rest[:, qm0:gm0] * (SCALE * LOG2_E)).astype(_BF16)
    rest_ref[:, gm0:] = _silu(rest[:, gm0:]).astype(_BF16)


def _in_proj(xf, g, w, *, batch, seq):
    N, D = xf.shape
    tm = IN_PROJ_ROWS
    tiles_per_batch = seq // tm
    blocks_per_tile = tm // MOBA_BLOCK
    n_blocks = seq // MOBA_BLOCK
    kern = functools.partial(_in_proj_kernel, tiles_per_batch=tiles_per_batch, n_blocks=n_blocks)
    return pl.pallas_call(
        kern,
        grid=(N // tm,),
        in_specs=[
            pl.BlockSpec((tm, D), lambda t: (t, 0)),
            pl.BlockSpec((1, D), lambda t: (0, 0)),
            pl.BlockSpec(memory_space=pl.ANY),
        ],
        out_specs=[
            pl.BlockSpec((tm, MOBA_WIDTH), lambda t: (t, 0)),
            pl.BlockSpec((tm, MOBA_WIDTH), lambda t: (t, 0)),
            pl.BlockSpec((None, blocks_per_tile, MOBA_WIDTH, MOBA_BLOCK),
                         lambda t: (t // tiles_per_batch, t % tiles_per_batch, 0, 0)),
            pl.BlockSpec((tm, MOBA_WIDTH), lambda t: (t, 0)),
            pl.BlockSpec((tm, _REST_W), lambda t: (t, 0)),
            pl.BlockSpec((None, N_MOBA_HEADS, n_blocks, tm),
                         lambda t: (t // tiles_per_batch, 0, 0, t % tiles_per_batch)),
        ],
        out_shape=[
            jax.ShapeDtypeStruct((N, MOBA_WIDTH), _BF16),
            jax.ShapeDtypeStruct((N, MOBA_WIDTH), _BF16),
            jax.ShapeDtypeStruct((batch, n_blocks, MOBA_WIDTH, MOBA_BLOCK), _BF16),
            jax.ShapeDtypeStruct((N, MOBA_WIDTH), _BF16),
            jax.ShapeDtypeStruct((N, _REST_W), _BF16),
            jax.ShapeDtypeStruct((batch, N_MOBA_HEADS, n_blocks, seq), _F32),
        ],
        scratch_shapes=[
            pltpu.VMEM(w.shape, _BF16),
            pltpu.VMEM((2, D, WEIGHT_STAGE_COLS), _F32),
            pltpu.SemaphoreType.DMA((2,)),
            pltpu.VMEM((n_blocks, MOBA_WIDTH), _F32),
            pltpu.VMEM((2, 2, HEAD_DIM, tm), _BF16),
        ],
        compiler_params=pltpu.CompilerParams(
            dimension_semantics=("arbitrary",), vmem_limit_bytes=V7X_VMEM_LIMIT_BYTES),
        name="in_proj",
    )(xf, g, w)


def _moba_kernel(q_ref, qn_ref, k_hbm, vt_hbm, bias_ref, sga_ref, o_ref,
                 k_ref, vt_ref, k_sem, vt_sem, qt_sc, s_sc, smax_sc, m_sc, acc_sc):
    tq = q_ref.shape[0]
    b = pl.program_id(0)
    i = pl.program_id(1)
    n_blocks = vt_ref.shape[0]
    ones_rows = jnp.ones((MOBA_SUM_ROWS, MOBA_BLOCK), _BF16)

    def head_cols(hd):
        return slice(hd * HEAD_DIM, (hd + 1) * HEAD_DIM)

    def kv_block_copies(j):
        rows = pl.ds(pl.multiple_of(j * MOBA_BLOCK, MOBA_BLOCK), MOBA_BLOCK)
        return (pltpu.make_async_copy(k_hbm.at[b, rows, :], k_ref.at[rows, :], k_sem.at[j % 2]),
                pltpu.make_async_copy(vt_hbm.at[b, j], vt_ref.at[j], vt_sem.at[j % 2]))

    def start_kv_block(j):
        for copy in kv_block_copies(j):
            copy.start()

    def wait_kv_block(j):
        for copy in kv_block_copies(j):
            copy.wait()

    @pl.when(i == 0)
    def _():
        start_kv_block(0)
        start_kv_block(1)
        wait_kv_block(0)
        wait_kv_block(1)

    @pl.when((i > 0) & (i + 1 < n_blocks))
    def _():
        wait_kv_block(i + 1)

    @pl.when(i + 2 < n_blocks)
    def _():
        start_kv_block(i + 2)

    def produce_head(hd, j, slot, mask=lambda s: s, *, qt):
        start = pl.multiple_of(j * MOBA_BLOCK, MOBA_BLOCK)
        s = mask(jnp.dot(k_ref[pl.ds(start, MOBA_BLOCK), head_cols(hd)], qt[hd],
                         preferred_element_type=_F32))
        s_sc[slot, hd] = s
        smax_sc[slot, pl.ds(hd, 1), :] = jnp.max(s, axis=0, keepdims=True)

    kpos = lax.broadcasted_iota(jnp.int32, (MOBA_BLOCK, tq), 0)
    qpos = lax.broadcasted_iota(jnp.int32, (MOBA_BLOCK, tq), 1)
    causal = kpos <= qpos

    def produce_first_pair_head(hd, own_block, qt):
        produce_head(hd, own_block, 0, lambda s: jnp.where(causal, s, _NEG_INF), qt=qt)
        produce_head(hd, 0, 1, qt=qt)

    n_pairs = (i + 2) // 2

    def pair_blocks(pair):
        return jnp.where(pair == 0, i, 2 * pair - 1), 2 * pair

    def consume_head(hd, pair):
        row = pl.ds(hd, 1)
        blocks = pair_blocks(pair)
        biases = (jnp.where(pair == 0, 0.0, bias_ref[hd, pl.ds(blocks[0], 1), :]),
                  bias_ref[hd, pl.ds(blocks[1], 1), :])
        m = m_sc[row, :]
        m_new = m
        for slot, bias in enumerate(biases):
            m_new = jnp.maximum(m_new, smax_sc[slot, row, :] + bias)
        alpha = jnp.exp2(m - m_new)
        m_sc[row, :] = m_new
        pv = None
        for slot, (j, bias) in enumerate(zip(blocks, biases)):
            p = jnp.exp2(s_sc[slot, hd] - (m_new - bias))
            v_and_ones = jnp.concatenate([vt_ref[j, head_cols(hd), :], ones_rows], axis=0)
            d = jnp.dot(v_and_ones, p.astype(_BF16), preferred_element_type=_F32)
            pv = d if pv is None else pv + d
        acc_sc[hd] = alpha * acc_sc[hd] + pv

    def reset_head(hd):
        m_sc[pl.ds(hd, 1), :] = jnp.full((1, tq), _NEG_INF, _F32)
        acc_sc[hd] = jnp.zeros(acc_sc.shape[1:], _F32)

    qt_cur, qt_next = qt_sc.at[i % 2], qt_sc.at[(i + 1) % 2]

    @pl.when(i == 0)
    def _():
        for hd in range(N_MOBA_HEADS):
            reset_head(hd)
            qt_cur[hd] = q_ref[:, head_cols(hd)].T
            produce_first_pair_head(hd, i, qt_cur)

    def step(pair):
        for hd in range(N_MOBA_HEADS):
            consume_head(hd, pair - 1)
            produce_head(hd, 2 * pair - 1, 0, qt=qt_cur)
            produce_head(hd, 2 * pair, 1, qt=qt_cur)

    def body(it, carry):
        for u in range(MOBA_STEPS_PER_ITERATION):
            step(MOBA_STEPS_PER_ITERATION * it + 1 + u)
        return carry

    n_steps = n_pairs - 1
    n_iterations = n_steps // MOBA_STEPS_PER_ITERATION
    lax.fori_loop(0, n_iterations, body, 0)

    done = MOBA_STEPS_PER_ITERATION * n_iterations
    chunk = MOBA_STEPS_PER_ITERATION // 2
    while chunk >= 1:
        take = ((n_steps - done) // chunk) % 2 == 1

        @pl.when(take)
        def _(done=done, chunk=chunk):
            for u in range(chunk):
                step(done + 1 + u)

        done = done + jnp.where(take, chunk, 0)
        chunk //= 2

    next_own = jnp.minimum(i + 1, n_blocks - 1)
    for hd in range(N_MOBA_HEADS):
        qt_next[hd] = qn_ref[:, head_cols(hd)].T
    for hd in range(N_MOBA_HEADS):
        consume_head(hd, n_pairs - 1)
        produce_first_pair_head(hd, next_own, qt_next)

    for hd in range(N_MOBA_HEADS):
        l = acc_sc[hd, HEAD_DIM:HEAD_DIM + 1, :]
        o = (acc_sc[hd, 0:HEAD_DIM, :] / l).T
        o_ref[:, head_cols(hd)] = (o * sga_ref[:, head_cols(hd)].astype(_F32)).astype(_BF16)
        reset_head(hd)


def _moba(q, k, vt, bias, sga, *, batch, seq):
    N = q.shape[0]
    n_blocks = seq // MOBA_BLOCK
    tq = MOBA_BLOCK
    k3 = k.reshape(batch, seq, MOBA_WIDTH)
    row_block = lambda b, i: (b * n_blocks + i, 0)
    next_row_block = lambda b, i: (b * n_blocks + jnp.minimum(i + 1, n_blocks - 1), 0)
    return pl.pallas_call(
        _moba_kernel,
        grid=(batch, n_blocks),
        in_specs=[
            pl.BlockSpec((tq, MOBA_WIDTH), row_block),
            pl.BlockSpec((tq, MOBA_WIDTH), next_row_block),
            pl.BlockSpec(memory_space=pl.ANY),
            pl.BlockSpec(memory_space=pl.ANY),
            pl.BlockSpec((None, N_MOBA_HEADS, n_blocks, tq), lambda b, i: (b, 0, 0, i)),
            pl.BlockSpec((tq, MOBA_WIDTH), row_block),
        ],
        out_specs=pl.BlockSpec((tq, MOBA_WIDTH), row_block),
        out_shape=jax.ShapeDtypeStruct((N, MOBA_WIDTH), _BF16),
        scratch_shapes=[
            pltpu.VMEM((seq, MOBA_WIDTH), _BF16),
            pltpu.VMEM((n_blocks, MOBA_WIDTH, MOBA_BLOCK), _BF16),
            pltpu.SemaphoreType.DMA((2,)),
            pltpu.SemaphoreType.DMA((2,)),
            pltpu.VMEM((2, N_MOBA_HEADS, HEAD_DIM, tq), _BF16),
            pltpu.VMEM((2, N_MOBA_HEADS, MOBA_BLOCK, tq), _F32),
            pltpu.VMEM((2, N_MOBA_HEADS, tq), _F32),
            pltpu.VMEM((N_MOBA_HEADS, tq), _F32),
            pltpu.VMEM((N_MOBA_HEADS, HEAD_DIM + MOBA_SUM_ROWS, tq), _F32),
        ],
        compiler_params=pltpu.CompilerParams(
            dimension_semantics=("arbitrary", "arbitrary"), vmem_limit_bytes=V7X_VMEM_LIMIT_BYTES),
        name="moba",
    )(q, q, k3, vt, bias, sga)


def _pool_mem_kernel(rest_ref, halo_ref, mk_ref, mv_ref, wp_ref, ps_ref, o_ref, *, tiles_per_batch):
    tm = rest_ref.shape[0]
    tb = lax.rem(pl.program_id(0), tiles_per_batch)
    u0, gp0, qm0, gm0 = 0, POOL_WIDTH, 2 * POOL_WIDTH, 2 * POOL_WIDTH + MEM_WIDTH

    u = rest_ref[:, u0:gp0].astype(_F32)
    halo = jnp.where(tb != 0, halo_ref[...].astype(_F32), 0.0)
    tokens_so_far = (tb * tm + 1 + lax.broadcasted_iota(jnp.int32, (tm, POOL_GROUP_DIM), 0)).astype(_F32)
    for g, w in enumerate(POOL_WINDOWS):
        cols = slice(g * POOL_GROUP_DIM, (g + 1) * POOL_GROUP_DIM)
        u_g = u[:, cols]
        run = jnp.concatenate([halo[:, cols], u_g], axis=0)
        span = 1
        while span < w:
            run = run + pltpu.roll(run, span, axis=0)
            span *= 2
        win = run[POOL_HALO:]
        cnt = jnp.minimum(tokens_so_far, float(w))
        pooled = win / cnt - u_g
        mixed = jnp.dot(pooled.astype(_BF16), wp_ref[g].astype(_BF16), preferred_element_type=_F32)
        gate = rest_ref[:, gp0 + g * POOL_GROUP_DIM:gp0 + (g + 1) * POOL_GROUP_DIM].astype(_F32)
        o_ref[:, cols] = (mixed * ps_ref[:, cols] * gate).astype(_BF16)

    for hd in range(N_MEM_HEADS):
        cols = slice(hd * HEAD_DIM, (hd + 1) * HEAD_DIM)
        qm = rest_ref[:, qm0 + hd * HEAD_DIM:qm0 + (hd + 1) * HEAD_DIM]
        s = _nt_dot(qm, mk_ref[:, cols])
        m = jnp.max(s, axis=-1, keepdims=True)
        e = jnp.exp2(s - m)
        l = jnp.sum(e, axis=-1, keepdims=True)
        o = jnp.dot(e.astype(_BF16), mv_ref[:, cols], preferred_element_type=_F32) / l
        gate = rest_ref[:, gm0 + hd * HEAD_DIM:gm0 + (hd + 1) * HEAD_DIM].astype(_F32)
        o_ref[:, POOL_WIDTH + hd * HEAD_DIM:POOL_WIDTH + (hd + 1) * HEAD_DIM] = (o * gate).astype(_BF16)


def _pool_mem(rest, mk, mv, wp, pool_scale, *, seq):
    N = rest.shape[0]
    tm = POOL_MEM_ROWS
    tiles_per_batch = seq // tm
    halo_blocks_per_tile = tm // POOL_HALO
    M = mk.shape[1]
    kern = functools.partial(_pool_mem_kernel, tiles_per_batch=tiles_per_batch)
    return pl.pallas_call(
        kern,
        grid=(N // tm,),
        in_specs=[
            pl.BlockSpec((tm, _REST_W), lambda t: (t, 0)),
            pl.BlockSpec((POOL_HALO, POOL_WIDTH),
                         lambda t: (jnp.maximum(t * halo_blocks_per_tile - 1, 0), 0)),
            pl.BlockSpec((None, M, MEM_WIDTH), lambda t: (t // tiles_per_batch, 0, 0)),
            pl.BlockSpec((None, M, MEM_WIDTH), lambda t: (t // tiles_per_batch, 0, 0)),
            pl.BlockSpec(wp.shape, lambda t: (0, 0, 0)),
            pl.BlockSpec((1, POOL_WIDTH), lambda t: (0, 0)),
        ],
        out_specs=pl.BlockSpec((tm, POOL_WIDTH + MEM_WIDTH), lambda t: (t, 0)),
        out_shape=jax.ShapeDtypeStruct((N, POOL_WIDTH + MEM_WIDTH), _BF16),
        compiler_params=pltpu.CompilerParams(dimension_semantics=("arbitrary",)),
        name="pool_mem",
    )(rest, rest, mk, mv, wp, pool_scale)


def _out_proj_kernel(ya_ref, ypm_ref, x_ref, w_hbm, fg_ref, o_ref, w_ref, stage_sc, w_sem,
                     *, final_norm):
    @pl.when(pl.program_id(0) == 0)
    def _():
        _load_weight_as_bf16(w_hbm, w_ref, stage_sc, w_sem)

    for r0 in range(0, x_ref.shape[0], OUT_PROJ_SUBROWS):
        rows = slice(r0, r0 + OUT_PROJ_SUBROWS)
        y = jnp.dot(ya_ref[rows, :], w_ref[0:MOBA_WIDTH, :], preferred_element_type=_F32)
        y = y + jnp.dot(ypm_ref[rows, :], w_ref[MOBA_WIDTH:, :], preferred_element_type=_F32)
        r = x_ref[rows, :] + y
        o_ref[rows, :] = _rmsnorm(r, fg_ref[...]) if final_norm else r


def _out_proj(ya, ypm, xf, w, final_g, *, final_norm):
    N, D = xf.shape
    tm = OUT_PROJ_ROWS
    kern = functools.partial(_out_proj_kernel, final_norm=final_norm)
    return pl.pallas_call(
        kern,
        grid=(N // tm,),
        in_specs=[
            pl.BlockSpec((tm, MOBA_WIDTH), lambda t: (t, 0)),
            pl.BlockSpec((tm, POOL_WIDTH + MEM_WIDTH), lambda t: (t, 0)),
            pl.BlockSpec((tm, D), lambda t: (t, 0)),
            pl.BlockSpec(memory_space=pl.ANY),
            pl.BlockSpec((1, D), lambda t: (0, 0)),
        ],
        out_specs=pl.BlockSpec((tm, D), lambda t: (t, 0)),
        out_shape=jax.ShapeDtypeStruct((N, D), _F32),
        scratch_shapes=[
            pltpu.VMEM(w.shape, _BF16),
            pltpu.VMEM((2, w.shape[0], WEIGHT_STAGE_COLS), _F32),
            pltpu.SemaphoreType.DMA((2,)),
        ],
        compiler_params=pltpu.CompilerParams(
            dimension_semantics=("arbitrary",), vmem_limit_bytes=V7X_VMEM_LIMIT_BYTES),
        name="out_proj",
    )(ya, ypm, xf, w, final_g)


def _layer(xf, mem, norm_g, mem_norm_g, w_in, w_mem_kv, w_pool, pool_scale, w_out, final_g,
           *, batch, seq, final_norm):
    mk, mv = _mem_kv(mem, mem_norm_g[None, :], w_mem_kv)
    q, k, vt, sga, rest, bias = _in_proj(xf, norm_g[None, :], w_in, batch=batch, seq=seq)
    ya = _moba(q, k, vt, bias, sga, batch=batch, seq=seq)
    ypm = _pool_mem(rest, mk, mv, w_pool, pool_scale[None, :], seq=seq)
    return _out_proj(ya, ypm, xf, w_out, final_g[None, :], final_norm=final_norm)


def kernel(x, mem, norm_g, mem_norm_g, w_in, w_mem_kv, w_pool, pool_scale, w_out, final_norm_g):
    batch, seq, d_model = x.shape
    depth = norm_g.shape[0]
    assert seq % MOBA_BLOCK == 0 and seq % IN_PROJ_ROWS == 0
    assert seq % POOL_MEM_ROWS == 0 and seq % OUT_PROJ_ROWS == 0
    assert w_in.shape[2] == 4 * MOBA_WIDTH + 2 * POOL_WIDTH + 2 * MEM_WIDTH
    xf = x.reshape(batch * seq, d_model)
    for l in range(depth):
        xf = _layer(xf, mem, norm_g[l], mem_norm_g[l], w_in[l], w_mem_kv[l], w_pool[l],
                    pool_scale[l], w_out[l], final_norm_g,
                    batch=batch, seq=seq, final_norm=(l == depth - 1))
    return xf.reshape(batch, seq, d_model)
```

```python
import functools

import jax
import jax.numpy as jnp
from jax import lax
from jax.experimental import pallas as pl
from jax.experimental.pallas import tpu as pltpu

HEAD_DIM = 128
N_MOBA_HEADS = 8
MOBA_WIDTH = N_MOBA_HEADS * HEAD_DIM
N_MEM_HEADS = 4
MEM_WIDTH = N_MEM_HEADS * HEAD_DIM
POOL_WINDOWS = (2, 4, 8, 16)
POOL_GROUP_DIM = 128
POOL_WIDTH = len(POOL_WINDOWS) * POOL_GROUP_DIM
POOL_HALO = 16
MOBA_BLOCK = 256
MOBA_TOPK = 3
EPS = 1e-6
SCALE = HEAD_DIM ** -0.5
LOG2_E = 1.4426950408889634
MOBA_SUM_ROWS = 16
MOBA_STEPS_PER_ITERATION = 4

_Q0, _K0, _V0, _GA0 = 0, MOBA_WIDTH, 2 * MOBA_WIDTH, 3 * MOBA_WIDTH
_REST0 = 4 * MOBA_WIDTH
_REST_W = 2 * POOL_WIDTH + 2 * MEM_WIDTH

V7X_VMEM_LIMIT_BYTES = 56 * 1024 * 1024

IN_PROJ_ROWS = 512
POOL_MEM_ROWS = 1024
OUT_PROJ_ROWS = 512
OUT_PROJ_SUBROWS = 256
WEIGHT_STAGE_COLS = 256

_F32 = jnp.float32
_BF16 = jnp.bfloat16
_NEG_INF = float("-inf")


def _nt_dot(a, b):
    return lax.dot_general(a, b, (((1,), (1,)), ((), ())), preferred_element_type=_F32)


def _split_bf16(a):
    hi = a.astype(_BF16)
    lo = (a - hi.astype(_F32)).astype(_BF16)
    return hi, lo


def _nt_dot_f32(a, b, bt_sc):
    m = a.shape[0]
    a_hi, a_lo = _split_bf16(a)
    b_hi, b_lo = _split_bf16(b)
    bt_sc[0] = b_hi.T
    bt_sc[1] = b_lo.T
    both = jnp.dot(jnp.concatenate([a_hi, a_lo], axis=0), bt_sc[0], preferred_element_type=_F32)
    return both[:m] + (jnp.dot(a_hi, bt_sc[1], preferred_element_type=_F32) + both[m:])


def _silu(a):
    return a * (1.0 / (1.0 + jnp.exp(-a)))


def _rmsnorm(x, g):
    ms = jnp.mean(x * x, axis=-1, keepdims=True)
    return x * lax.rsqrt(ms + EPS) * g


def _load_weight_as_bf16(w_hbm, w_sc, stage_sc, sem):
    n_chunks = w_sc.shape[1] // WEIGHT_STAGE_COLS

    def chunk_copy(c):
        cols = slice(c * WEIGHT_STAGE_COLS, (c + 1) * WEIGHT_STAGE_COLS)
        return pltpu.make_async_copy(w_hbm.at[:, cols], stage_sc.at[c % 2], sem.at[c % 2])

    chunk_copy(0).start()
    for c in range(n_chunks):
        if c + 1 < n_chunks:
            chunk_copy(c + 1).start()
        chunk_copy(c).wait()
        w_sc[:, c * WEIGHT_STAGE_COLS:(c + 1) * WEIGHT_STAGE_COLS] = stage_sc[c % 2].astype(_BF16)


def _mem_kv_kernel(mem_ref, g_ref, w_ref, mk_ref, mv_ref):
    h = _rmsnorm(mem_ref[...], g_ref[...]).astype(_BF16)
    kv = jnp.dot(h, w_ref[...].astype(_BF16), preferred_element_type=_F32)
    mk_ref[...] = kv[:, :MEM_WIDTH].astype(_BF16)
    mv_ref[...] = kv[:, MEM_WIDTH:].astype(_BF16)


def _mem_kv(mem, g, w):
    B, M, D = mem.shape
    return pl.pallas_call(
        _mem_kv_kernel,
        grid=(B,),
        in_specs=[
            pl.BlockSpec((None, M, D), lambda b: (b, 0, 0)),
            pl.BlockSpec((1, D), lambda b: (0, 0)),
            pl.BlockSpec((D, 2 * MEM_WIDTH), lambda b: (0, 0)),
        ],
        out_specs=[
            pl.BlockSpec((None, M, MEM_WIDTH), lambda b: (b, 0, 0)),
            pl.BlockSpec((None, M, MEM_WIDTH), lambda b: (b, 0, 0)),
        ],
        out_shape=[jax.ShapeDtypeStruct((B, M, MEM_WIDTH), _BF16)] * 2,
        compiler_params=pltpu.CompilerParams(dimension_semantics=("arbitrary",)),
        name="mem_kv",
    )(mem, g, w)


def _in_proj_kernel(x_ref, g_ref, w_hbm,
                    q_ref, k_ref, vt_ref, sga_ref, rest_ref, bias_ref,
                    w_ref, stage_sc, w_sem, kmean_sc, qt_sc, *, tiles_per_batch, n_blocks):
    tm = x_ref.shape[0]
    blocks_per_tile = tm // MOBA_BLOCK
    t = pl.program_id(0)
    first_block = lax.rem(t, tiles_per_batch) * blocks_per_tile

    @pl.when(t == 0)
    def _():
        _load_weight_as_bf16(w_hbm, w_ref, stage_sc, w_sem)
        kmean_sc[...] = jnp.zeros_like(kmean_sc)

    h = _rmsnorm(x_ref[...], g_ref[...]).astype(_BF16)

    def project(col0, width):
        return jnp.dot(h, w_ref[:, col0:col0 + width], preferred_element_type=_F32)

    k = project(_K0, MOBA_WIDTH)
    k_ref[...] = k.astype(_BF16)
    for b in range(blocks_per_tile):
        kmean_sc[pl.ds(first_block + b, 1), :] = jnp.mean(
            k[b * MOBA_BLOCK:(b + 1) * MOBA_BLOCK], axis=0, keepdims=True)

    q = project(_Q0, MOBA_WIDTH)
    q_ref[...] = (q * (SCALE * LOG2_E)).astype(_BF16)

    n_idx = lax.broadcasted_iota(jnp.int32, (n_blocks, tm), 0)
    col = lax.broadcasted_iota(jnp.int32, (n_blocks, tm), 1)
    own_block = first_block + col // MOBA_BLOCK
    is_past = n_idx < own_block

    def gate_head(hd):
        cols = slice(hd * HEAD_DIM, (hd + 1) * HEAD_DIM)
        gate = jnp.where(is_past, _nt_dot_f32(kmean_sc[:, cols], q[:, cols], qt_sc.at[hd % 2]), _NEG_INF)
        keep = None
        for _ in range(MOBA_TOPK):
            best = jnp.max(gate, axis=0, keepdims=True)
            first = jnp.min(jnp.where(gate == best, n_idx, n_blocks), axis=0, keepdims=True)
            pick = n_idx == first
            keep = pick if keep is None else keep | pick
            gate = jnp.where(pick, _NEG_INF, gate)
        bias_ref[hd] = jnp.where(keep & is_past, 0.0, _NEG_INF)

    chunks_per_projection = N_MOBA_HEADS // 2
    chunk_w = MOBA_WIDTH // chunks_per_projection
    for c in range(chunks_per_projection):
        cols = slice(c * chunk_w, (c + 1) * chunk_w)
        v = project(_V0 + c * chunk_w, chunk_w)
        for b in range(blocks_per_tile):
            vt_ref[b, cols, :] = v[b * MOBA_BLOCK:(b + 1) * MOBA_BLOCK].T.astype(_BF16)
        gate_head(c)
    for c in range(chunks_per_projection):
        cols = slice(c * chunk_w, (c + 1) * chunk_w)
        sga_ref[:, cols] = _silu(project(_GA0 + c * chunk_w, chunk_w)).astype(_BF16)
        gate_head(chunks_per_projection + c)

    u0, gp0, qm0, gm0 = 0, POOL_WIDTH, 2 * POOL_WIDTH, 2 * POOL_WIDTH + MEM_WIDTH
    rest_ref[:, gp0:qm0] = _silu(project(_REST0 + gp0, POOL_WIDTH)).astype(_BF16)
    rest_ref[:, gm0:] = _silu(project(_REST0 + gm0, MEM_WIDTH)).astype(_BF16)
    rest_ref[:, qm0:gm0] = (project(_REST0 + qm0, MEM_WIDTH) * (SCALE * LOG2_E)).astype(_BF16)
    rest_ref[:, u0:gp0] = project(_REST0 + u0, POOL_WIDTH).astype(_BF16)


def _in_proj(xf, g, w, *, batch, seq):
    N, D = xf.shape
    tm = IN_PROJ_ROWS
    tiles_per_batch = seq // tm
    blocks_per_tile = tm // MOBA_BLOCK
    n_blocks = seq // MOBA_BLOCK
    kern = functools.partial(_in_proj_kernel, tiles_per_batch=tiles_per_batch, n_blocks=n_blocks)
    return pl.pallas_call(
        kern,
        grid=(N // tm,),
        in_specs=[
            pl.BlockSpec((tm, D), lambda t: (t, 0)),
            pl.BlockSpec((1, D), lambda t: (0, 0)),
            pl.BlockSpec(memory_space=pl.ANY),
        ],
        out_specs=[
            pl.BlockSpec((tm, MOBA_WIDTH), lambda t: (t, 0)),
            pl.BlockSpec((tm, MOBA_WIDTH), lambda t: (t, 0)),
            pl.BlockSpec((None, blocks_per_tile, MOBA_WIDTH, MOBA_BLOCK),
                         lambda t: (t // tiles_per_batch, t % tiles_per_batch, 0, 0)),
            pl.BlockSpec((tm, MOBA_WIDTH), lambda t: (t, 0)),
            pl.BlockSpec((tm, _REST_W), lambda t: (t, 0)),
            pl.BlockSpec((None, N_MOBA_HEADS, n_blocks, tm),
                         lambda t: (t // tiles_per_batch, 0, 0, t % tiles_per_batch)),
        ],
        out_shape=[
            jax.ShapeDtypeStruct((N, MOBA_WIDTH), _BF16),
            jax.ShapeDtypeStruct((N, MOBA_WIDTH), _BF16),
            jax.ShapeDtypeStruct((batch, n_blocks, MOBA_WIDTH, MOBA_BLOCK), _BF16),
            jax.ShapeDtypeStruct((N, MOBA_WIDTH), _BF16),
            jax.ShapeDtypeStruct((N, _REST_W), _BF16),
            jax.ShapeDtypeStruct((batch, N_MOBA_HEADS, n_blocks, seq), _F32),
        ],
        scratch_shapes=[
            pltpu.VMEM(w.shape, _BF16),
            pltpu.VMEM((2, D, WEIGHT_STAGE_COLS), _F32),
            pltpu.SemaphoreType.DMA((2,)),
            pltpu.VMEM((n_blocks, MOBA_WIDTH), _F32),
            pltpu.VMEM((2, 2, HEAD_DIM, tm), _BF16),
        ],
        compiler_params=pltpu.CompilerParams(
            dimension_semantics=("arbitrary",), vmem_limit_bytes=V7X_VMEM_LIMIT_BYTES),
        name="in_proj",
    )(xf, g, w)


def _moba_kernel(q_ref, qn_ref, k_hbm, vt_hbm, bias_ref, sga_ref, o_ref,
                 k_ref, vt_ref, k_sem, vt_sem, qt_sc, s_sc, smax_sc, m_sc, acc_sc):
    tq = q_ref.shape[0]
    b = pl.program_id(0)
    i = pl.program_id(1)
    n_blocks = vt_ref.shape[0]
    ones_rows = jnp.ones((MOBA_SUM_ROWS, MOBA_BLOCK), _BF16)

    def head_cols(hd):
        return slice(hd * HEAD_DIM, (hd + 1) * HEAD_DIM)

    def kv_block_copies(j):
        rows = pl.ds(pl.multiple_of(j * MOBA_BLOCK, MOBA_BLOCK), MOBA_BLOCK)
        return (pltpu.make_async_copy(k_hbm.at[b, rows, :], k_ref.at[rows, :], k_sem.at[j % 2]),
                pltpu.make_async_copy(vt_hbm.at[b, j], vt_ref.at[j], vt_sem.at[j % 2]))

    def start_kv_block(j):
        for copy in kv_block_copies(j):
            copy.start()

    def wait_kv_block(j):
        for copy in kv_block_copies(j):
            copy.wait()

    @pl.when(i == 0)
    def _():
        start_kv_block(0)
        start_kv_block(1)
        wait_kv_block(0)
        wait_kv_block(1)

    @pl.when((i > 0) & (i + 1 < n_blocks))
    def _():
        wait_kv_block(i + 1)

    @pl.when(i + 2 < n_blocks)
    def _():
        start_kv_block(i + 2)

    def produce_head(hd, j, slot, mask=lambda s: s, *, qt):
        start = pl.multiple_of(j * MOBA_BLOCK, MOBA_BLOCK)
        s = mask(jnp.dot(k_ref[pl.ds(start, MOBA_BLOCK), head_cols(hd)], qt[hd],
                         preferred_element_type=_F32))
        s_sc[slot, hd] = s
        smax_sc[slot, pl.ds(hd, 1), :] = jnp.max(s, axis=0, keepdims=True)

    kpos = lax.broadcasted_iota(jnp.int32, (MOBA_BLOCK, tq), 0)
    qpos = lax.broadcasted_iota(jnp.int32, (MOBA_BLOCK, tq), 1)
    causal = kpos <= qpos

    def produce_first_pair_head(hd, own_block, qt):
        produce_head(hd, own_block, 0, lambda s: jnp.where(causal, s, _NEG_INF), qt=qt)
        produce_head(hd, 0, 1, qt=qt)

    n_pairs = (i + 2) // 2

    def pair_blocks(pair):
        return jnp.where(pair == 0, i, 2 * pair - 1), 2 * pair

    def consume_head(hd, pair, n_live=2):
        row = pl.ds(hd, 1)
        blocks = pair_blocks(pair)[:n_live]
        biases = [bias_ref[hd, pl.ds(j, 1), :] for j in blocks]
        biases[0] = jnp.where(pair == 0, 0.0, biases[0])
        m = m_sc[row, :]
        m_new = m
        for slot, bias in enumerate(biases):
            m_new = jnp.maximum(m_new, smax_sc[slot, row, :] + bias)
        alpha = jnp.exp2(m - m_new)
        m_sc[row, :] = m_new
        pv = None
        for slot, (j, bias) in enumerate(zip(blocks, biases)):
            p = jnp.exp2(s_sc[slot, hd] - (m_new - bias))
            v_and_ones = jnp.concatenate([vt_ref[j, head_cols(hd), :], ones_rows], axis=0)
            d = jnp.dot(v_and_ones, p.astype(_BF16), preferred_element_type=_F32)
            pv = d if pv is None else pv + d
        acc_sc[hd] = alpha * acc_sc[hd] + pv

    def reset_head(hd):
        m_sc[pl.ds(hd, 1), :] = jnp.full((1, tq), _NEG_INF, _F32)
        acc_sc[hd] = jnp.zeros(acc_sc.shape[1:], _F32)

    qt_cur, qt_next = qt_sc.at[i % 2], qt_sc.at[(i + 1) % 2]

    @pl.when(i == 0)
    def _():
        for hd in range(N_MOBA_HEADS):
            reset_head(hd)
            qt_cur[hd] = q_ref[:, head_cols(hd)].T
            produce_first_pair_head(hd, i, qt_cur)

    def step(pair):
        for hd in range(N_MOBA_HEADS):
            consume_head(hd, pair - 1)
            produce_head(hd, 2 * pair - 1, 0, qt=qt_cur)
            produce_head(hd, 2 * pair, 1, qt=qt_cur)

    def body(it, carry):
        for u in range(MOBA_STEPS_PER_ITERATION):
            step(MOBA_STEPS_PER_ITERATION * it + 1 + u)
        return carry

    n_steps = n_pairs - 1
    n_iterations = n_steps // MOBA_STEPS_PER_ITERATION
    lax.fori_loop(0, n_iterations, body, 0)

    done = MOBA_STEPS_PER_ITERATION * n_iterations
    chunk = MOBA_STEPS_PER_ITERATION // 2
    while chunk >= 1:
        take = ((n_steps - done) // chunk) % 2 == 1

        @pl.when(take)
        def _(done=done, chunk=chunk):
            for u in range(chunk):
                step(done + 1 + u)

        done = done + jnp.where(take, chunk, 0)
        chunk //= 2

    next_own = jnp.minimum(i + 1, n_blocks - 1)

    def tail(n_live):
        for hd in range(N_MOBA_HEADS):
            qt_next[hd] = qn_ref[:, head_cols(hd)].T
        for hd in range(N_MOBA_HEADS):
            consume_head(hd, n_pairs - 1, n_live)
            produce_first_pair_head(hd, next_own, qt_next)
        for hd in range(N_MOBA_HEADS):
            l = acc_sc[hd, HEAD_DIM:HEAD_DIM + 1, :]
            o = (acc_sc[hd, 0:HEAD_DIM, :] / l).T
            o_ref[:, head_cols(hd)] = (o * sga_ref[:, head_cols(hd)].astype(_F32)).astype(_BF16)
            reset_head(hd)

    @pl.when(i % 2 == 1)
    def _():
        tail(2)

    @pl.when(i % 2 == 0)
    def _():
        tail(1)


def _moba(q, k, vt, bias, sga, *, batch, seq):
    N = q.shape[0]
    n_blocks = seq // MOBA_BLOCK
    tq = MOBA_BLOCK
    k3 = k.reshape(batch, seq, MOBA_WIDTH)
    row_block = lambda b, i: (b * n_blocks + i, 0)
    next_row_block = lambda b, i: (b * n_blocks + jnp.minimum(i + 1, n_blocks - 1), 0)
    return pl.pallas_call(
        _moba_kernel,
        grid=(batch, n_blocks),
        in_specs=[
            pl.BlockSpec((tq, MOBA_WIDTH), row_block),
            pl.BlockSpec((tq, MOBA_WIDTH), next_row_block),
            pl.BlockSpec(memory_space=pl.ANY),
            pl.BlockSpec(memory_space=pl.ANY),
            pl.BlockSpec((None, N_MOBA_HEADS, n_blocks, tq), lambda b, i: (b, 0, 0, i)),
            pl.BlockSpec((tq, MOBA_WIDTH), row_block),
        ],
        out_specs=pl.BlockSpec((tq, MOBA_WIDTH), row_block),
        out_shape=jax.ShapeDtypeStruct((N, MOBA_WIDTH), _BF16),
        scratch_shapes=[
            pltpu.VMEM((seq, MOBA_WIDTH), _BF16),
            pltpu.VMEM((n_blocks, MOBA_WIDTH, MOBA_BLOCK), _BF16),
            pltpu.SemaphoreType.DMA((2,)),
            pltpu.SemaphoreType.DMA((2,)),
            pltpu.VMEM((2, N_MOBA_HEADS, HEAD_DIM, tq), _BF16),
            pltpu.VMEM((2, N_MOBA_HEADS, MOBA_BLOCK, tq), _F32),
            pltpu.VMEM((2, N_MOBA_HEADS, tq), _F32),
            pltpu.VMEM((N_MOBA_HEADS, tq), _F32),
            pltpu.VMEM((N_MOBA_HEADS, HEAD_DIM + MOBA_SUM_ROWS, tq), _F32),
        ],
        compiler_params=pltpu.CompilerParams(
            dimension_semantics=("arbitrary", "arbitrary"), vmem_limit_bytes=V7X_VMEM_LIMIT_BYTES),
        name="moba",
    )(q, q, k3, vt, bias, sga)


def _pool_mem_kernel(rest_ref, halo_ref, mk_ref, mv_ref, wp_ref, ps_ref, o_ref, *, tiles_per_batch):
    tm = rest_ref.shape[0]
    tb = lax.rem(pl.program_id(0), tiles_per_batch)
    u0, gp0, qm0, gm0 = 0, POOL_WIDTH, 2 * POOL_WIDTH, 2 * POOL_WIDTH + MEM_WIDTH

    u = rest_ref[:, u0:gp0].astype(_F32)
    halo = jnp.where(tb != 0, halo_ref[...].astype(_F32), 0.0)
    tokens_so_far = (tb * tm + 1 + lax.broadcasted_iota(jnp.int32, (tm, POOL_GROUP_DIM), 0)).astype(_F32)
    for g, w in enumerate(POOL_WINDOWS):
        cols = slice(g * POOL_GROUP_DIM, (g + 1) * POOL_GROUP_DIM)
        u_g = u[:, cols]
        run = jnp.concatenate([halo[:, cols], u_g], axis=0)
        span = 1
        while span < w:
            run = run + pltpu.roll(run, span, axis=0)
            span *= 2
        win = run[POOL_HALO:]
        head_rows = POOL_HALO
        assert max(POOL_WINDOWS) <= head_rows and w & (w - 1) == 0
        cnt = jnp.minimum(tokens_so_far[:head_rows], float(w))
        mean = jnp.concatenate([win[:head_rows] / cnt, win[head_rows:] * (1.0 / w)], axis=0)
        pooled = mean - u_g
        mixed = jnp.dot(pooled.astype(_BF16), wp_ref[g].astype(_BF16), preferred_element_type=_F32)
        gate = rest_ref[:, gp0 + g * POOL_GROUP_DIM:gp0 + (g + 1) * POOL_GROUP_DIM].astype(_F32)
        o_ref[:, cols] = (mixed * ps_ref[:, cols] * gate).astype(_BF16)

    for hd in range(N_MEM_HEADS):
        cols = slice(hd * HEAD_DIM, (hd + 1) * HEAD_DIM)
        qm = rest_ref[:, qm0 + hd * HEAD_DIM:qm0 + (hd + 1) * HEAD_DIM]
        s = _nt_dot(qm, mk_ref[:, cols])
        m = jnp.max(s, axis=-1, keepdims=True)
        e = jnp.exp2(s - m)
        l = jnp.sum(e, axis=-1, keepdims=True)
        o = jnp.dot(e.astype(_BF16), mv_ref[:, cols], preferred_element_type=_F32) / l
        gate = rest_ref[:, gm0 + hd * HEAD_DIM:gm0 + (hd + 1) * HEAD_DIM].astype(_F32)
        o_ref[:, POOL_WIDTH + hd * HEAD_DIM:POOL_WIDTH + (hd + 1) * HEAD_DIM] = (o * gate).astype(_BF16)


def _pool_mem(rest, mk, mv, wp, pool_scale, *, seq):
    N = rest.shape[0]
    tm = POOL_MEM_ROWS
    tiles_per_batch = seq // tm
    halo_blocks_per_tile = tm // POOL_HALO
    M = mk.shape[1]
    kern = functools.partial(_pool_mem_kernel, tiles_per_batch=tiles_per_batch)
    return pl.pallas_call(
        kern,
        grid=(N // tm,),
        in_specs=[
            pl.BlockSpec((tm, _REST_W), lambda t: (t, 0)),
            pl.BlockSpec((POOL_HALO, POOL_WIDTH),
                         lambda t: (jnp.maximum(t * halo_blocks_per_tile - 1, 0), 0)),
            pl.BlockSpec((None, M, MEM_WIDTH), lambda t: (t // tiles_per_batch, 0, 0)),
            pl.BlockSpec((None, M, MEM_WIDTH), lambda t: (t // tiles_per_batch, 0, 0)),
            pl.BlockSpec(wp.shape, lambda t: (0, 0, 0)),
            pl.BlockSpec((1, POOL_WIDTH), lambda t: (0, 0)),
        ],
        out_specs=pl.BlockSpec((tm, POOL_WIDTH + MEM_WIDTH), lambda t: (t, 0)),
        out_shape=jax.ShapeDtypeStruct((N, POOL_WIDTH + MEM_WIDTH), _BF16),
        compiler_params=pltpu.CompilerParams(dimension_semantics=("arbitrary",)),
        name="pool_mem",
    )(rest, rest, mk, mv, wp, pool_scale)


def _out_proj_kernel(ya_ref, ypm_ref, x_ref, w_hbm, fg_ref, o_ref, w_ref, stage_sc, w_sem,
                     *, final_norm):
    @pl.when(pl.program_id(0) == 0)
    def _():
        _load_weight_as_bf16(w_hbm, w_ref, stage_sc, w_sem)

    for r0 in range(0, x_ref.shape[0], OUT_PROJ_SUBROWS):
        rows = slice(r0, r0 + OUT_PROJ_SUBROWS)
        y = jnp.dot(ya_ref[rows, :], w_ref[0:MOBA_WIDTH, :], preferred_element_type=_F32)
        y = y + jnp.dot(ypm_ref[rows, :], w_ref[MOBA_WIDTH:, :], preferred_element_type=_F32)
        r = x_ref[rows, :] + y
        o_ref[rows, :] = _rmsnorm(r, fg_ref[...]) if final_norm else r


def _out_proj(ya, ypm, xf, w, final_g, *, final_norm):
    N, D = xf.shape
    tm = OUT_PROJ_ROWS
    kern = functools.partial(_out_proj_kernel, final_norm=final_norm)
    return pl.pallas_call(
        kern,
        grid=(N // tm,),
        in_specs=[
            pl.BlockSpec((tm, MOBA_WIDTH), lambda t: (t, 0)),
            pl.BlockSpec((tm, POOL_WIDTH + MEM_WIDTH), lambda t: (t, 0)),
            pl.BlockSpec((tm, D), lambda t: (t, 0)),
            pl.BlockSpec(memory_space=pl.ANY),
            pl.BlockSpec((1, D), lambda t: (0, 0)),
        ],
        out_specs=pl.BlockSpec((tm, D), lambda t: (t, 0)),
        out_shape=jax.ShapeDtypeStruct((N, D), _F32),
        scratch_shapes=[
            pltpu.VMEM(w.shape, _BF16),
            pltpu.VMEM((2, w.shape[0], WEIGHT_STAGE_COLS), _F32),
            pltpu.SemaphoreType.DMA((2,)),
        ],
        compiler_params=pltpu.CompilerParams(
            dimension_semantics=("arbitrary",), vmem_limit_bytes=V7X_VMEM_LIMIT_BYTES),
        name="out_proj",
    )(ya, ypm, xf, w, final_g)


def _layer(xf, mem, norm_g, mem_norm_g, w_in, w_mem_kv, w_pool, pool_scale, w_out, final_g,
           *, batch, seq, final_norm):
    mk, mv = _mem_kv(mem, mem_norm_g[None, :], w_mem_kv)
    q, k, vt, sga, rest, bias = _in_proj(xf, norm_g[None, :], w_in, batch=batch, seq=seq)
    ya = _moba(q, k, vt, bias, sga, batch=batch, seq=seq)
    ypm = _pool_mem(rest, mk, mv, w_pool, pool_scale[None, :], seq=seq)
    return _out_proj(ya, ypm, xf, w_out, final_g[None, :], final_norm=final_norm)


def kernel(x, mem, norm_g, mem_norm_g, w_in, w_mem_kv, w_pool, pool_scale, w_out, final_norm_g):
    batch, seq, d_model = x.shape
    depth = norm_g.shape[0]
    assert seq % MOBA_BLOCK == 0 and seq % IN_PROJ_ROWS == 0
    assert seq % POOL_MEM_ROWS == 0 and seq % OUT_PROJ_ROWS == 0
    assert w_in.shape[2] == 4 * MOBA_WIDTH + 2 * POOL_WIDTH + 2 * MEM_WIDTH
    xf = x.reshape(batch * seq, d_model)
    for l in range(depth):
        xf = _layer(xf, mem, norm_g[l], mem_norm_g[l], w_in[l], w_mem_kv[l], w_pool[l],
                    pool_scale[l], w_out[l], final_norm_g,
                    batch=batch, seq=seq, final_norm=(l == depth - 1))
    return xf.reshape(batch, seq, d_model)
```

```python
import functools

import jax
import jax.numpy as jnp
from jax import lax
from jax.experimental import pallas as pl
from jax.experimental.pallas import tpu as pltpu

HEAD_DIM = 128
N_MOBA_HEADS = 8
MOBA_WIDTH = N_MOBA_HEADS * HEAD_DIM
N_MEM_HEADS = 4
MEM_WIDTH = N_MEM_HEADS * HEAD_DIM
POOL_WINDOWS = (2, 4, 8, 16)
POOL_GROUP_DIM = 128
POOL_WIDTH = len(POOL_WINDOWS) * POOL_GROUP_DIM
POOL_HALO = 16
MOBA_BLOCK = 256
MOBA_TOPK = 3
EPS = 1e-6
SCALE = HEAD_DIM ** -0.5
LOG2_E = 1.4426950408889634
MOBA_SUM_ROWS = 16
MOBA_STEPS_PER_ITERATION = 4

_Q0, _K0, _V0, _GA0 = 0, MOBA_WIDTH, 2 * MOBA_WIDTH, 3 * MOBA_WIDTH
_REST0 = 4 * MOBA_WIDTH
_REST_W = 2 * POOL_WIDTH + 2 * MEM_WIDTH

V7X_VMEM_LIMIT_BYTES = 56 * 1024 * 1024

IN_PROJ_ROWS = 512
POOL_MEM_ROWS = 1024
OUT_PROJ_ROWS = 512
OUT_PROJ_SUBROWS = 256
WEIGHT_STAGE_BYTES = 3 * 1024 * 1024

_F32 = jnp.float32
_BF16 = jnp.bfloat16
_NEG_INF = float("-inf")


def _nt_dot(a, b):
    return lax.dot_general(a, b, (((1,), (1,)), ((), ())), preferred_element_type=_F32)


def _split_bf16(a):
    hi = a.astype(_BF16)
    lo = (a - hi.astype(_F32)).astype(_BF16)
    return hi, lo


def _nt_dot_f32(a, b, bt_sc):
    m = a.shape[0]
    a_hi, a_lo = _split_bf16(a)
    b_hi, b_lo = _split_bf16(b)
    bt_sc[0] = b_hi.T
    bt_sc[1] = b_lo.T
    both = jnp.dot(jnp.concatenate([a_hi, a_lo], axis=0), bt_sc[0], preferred_element_type=_F32)
    return both[:m] + (jnp.dot(a_hi, bt_sc[1], preferred_element_type=_F32) + both[m:])


def _silu(a):
    return a * (1.0 / (1.0 + jnp.exp(-a)))


def _rmsnorm(x, g):
    ms = jnp.mean(x * x, axis=-1, keepdims=True)
    return x * lax.rsqrt(ms + EPS) * g


def _weight_stage_rows(n_cols):
    rows = WEIGHT_STAGE_BYTES // (4 * n_cols)
    return 1 << (rows.bit_length() - 1)


def _load_weight_as_bf16(w_hbm, w_sc, stage_sc, sem):
    stage_rows = stage_sc.shape[1]
    n_chunks = w_sc.shape[0] // stage_rows

    def chunk_copy(c):
        rows = slice(c * stage_rows, (c + 1) * stage_rows)
        return pltpu.make_async_copy(w_hbm.at[rows, :], stage_sc.at[c % 2], sem.at[c % 2])

    chunk_copy(0).start()
    for c in range(n_chunks):
        if c + 1 < n_chunks:
            chunk_copy(c + 1).start()
        chunk_copy(c).wait()
        w_sc[c * stage_rows:(c + 1) * stage_rows, :] = stage_sc[c % 2].astype(_BF16)


def _mem_kv_kernel(mem_ref, g_ref, w_ref, mk_ref, mv_ref):
    h = _rmsnorm(mem_ref[...], g_ref[...]).astype(_BF16)
    kv = jnp.dot(h, w_ref[...].astype(_BF16), preferred_element_type=_F32)
    mk_ref[...] = kv[:, :MEM_WIDTH].astype(_BF16)
    mv_ref[...] = kv[:, MEM_WIDTH:].astype(_BF16)


def _mem_kv(mem, g, w):
    B, M, D = mem.shape
    return pl.pallas_call(
        _mem_kv_kernel,
        grid=(B,),
        in_specs=[
            pl.BlockSpec((None, M, D), lambda b: (b, 0, 0)),
            pl.BlockSpec((1, D), lambda b: (0, 0)),
            pl.BlockSpec((D, 2 * MEM_WIDTH), lambda b: (0, 0)),
        ],
        out_specs=[
            pl.BlockSpec((None, M, MEM_WIDTH), lambda b: (b, 0, 0)),
            pl.BlockSpec((None, M, MEM_WIDTH), lambda b: (b, 0, 0)),
        ],
        out_shape=[jax.ShapeDtypeStruct((B, M, MEM_WIDTH), _BF16)] * 2,
        compiler_params=pltpu.CompilerParams(dimension_semantics=("arbitrary",)),
        name="mem_kv",
    )(mem, g, w)


def _in_proj_kernel(x_ref, g_ref, w_hbm,
                    q_ref, k_ref, vt_ref, sga_ref, rest_ref, bias_ref,
                    w_ref, stage_sc, w_sem, kmean_sc, qt_sc, *, tiles_per_batch, n_blocks):
    tm = x_ref.shape[0]
    blocks_per_tile = tm // MOBA_BLOCK
    t = pl.program_id(0)
    first_block = lax.rem(t, tiles_per_batch) * blocks_per_tile

    @pl.when(t == 0)
    def _():
        _load_weight_as_bf16(w_hbm, w_ref, stage_sc, w_sem)
        kmean_sc[...] = jnp.zeros_like(kmean_sc)

    h = _rmsnorm(x_ref[...], g_ref[...]).astype(_BF16)

    def project(col0, width):
        return jnp.dot(h, w_ref[:, col0:col0 + width], preferred_element_type=_F32)

    k = project(_K0, MOBA_WIDTH)
    k_ref[...] = k.astype(_BF16)
    for b in range(blocks_per_tile):
        kmean_sc[pl.ds(first_block + b, 1), :] = jnp.mean(
            k[b * MOBA_BLOCK:(b + 1) * MOBA_BLOCK], axis=0, keepdims=True)

    q = project(_Q0, MOBA_WIDTH)
    q_ref[...] = (q * (SCALE * LOG2_E)).astype(_BF16)

    n_idx = lax.broadcasted_iota(jnp.int32, (n_blocks, tm), 0)
    col = lax.broadcasted_iota(jnp.int32, (n_blocks, tm), 1)
    own_block = first_block + col // MOBA_BLOCK
    is_past = n_idx < own_block

    def gate_head(hd):
        cols = slice(hd * HEAD_DIM, (hd + 1) * HEAD_DIM)
        gate = jnp.where(is_past, _nt_dot_f32(kmean_sc[:, cols], q[:, cols], qt_sc.at[hd % 2]), _NEG_INF)
        keep = None
        for _ in range(MOBA_TOPK):
            best = jnp.max(gate, axis=0, keepdims=True)
            first = jnp.min(jnp.where(gate == best, n_idx, n_blocks), axis=0, keepdims=True)
            pick = n_idx == first
            keep = pick if keep is None else keep | pick
            gate = jnp.where(pick, _NEG_INF, gate)
        bias_ref[hd] = jnp.where(keep & is_past, 0.0, _NEG_INF)

    chunks_per_projection = N_MOBA_HEADS // 2
    chunk_w = MOBA_WIDTH // chunks_per_projection
    for c in range(chunks_per_projection):
        cols = slice(c * chunk_w, (c + 1) * chunk_w)
        v = project(_V0 + c * chunk_w, chunk_w)
        for b in range(blocks_per_tile):
            vt_ref[b, cols, :] = v[b * MOBA_BLOCK:(b + 1) * MOBA_BLOCK].T.astype(_BF16)
        gate_head(c)
    for c in range(chunks_per_projection):
        cols = slice(c * chunk_w, (c + 1) * chunk_w)
        sga_ref[:, cols] = _silu(project(_GA0 + c * chunk_w, chunk_w)).astype(_BF16)
        gate_head(chunks_per_projection + c)

    u0, gp0, qm0, gm0 = 0, POOL_WIDTH, 2 * POOL_WIDTH, 2 * POOL_WIDTH + MEM_WIDTH
    rest_ref[:, gp0:qm0] = _silu(project(_REST0 + gp0, POOL_WIDTH)).astype(_BF16)
    rest_ref[:, gm0:] = _silu(project(_REST0 + gm0, MEM_WIDTH)).astype(_BF16)
    rest_ref[:, qm0:gm0] = (project(_REST0 + qm0, MEM_WIDTH) * (SCALE * LOG2_E)).astype(_BF16)
    rest_ref[:, u0:gp0] = project(_REST0 + u0, POOL_WIDTH).astype(_BF16)


def _in_proj(xf, g, w, *, batch, seq):
    N, D = xf.shape
    tm = IN_PROJ_ROWS
    tiles_per_batch = seq // tm
    blocks_per_tile = tm // MOBA_BLOCK
    n_blocks = seq // MOBA_BLOCK
    kern = functools.partial(_in_proj_kernel, tiles_per_batch=tiles_per_batch, n_blocks=n_blocks)
    return pl.pallas_call(
        kern,
        grid=(N // tm,),
        in_specs=[
            pl.BlockSpec((tm, D), lambda t: (t, 0)),
            pl.BlockSpec((1, D), lambda t: (0, 0)),
            pl.BlockSpec(memory_space=pl.ANY),
        ],
        out_specs=[
            pl.BlockSpec((tm, MOBA_WIDTH), lambda t: (t, 0)),
            pl.BlockSpec((tm, MOBA_WIDTH), lambda t: (t, 0)),
            pl.BlockSpec((None, blocks_per_tile, MOBA_WIDTH, MOBA_BLOCK),
                         lambda t: (t // tiles_per_batch, t % tiles_per_batch, 0, 0)),
            pl.BlockSpec((tm, MOBA_WIDTH), lambda t: (t, 0)),
            pl.BlockSpec((tm, _REST_W), lambda t: (t, 0)),
            pl.BlockSpec((None, N_MOBA_HEADS, n_blocks, tm),
                         lambda t: (t // tiles_per_batch, 0, 0, t % tiles_per_batch)),
        ],
        out_shape=[
            jax.ShapeDtypeStruct((N, MOBA_WIDTH), _BF16),
            jax.ShapeDtypeStruct((N, MOBA_WIDTH), _BF16),
            jax.ShapeDtypeStruct((batch, n_blocks, MOBA_WIDTH, MOBA_BLOCK), _BF16),
            jax.ShapeDtypeStruct((N, MOBA_WIDTH), _BF16),
            jax.ShapeDtypeStruct((N, _REST_W), _BF16),
            jax.ShapeDtypeStruct((batch, N_MOBA_HEADS, n_blocks, seq), _F32),
        ],
        scratch_shapes=[
            pltpu.VMEM(w.shape, _BF16),
            pltpu.VMEM((2, _weight_stage_rows(w.shape[1]), w.shape[1]), _F32),
            pltpu.SemaphoreType.DMA((2,)),
            pltpu.VMEM((n_blocks, MOBA_WIDTH), _F32),
            pltpu.VMEM((2, 2, HEAD_DIM, tm), _BF16),
        ],
        compiler_params=pltpu.CompilerParams(
            dimension_semantics=("arbitrary",), vmem_limit_bytes=V7X_VMEM_LIMIT_BYTES),
        name="in_proj",
    )(xf, g, w)


def _moba_kernel(q_ref, qn_ref, k_hbm, vt_hbm, bias_ref, sga_ref, o_ref,
                 k_ref, vt_ref, k_sem, vt_sem, qt_sc, s_sc, smax_sc, m_sc, acc_sc):
    tq = q_ref.shape[0]
    b = pl.program_id(0)
    i = pl.program_id(1)
    n_blocks = vt_ref.shape[0]
    ones_rows = jnp.ones((MOBA_SUM_ROWS, MOBA_BLOCK), _BF16)

    def head_cols(hd):
        return slice(hd * HEAD_DIM, (hd + 1) * HEAD_DIM)

    def kv_block_copies(j):
        rows = pl.ds(pl.multiple_of(j * MOBA_BLOCK, MOBA_BLOCK), MOBA_BLOCK)
        return (pltpu.make_async_copy(k_hbm.at[b, rows, :], k_ref.at[rows, :], k_sem.at[j % 2]),
                pltpu.make_async_copy(vt_hbm.at[b, j], vt_ref.at[j], vt_sem.at[j % 2]))

    def start_kv_block(j):
        for copy in kv_block_copies(j):
            copy.start()

    def wait_kv_block(j):
        for copy in kv_block_copies(j):
            copy.wait()

    @pl.when(i == 0)
    def _():
        start_kv_block(0)
        start_kv_block(1)
        wait_kv_block(0)
        wait_kv_block(1)

    @pl.when((i > 0) & (i + 1 < n_blocks))
    def _():
        wait_kv_block(i + 1)

    @pl.when(i + 2 < n_blocks)
    def _():
        start_kv_block(i + 2)

    def produce_head(hd, j, slot, mask=lambda s: s, *, qt):
        start = pl.multiple_of(j * MOBA_BLOCK, MOBA_BLOCK)
        s = mask(jnp.dot(k_ref[pl.ds(start, MOBA_BLOCK), head_cols(hd)], qt[hd],
                         preferred_element_type=_F32))
        s_sc[slot, hd] = s
        smax_sc[slot, pl.ds(hd, 1), :] = jnp.max(s, axis=0, keepdims=True)

    kpos = lax.broadcasted_iota(jnp.int32, (MOBA_BLOCK, tq), 0)
    qpos = lax.broadcasted_iota(jnp.int32, (MOBA_BLOCK, tq), 1)
    causal = kpos <= qpos

    def produce_first_pair_head(hd, own_block, qt):
        produce_head(hd, own_block, 0, lambda s: jnp.where(causal, s, _NEG_INF), qt=qt)
        produce_head(hd, 0, 1, qt=qt)

    n_pairs = (i + 2) // 2

    def pair_blocks(pair):
        return jnp.where(pair == 0, i, 2 * pair - 1), 2 * pair

    def consume_head(hd, pair, n_live=2):
        row = pl.ds(hd, 1)
        blocks = pair_blocks(pair)[:n_live]
        biases = [bias_ref[hd, pl.ds(j, 1), :] for j in blocks]
        biases[0] = jnp.where(pair == 0, 0.0, biases[0])
        m = m_sc[row, :]
        m_new = m
        for slot, bias in enumerate(biases):
            m_new = jnp.maximum(m_new, smax_sc[slot, row, :] + bias)
        alpha = jnp.exp2(m - m_new)
        m_sc[row, :] = m_new
        pv = None
        for slot, (j, bias) in enumerate(zip(blocks, biases)):
            p = jnp.exp2(s_sc[slot, hd] - (m_new - bias))
            v_and_ones = jnp.concatenate([vt_ref[j, head_cols(hd), :], ones_rows], axis=0)
            d = jnp.dot(v_and_ones, p.astype(_BF16), preferred_element_type=_F32)
            pv = d if pv is None else pv + d
        acc_sc[hd] = alpha * acc_sc[hd] + pv

    def reset_head(hd):
        m_sc[pl.ds(hd, 1), :] = jnp.full((1, tq), _NEG_INF, _F32)
        acc_sc[hd] = jnp.zeros(acc_sc.shape[1:], _F32)

    qt_cur, qt_next = qt_sc.at[i % 2], qt_sc.at[(i + 1) % 2]

    @pl.when(i == 0)
    def _():
        for hd in range(N_MOBA_HEADS):
            reset_head(hd)
            qt_cur[hd] = q_ref[:, head_cols(hd)].T
            produce_first_pair_head(hd, i, qt_cur)

    def step(pair):
        for hd in range(N_MOBA_HEADS):
            consume_head(hd, pair - 1)
            produce_head(hd, 2 * pair - 1, 0, qt=qt_cur)
            produce_head(hd, 2 * pair, 1, qt=qt_cur)

    def body(it, carry):
        for u in range(MOBA_STEPS_PER_ITERATION):
            step(MOBA_STEPS_PER_ITERATION * it + 1 + u)
        return carry

    n_steps = n_pairs - 1
    n_iterations = n_steps // MOBA_STEPS_PER_ITERATION
    lax.fori_loop(0, n_iterations, body, 0)

    done = MOBA_STEPS_PER_ITERATION * n_iterations
    chunk = MOBA_STEPS_PER_ITERATION // 2
    while chunk >= 1:
        take = ((n_steps - done) // chunk) % 2 == 1

        @pl.when(take)
        def _(done=done, chunk=chunk):
            for u in range(chunk):
                step(done + 1 + u)

        done = done + jnp.where(take, chunk, 0)
        chunk //= 2

    next_own = jnp.minimum(i + 1, n_blocks - 1)

    def tail(n_live):
        for hd in range(N_MOBA_HEADS):
            qt_next[hd] = qn_ref[:, head_cols(hd)].T
        for hd in range(N_MOBA_HEADS):
            consume_head(hd, n_pairs - 1, n_live)
            produce_first_pair_head(hd, next_own, qt_next)
        for hd in range(N_MOBA_HEADS):
            l = acc_sc[hd, HEAD_DIM:HEAD_DIM + 1, :]
            o = (acc_sc[hd, 0:HEAD_DIM, :] / l).T
            o_ref[:, head_cols(hd)] = (o * sga_ref[:, head_cols(hd)].astype(_F32)).astype(_BF16)
            reset_head(hd)

    @pl.when(i % 2 == 1)
    def _():
        tail(2)

    @pl.when(i % 2 == 0)
    def _():
        tail(1)


def _moba(q, k, vt, bias, sga, *, batch, seq):
    N = q.shape[0]
    n_blocks = seq // MOBA_BLOCK
    tq = MOBA_BLOCK
    k3 = k.reshape(batch, seq, MOBA_WIDTH)
    row_block = lambda b, i: (b * n_blocks + i, 0)
    next_row_block = lambda b, i: (b * n_blocks + jnp.minimum(i + 1, n_blocks - 1), 0)
    return pl.pallas_call(
        _moba_kernel,
        grid=(batch, n_blocks),
        in_specs=[
            pl.BlockSpec((tq, MOBA_WIDTH), row_block),
            pl.BlockSpec((tq, MOBA_WIDTH), next_row_block),
            pl.BlockSpec(memory_space=pl.ANY),
            pl.BlockSpec(memory_space=pl.ANY),
            pl.BlockSpec((None, N_MOBA_HEADS, n_blocks, tq), lambda b, i: (b, 0, 0, i)),
            pl.BlockSpec((tq, MOBA_WIDTH), row_block),
        ],
        out_specs=pl.BlockSpec((tq, MOBA_WIDTH), row_block),
        out_shape=jax.ShapeDtypeStruct((N, MOBA_WIDTH), _BF16),
        scratch_shapes=[
            pltpu.VMEM((seq, MOBA_WIDTH), _BF16),
            pltpu.VMEM((n_blocks, MOBA_WIDTH, MOBA_BLOCK), _BF16),
            pltpu.SemaphoreType.DMA((2,)),
            pltpu.SemaphoreType.DMA((2,)),
            pltpu.VMEM((2, N_MOBA_HEADS, HEAD_DIM, tq), _BF16),
            pltpu.VMEM((2, N_MOBA_HEADS, MOBA_BLOCK, tq), _F32),
            pltpu.VMEM((2, N_MOBA_HEADS, tq), _F32),
            pltpu.VMEM((N_MOBA_HEADS, tq), _F32),
            pltpu.VMEM((N_MOBA_HEADS, HEAD_DIM + MOBA_SUM_ROWS, tq), _F32),
        ],
        compiler_params=pltpu.CompilerParams(
            dimension_semantics=("arbitrary", "arbitrary"), vmem_limit_bytes=V7X_VMEM_LIMIT_BYTES),
        name="moba",
    )(q, q, k3, vt, bias, sga)


def _pool_mem_kernel(rest_ref, halo_ref, mk_ref, mv_ref, wp_ref, ps_ref, o_ref, *, tiles_per_batch):
    tm = rest_ref.shape[0]
    tb = lax.rem(pl.program_id(0), tiles_per_batch)
    u0, gp0, qm0, gm0 = 0, POOL_WIDTH, 2 * POOL_WIDTH, 2 * POOL_WIDTH + MEM_WIDTH

    u = rest_ref[:, u0:gp0].astype(_F32)
    halo = jnp.where(tb != 0, halo_ref[...].astype(_F32), 0.0)
    tokens_so_far = (tb * tm + 1 + lax.broadcasted_iota(jnp.int32, (tm, POOL_GROUP_DIM), 0)).astype(_F32)
    for g, w in enumerate(POOL_WINDOWS):
        cols = slice(g * POOL_GROUP_DIM, (g + 1) * POOL_GROUP_DIM)
        u_g = u[:, cols]
        run = jnp.concatenate([halo[:, cols], u_g], axis=0)
        span = 1
        while span < w:
            run = run + pltpu.roll(run, span, axis=0)
            span *= 2
        win = run[POOL_HALO:]
        head_rows = POOL_HALO
        assert max(POOL_WINDOWS) <= head_rows and w & (w - 1) == 0
        cnt = jnp.minimum(tokens_so_far[:head_rows], float(w))
        mean = jnp.concatenate([win[:head_rows] / cnt, win[head_rows:] * (1.0 / w)], axis=0)
        pooled = mean - u_g
        mixed = jnp.dot(pooled.astype(_BF16), wp_ref[g].astype(_BF16), preferred_element_type=_F32)
        gate = rest_ref[:, gp0 + g * POOL_GROUP_DIM:gp0 + (g + 1) * POOL_GROUP_DIM].astype(_F32)
        o_ref[:, cols] = (mixed * ps_ref[:, cols] * gate).astype(_BF16)

    for hd in range(N_MEM_HEADS):
        cols = slice(hd * HEAD_DIM, (hd + 1) * HEAD_DIM)
        qm = rest_ref[:, qm0 + hd * HEAD_DIM:qm0 + (hd + 1) * HEAD_DIM]
        s = _nt_dot(qm, mk_ref[:, cols])
        m = jnp.max(s, axis=-1, keepdims=True)
        e = jnp.exp2(s - m)
        l = jnp.sum(e, axis=-1, keepdims=True)
        o = jnp.dot(e.astype(_BF16), mv_ref[:, cols], preferred_element_type=_F32) / l
        gate = rest_ref[:, gm0 + hd * HEAD_DIM:gm0 + (hd + 1) * HEAD_DIM].astype(_F32)
        o_ref[:, POOL_WIDTH + hd * HEAD_DIM:POOL_WIDTH + (hd + 1) * HEAD_DIM] = (o * gate).astype(_BF16)


def _pool_mem(rest, mk, mv, wp, pool_scale, *, seq):
    N = rest.shape[0]
    tm = POOL_MEM_ROWS
    tiles_per_batch = seq // tm
    halo_blocks_per_tile = tm // POOL_HALO
    M = mk.shape[1]
    kern = functools.partial(_pool_mem_kernel, tiles_per_batch=tiles_per_batch)
    return pl.pallas_call(
        kern,
        grid=(N // tm,),
        in_specs=[
            pl.BlockSpec((tm, _REST_W), lambda t: (t, 0)),
            pl.BlockSpec((POOL_HALO, POOL_WIDTH),
                         lambda t: (jnp.maximum(t * halo_blocks_per_tile - 1, 0), 0)),
            pl.BlockSpec((None, M, MEM_WIDTH), lambda t: (t // tiles_per_batch, 0, 0)),
            pl.BlockSpec((None, M, MEM_WIDTH), lambda t: (t // tiles_per_batch, 0, 0)),
            pl.BlockSpec(wp.shape, lambda t: (0, 0, 0)),
            pl.BlockSpec((1, POOL_WIDTH), lambda t: (0, 0)),
        ],
        out_specs=pl.BlockSpec((tm, POOL_WIDTH + MEM_WIDTH), lambda t: (t, 0)),
        out_shape=jax.ShapeDtypeStruct((N, POOL_WIDTH + MEM_WIDTH), _BF16),
        compiler_params=pltpu.CompilerParams(dimension_semantics=("arbitrary",)),
        name="pool_mem",
    )(rest, rest, mk, mv, wp, pool_scale)


def _out_proj_kernel(ya_ref, ypm_ref, x_ref, w_hbm, fg_ref, o_ref, w_ref, stage_sc, w_sem,
                     *, final_norm):
    @pl.when(pl.program_id(0) == 0)
    def _():
        _load_weight_as_bf16(w_hbm, w_ref, stage_sc, w_sem)

    for r0 in range(0, x_ref.shape[0], OUT_PROJ_SUBROWS):
        rows = slice(r0, r0 + OUT_PROJ_SUBROWS)
        y = jnp.dot(ya_ref[rows, :], w_ref[0:MOBA_WIDTH, :], preferred_element_type=_F32)
        y = y + jnp.dot(ypm_ref[rows, :], w_ref[MOBA_WIDTH:, :], preferred_element_type=_F32)
        r = x_ref[rows, :] + y
        o_ref[rows, :] = _rmsnorm(r, fg_ref[...]) if final_norm else r


def _out_proj(ya, ypm, xf, w, final_g, *, final_norm):
    N, D = xf.shape
    tm = OUT_PROJ_ROWS
    kern = functools.partial(_out_proj_kernel, final_norm=final_norm)
    return pl.pallas_call(
        kern,
        grid=(N // tm,),
        in_specs=[
            pl.BlockSpec((tm, MOBA_WIDTH), lambda t: (t, 0)),
            pl.BlockSpec((tm, POOL_WIDTH + MEM_WIDTH), lambda t: (t, 0)),
            pl.BlockSpec((tm, D), lambda t: (t, 0)),
            pl.BlockSpec(memory_space=pl.ANY),
            pl.BlockSpec((1, D), lambda t: (0, 0)),
        ],
        out_specs=pl.BlockSpec((tm, D), lambda t: (t, 0)),
        out_shape=jax.ShapeDtypeStruct((N, D), _F32),
        scratch_shapes=[
            pltpu.VMEM(w.shape, _BF16),
            pltpu.VMEM((2, _weight_stage_rows(w.shape[1]), w.shape[1]), _F32),
            pltpu.SemaphoreType.DMA((2,)),
        ],
        compiler_params=pltpu.CompilerParams(
            dimension_semantics=("arbitrary",), vmem_limit_bytes=V7X_VMEM_LIMIT_BYTES),
        name="out_proj",
    )(ya, ypm, xf, w, final_g)


def _layer(xf, mem, norm_g, mem_norm_g, w_in, w_mem_kv, w_pool, pool_scale, w_out, final_g,
           *, batch, seq, final_norm):
    mk, mv = _mem_kv(mem, mem_norm_g[None, :], w_mem_kv)
    q, k, vt, sga, rest, bias = _in_proj(xf, norm_g[None, :], w_in, batch=batch, seq=seq)
    ya = _moba(q, k, vt, bias, sga, batch=batch, seq=seq)
    ypm = _pool_mem(rest, mk, mv, w_pool, pool_scale[None, :], seq=seq)
    return _out_proj(ya, ypm, xf, w_out, final_g[None, :], final_norm=final_norm)


def kernel(x, mem, norm_g, mem_norm_g, w_in, w_mem_kv, w_pool, pool_scale, w_out, final_norm_g):
    batch, seq, d_model = x.shape
    depth = norm_g.shape[0]
    assert seq % MOBA_BLOCK == 0 and seq % IN_PROJ_ROWS == 0
    assert seq % POOL_MEM_ROWS == 0 and seq % OUT_PROJ_ROWS == 0
    assert w_in.shape[2] == 4 * MOBA_WIDTH + 2 * POOL_WIDTH + 2 * MEM_WIDTH
    xf = x.reshape(batch * seq, d_model)
    for l in range(depth):
        xf = _layer(xf, mem, norm_g[l], mem_norm_g[l], w_in[l], w_mem_kv[l], w_pool[l],
                    pool_scale[l], w_out[l], final_norm_g,
                    batch=batch, seq=seq, final_norm=(l == depth - 1))
    return xf.reshape(batch, seq, d_model)
```

```python
import functools

import jax
import jax.numpy as jnp
from jax import lax
from jax.experimental import pallas as pl
from jax.experimental.pallas import tpu as pltpu

HEAD_DIM = 128
N_MOBA_HEADS = 8
MOBA_WIDTH = N_MOBA_HEADS * HEAD_DIM
N_MEM_HEADS = 4
MEM_WIDTH = N_MEM_HEADS * HEAD_DIM
POOL_WINDOWS = (2, 4, 8, 16)
POOL_GROUP_DIM = 128
POOL_WIDTH = len(POOL_WINDOWS) * POOL_GROUP_DIM
POOL_HALO = 16
MOBA_BLOCK = 256
MOBA_TOPK = 3
EPS = 1e-6
SCALE = HEAD_DIM ** -0.5
LOG2_E = 1.4426950408889634
MOBA_SUM_ROWS = 16
MOBA_STEPS_PER_ITERATION = 4

_Q0, _K0, _V0, _GA0 = 0, MOBA_WIDTH, 2 * MOBA_WIDTH, 3 * MOBA_WIDTH
_REST0 = 4 * MOBA_WIDTH
_REST_W = 2 * POOL_WIDTH + 2 * MEM_WIDTH

V7X_VMEM_LIMIT_BYTES = 56 * 1024 * 1024

IN_PROJ_ROWS = 512
POOL_MEM_ROWS = 2048
OUT_PROJ_ROWS = 512
OUT_PROJ_SUBROWS = 256
WEIGHT_STAGE_BYTES = 3 * 1024 * 1024

_F32 = jnp.float32
_BF16 = jnp.bfloat16
_NEG_INF = float("-inf")


def _nt_dot(a, b):
    return lax.dot_general(a, b, (((1,), (1,)), ((), ())), preferred_element_type=_F32)


def _split_bf16(a):
    hi = a.astype(_BF16)
    lo = (a - hi.astype(_F32)).astype(_BF16)
    return hi, lo


def _nt_dot_f32(a, b, bt_sc):
    m = a.shape[0]
    a_hi, a_lo = _split_bf16(a)
    b_hi, b_lo = _split_bf16(b)
    bt_sc[0] = b_hi.T
    bt_sc[1] = b_lo.T
    both = jnp.dot(jnp.concatenate([a_hi, a_lo], axis=0), bt_sc[0], preferred_element_type=_F32)
    return both[:m] + (jnp.dot(a_hi, bt_sc[1], preferred_element_type=_F32) + both[m:])


def _silu(a):
    return a * (1.0 / (1.0 + jnp.exp(-a)))


def _rmsnorm(x, g):
    ms = jnp.mean(x * x, axis=-1, keepdims=True)
    return x * lax.rsqrt(ms + EPS) * g


def _weight_stage_rows(n_cols):
    rows = WEIGHT_STAGE_BYTES // (4 * n_cols)
    return 1 << (rows.bit_length() - 1)


def _load_weight_as_bf16(w_hbm, w_sc, stage_sc, sem):
    stage_rows = stage_sc.shape[1]
    n_chunks = w_sc.shape[0] // stage_rows

    def chunk_copy(c):
        rows = slice(c * stage_rows, (c + 1) * stage_rows)
        return pltpu.make_async_copy(w_hbm.at[rows, :], stage_sc.at[c % 2], sem.at[c % 2])

    chunk_copy(0).start(priority=0)
    for c in range(n_chunks):
        if c + 1 < n_chunks:
            chunk_copy(c + 1).start(priority=(c + 1) % 2)
        chunk_copy(c).wait()
        w_sc[c * stage_rows:(c + 1) * stage_rows, :] = stage_sc[c % 2].astype(_BF16)


def _mem_kv_kernel(mem_ref, g_ref, w_ref, mk_ref, mv_ref):
    h = _rmsnorm(mem_ref[...], g_ref[...]).astype(_BF16)
    kv = jnp.dot(h, w_ref[...].astype(_BF16), preferred_element_type=_F32)
    mk_ref[...] = kv[:, :MEM_WIDTH].astype(_BF16)
    mv_ref[...] = kv[:, MEM_WIDTH:].astype(_BF16)


def _mem_kv(mem, g, w):
    B, M, D = mem.shape
    return pl.pallas_call(
        _mem_kv_kernel,
        grid=(B,),
        in_specs=[
            pl.BlockSpec((None, M, D), lambda b: (b, 0, 0)),
            pl.BlockSpec((1, D), lambda b: (0, 0)),
            pl.BlockSpec((D, 2 * MEM_WIDTH), lambda b: (0, 0)),
        ],
        out_specs=[
            pl.BlockSpec((None, M, MEM_WIDTH), lambda b: (b, 0, 0)),
            pl.BlockSpec((None, M, MEM_WIDTH), lambda b: (b, 0, 0)),
        ],
        out_shape=[jax.ShapeDtypeStruct((B, M, MEM_WIDTH), _BF16)] * 2,
        compiler_params=pltpu.CompilerParams(dimension_semantics=("arbitrary",)),
        name="mem_kv",
    )(mem, g, w)


def _in_proj_kernel(x_ref, g_ref, w_hbm,
                    q_ref, k_ref, vt_ref, sga_ref, rest_ref, bias_ref,
                    w_ref, stage_sc, w_sem, kmean_sc, qt_sc, *, tiles_per_batch, n_blocks):
    tm = x_ref.shape[0]
    blocks_per_tile = tm // MOBA_BLOCK
    t = pl.program_id(0)
    first_block = lax.rem(t, tiles_per_batch) * blocks_per_tile

    @pl.when(t == 0)
    def _():
        _load_weight_as_bf16(w_hbm, w_ref, stage_sc, w_sem)
        kmean_sc[...] = jnp.zeros_like(kmean_sc)

    h = _rmsnorm(x_ref[...], g_ref[...]).astype(_BF16)

    def project(col0, width):
        return jnp.dot(h, w_ref[:, col0:col0 + width], preferred_element_type=_F32)

    k = project(_K0, MOBA_WIDTH)
    k_ref[...] = k.astype(_BF16)
    for b in range(blocks_per_tile):
        kmean_sc[pl.ds(first_block + b, 1), :] = jnp.mean(
            k[b * MOBA_BLOCK:(b + 1) * MOBA_BLOCK], axis=0, keepdims=True)

    q = project(_Q0, MOBA_WIDTH)
    q_ref[...] = (q * (SCALE * LOG2_E)).astype(_BF16)

    n_idx = lax.broadcasted_iota(jnp.int32, (n_blocks, tm), 0)
    col = lax.broadcasted_iota(jnp.int32, (n_blocks, tm), 1)
    own_block = first_block + col // MOBA_BLOCK
    is_past = n_idx < own_block

    def gate_head(hd):
        cols = slice(hd * HEAD_DIM, (hd + 1) * HEAD_DIM)
        gate = jnp.where(is_past, _nt_dot_f32(kmean_sc[:, cols], q[:, cols], qt_sc.at[hd % 2]), _NEG_INF)
        keep = None
        for _ in range(MOBA_TOPK):
            best = jnp.max(gate, axis=0, keepdims=True)
            first = jnp.min(jnp.where(gate == best, n_idx, n_blocks), axis=0, keepdims=True)
            pick = n_idx == first
            keep = pick if keep is None else keep | pick
            gate = jnp.where(pick, _NEG_INF, gate)
        bias_ref[hd] = jnp.where(keep & is_past, 0.0, _NEG_INF)

    chunks_per_projection = N_MOBA_HEADS // 2
    chunk_w = MOBA_WIDTH // chunks_per_projection
    for c in range(chunks_per_projection):
        cols = slice(c * chunk_w, (c + 1) * chunk_w)
        v = project(_V0 + c * chunk_w, chunk_w)
        for b in range(blocks_per_tile):
            vt_ref[b, cols, :] = v[b * MOBA_BLOCK:(b + 1) * MOBA_BLOCK].T.astype(_BF16)
        gate_head(c)
    for c in range(chunks_per_projection):
        cols = slice(c * chunk_w, (c + 1) * chunk_w)
        sga_ref[:, cols] = _silu(project(_GA0 + c * chunk_w, chunk_w)).astype(_BF16)
        gate_head(chunks_per_projection + c)

    u0, gp0, qm0, gm0 = 0, POOL_WIDTH, 2 * POOL_WIDTH, 2 * POOL_WIDTH + MEM_WIDTH
    rest_ref[:, gp0:qm0] = _silu(project(_REST0 + gp0, POOL_WIDTH)).astype(_BF16)
    rest_ref[:, gm0:] = _silu(project(_REST0 + gm0, MEM_WIDTH)).astype(_BF16)
    rest_ref[:, qm0:gm0] = (project(_REST0 + qm0, MEM_WIDTH) * (SCALE * LOG2_E)).astype(_BF16)
    rest_ref[:, u0:gp0] = project(_REST0 + u0, POOL_WIDTH).astype(_BF16)


def _in_proj(xf, g, w, *, batch, seq):
    N, D = xf.shape
    tm = IN_PROJ_ROWS
    tiles_per_batch = seq // tm
    blocks_per_tile = tm // MOBA_BLOCK
    n_blocks = seq // MOBA_BLOCK
    kern = functools.partial(_in_proj_kernel, tiles_per_batch=tiles_per_batch, n_blocks=n_blocks)
    return pl.pallas_call(
        kern,
        grid=(N // tm,),
        in_specs=[
            pl.BlockSpec((tm, D), lambda t: (t, 0)),
            pl.BlockSpec((1, D), lambda t: (0, 0)),
            pl.BlockSpec(memory_space=pl.ANY),
        ],
        out_specs=[
            pl.BlockSpec((tm, MOBA_WIDTH), lambda t: (t, 0)),
            pl.BlockSpec((tm, MOBA_WIDTH), lambda t: (t, 0)),
            pl.BlockSpec((None, blocks_per_tile, MOBA_WIDTH, MOBA_BLOCK),
                         lambda t: (t // tiles_per_batch, t % tiles_per_batch, 0, 0)),
            pl.BlockSpec((tm, MOBA_WIDTH), lambda t: (t, 0)),
            pl.BlockSpec((tm, _REST_W), lambda t: (t, 0)),
            pl.BlockSpec((None, N_MOBA_HEADS, n_blocks, tm),
                         lambda t: (t // tiles_per_batch, 0, 0, t % tiles_per_batch)),
        ],
        out_shape=[
            jax.ShapeDtypeStruct((N, MOBA_WIDTH), _BF16),
            jax.ShapeDtypeStruct((N, MOBA_WIDTH), _BF16),
            jax.ShapeDtypeStruct((batch, n_blocks, MOBA_WIDTH, MOBA_BLOCK), _BF16),
            jax.ShapeDtypeStruct((N, MOBA_WIDTH), _BF16),
            jax.ShapeDtypeStruct((N, _REST_W), _BF16),
            jax.ShapeDtypeStruct((batch, N_MOBA_HEADS, n_blocks, seq), _F32),
        ],
        scratch_shapes=[
            pltpu.VMEM(w.shape, _BF16),
            pltpu.VMEM((2, _weight_stage_rows(w.shape[1]), w.shape[1]), _F32),
            pltpu.SemaphoreType.DMA((2,)),
            pltpu.VMEM((n_blocks, MOBA_WIDTH), _F32),
            pltpu.VMEM((2, 2, HEAD_DIM, tm), _BF16),
        ],
        compiler_params=pltpu.CompilerParams(
            dimension_semantics=("arbitrary",), vmem_limit_bytes=V7X_VMEM_LIMIT_BYTES),
        name="in_proj",
    )(xf, g, w)


def _moba_kernel(q_ref, qn_ref, k_hbm, vt_hbm, bias_ref, sga_ref, o_ref,
                 k_ref, vt_ref, k_sem, vt_sem, qt_sc, s_sc, smax_sc, m_sc, acc_sc):
    tq = q_ref.shape[0]
    b = pl.program_id(0)
    i = pl.program_id(1)
    n_blocks = vt_ref.shape[0]
    ones_rows = jnp.ones((MOBA_SUM_ROWS, MOBA_BLOCK), _BF16)

    def head_cols(hd):
        return slice(hd * HEAD_DIM, (hd + 1) * HEAD_DIM)

    def kv_block_copies(j):
        rows = pl.ds(pl.multiple_of(j * MOBA_BLOCK, MOBA_BLOCK), MOBA_BLOCK)
        return (pltpu.make_async_copy(k_hbm.at[b, rows, :], k_ref.at[rows, :], k_sem.at[j % 2]),
                pltpu.make_async_copy(vt_hbm.at[b, j], vt_ref.at[j], vt_sem.at[j % 2]))

    def start_kv_block(j):
        for copy in kv_block_copies(j):
            copy.start()

    def wait_kv_block(j):
        for copy in kv_block_copies(j):
            copy.wait()

    @pl.when(i == 0)
    def _():
        start_kv_block(0)
        start_kv_block(1)
        wait_kv_block(0)
        wait_kv_block(1)

    @pl.when((i > 0) & (i + 1 < n_blocks))
    def _():
        wait_kv_block(i + 1)

    @pl.when(i + 2 < n_blocks)
    def _():
        start_kv_block(i + 2)

    def produce_head(hd, j, slot, mask=lambda s: s, *, qt):
        start = pl.multiple_of(j * MOBA_BLOCK, MOBA_BLOCK)
        s = mask(jnp.dot(k_ref[pl.ds(start, MOBA_BLOCK), head_cols(hd)], qt[hd],
                         preferred_element_type=_F32))
        s_sc[slot, hd] = s
        smax_sc[slot, pl.ds(hd, 1), :] = jnp.max(s, axis=0, keepdims=True)

    kpos = lax.broadcasted_iota(jnp.int32, (MOBA_BLOCK, tq), 0)
    qpos = lax.broadcasted_iota(jnp.int32, (MOBA_BLOCK, tq), 1)
    causal = kpos <= qpos

    def produce_first_pair_head(hd, own_block, qt):
        produce_head(hd, own_block, 0, lambda s: jnp.where(causal, s, _NEG_INF), qt=qt)
        produce_head(hd, 0, 1, qt=qt)

    n_pairs = (i + 2) // 2

    def pair_blocks(pair):
        return jnp.where(pair == 0, i, 2 * pair - 1), 2 * pair

    def consume_head(hd, pair, n_live=2):
        row = pl.ds(hd, 1)
        blocks = pair_blocks(pair)[:n_live]
        biases = [bias_ref[hd, pl.ds(j, 1), :] for j in blocks]
        biases[0] = jnp.where(pair == 0, 0.0, biases[0])
        m = m_sc[row, :]
        m_new = m
        for slot, bias in enumerate(biases):
            m_new = jnp.maximum(m_new, smax_sc[slot, row, :] + bias)
        alpha = jnp.exp2(m - m_new)
        m_sc[row, :] = m_new
        pv = None
        for slot, (j, bias) in enumerate(zip(blocks, biases)):
            p = jnp.exp2(s_sc[slot, hd] - (m_new - bias))
            v_and_ones = jnp.concatenate([vt_ref[j, head_cols(hd), :], ones_rows], axis=0)
            d = jnp.dot(v_and_ones, p.astype(_BF16), preferred_element_type=_F32)
            pv = d if pv is None else pv + d
        acc_sc[hd] = alpha * acc_sc[hd] + pv

    def reset_head(hd):
        m_sc[pl.ds(hd, 1), :] = jnp.full((1, tq), _NEG_INF, _F32)
        acc_sc[hd] = jnp.zeros(acc_sc.shape[1:], _F32)

    qt_cur, qt_next = qt_sc.at[i % 2], qt_sc.at[(i + 1) % 2]

    @pl.when(i == 0)
    def _():
        for hd in range(N_MOBA_HEADS):
            reset_head(hd)
            qt_cur[hd] = q_ref[:, head_cols(hd)].T
            produce_first_pair_head(hd, i, qt_cur)

    def step(pair):
        for hd in range(N_MOBA_HEADS):
            consume_head(hd, pair - 1)
            produce_head(hd, 2 * pair - 1, 0, qt=qt_cur)
            produce_head(hd, 2 * pair, 1, qt=qt_cur)

    def body(it, carry):
        for u in range(MOBA_STEPS_PER_ITERATION):
            step(MOBA_STEPS_PER_ITERATION * it + 1 + u)
        return carry

    n_steps = n_pairs - 1
    n_iterations = n_steps // MOBA_STEPS_PER_ITERATION
    lax.fori_loop(0, n_iterations, body, 0)

    done = MOBA_STEPS_PER_ITERATION * n_iterations
    chunk = MOBA_STEPS_PER_ITERATION // 2
    while chunk >= 1:
        take = ((n_steps - done) // chunk) % 2 == 1

        @pl.when(take)
        def _(done=done, chunk=chunk):
            for u in range(chunk):
                step(done + 1 + u)

        done = done + jnp.where(take, chunk, 0)
        chunk //= 2

    next_own = jnp.minimum(i + 1, n_blocks - 1)

    def tail(n_live):
        for hd in range(N_MOBA_HEADS):
            qt_next[hd] = qn_ref[:, head_cols(hd)].T
        for hd in range(N_MOBA_HEADS):
            consume_head(hd, n_pairs - 1, n_live)
            produce_first_pair_head(hd, next_own, qt_next)
        for hd in range(N_MOBA_HEADS):
            l = acc_sc[hd, HEAD_DIM:HEAD_DIM + 1, :]
            o = (acc_sc[hd, 0:HEAD_DIM, :] / l).T
            o_ref[:, head_cols(hd)] = (o * sga_ref[:, head_cols(hd)].astype(_F32)).astype(_BF16)
            reset_head(hd)

    @pl.when(i % 2 == 1)
    def _():
        tail(2)

    @pl.when(i % 2 == 0)
    def _():
        tail(1)


def _moba(q, k, vt, bias, sga, *, batch, seq):
    N = q.shape[0]
    n_blocks = seq // MOBA_BLOCK
    tq = MOBA_BLOCK
    k3 = k.reshape(batch, seq, MOBA_WIDTH)
    row_block = lambda b, i: (b * n_blocks + i, 0)
    next_row_block = lambda b, i: (b * n_blocks + jnp.minimum(i + 1, n_blocks - 1), 0)
    return pl.pallas_call(
        _moba_kernel,
        grid=(batch, n_blocks),
        in_specs=[
            pl.BlockSpec((tq, MOBA_WIDTH), row_block),
            pl.BlockSpec((tq, MOBA_WIDTH), next_row_block),
            pl.BlockSpec(memory_space=pl.ANY),
            pl.BlockSpec(memory_space=pl.ANY),
            pl.BlockSpec((None, N_MOBA_HEADS, n_blocks, tq), lambda b, i: (b, 0, 0, i)),
            pl.BlockSpec((tq, MOBA_WIDTH), row_block),
        ],
        out_specs=pl.BlockSpec((tq, MOBA_WIDTH), row_block),
        out_shape=jax.ShapeDtypeStruct((N, MOBA_WIDTH), _BF16),
        scratch_shapes=[
            pltpu.VMEM((seq, MOBA_WIDTH), _BF16),
            pltpu.VMEM((n_blocks, MOBA_WIDTH, MOBA_BLOCK), _BF16),
            pltpu.SemaphoreType.DMA((2,)),
            pltpu.SemaphoreType.DMA((2,)),
            pltpu.VMEM((2, N_MOBA_HEADS, HEAD_DIM, tq), _BF16),
            pltpu.VMEM((2, N_MOBA_HEADS, MOBA_BLOCK, tq), _F32),
            pltpu.VMEM((2, N_MOBA_HEADS, tq), _F32),
            pltpu.VMEM((N_MOBA_HEADS, tq), _F32),
            pltpu.VMEM((N_MOBA_HEADS, HEAD_DIM + MOBA_SUM_ROWS, tq), _F32),
        ],
        compiler_params=pltpu.CompilerParams(
            dimension_semantics=("arbitrary", "arbitrary"), vmem_limit_bytes=V7X_VMEM_LIMIT_BYTES),
        name="moba",
    )(q, q, k3, vt, bias, sga)


def _pool_mem_kernel(rest_ref, halo_ref, mk_ref, mv_ref, wp_ref, ps_ref, o_ref, *, tiles_per_batch):
    tm = rest_ref.shape[0]
    tb = lax.rem(pl.program_id(0), tiles_per_batch)
    u0, gp0, qm0, gm0 = 0, POOL_WIDTH, 2 * POOL_WIDTH, 2 * POOL_WIDTH + MEM_WIDTH

    u = rest_ref[:, u0:gp0].astype(_F32)
    halo = jnp.where(tb != 0, halo_ref[...].astype(_F32), 0.0)
    tokens_so_far = (tb * tm + 1 + lax.broadcasted_iota(jnp.int32, (tm, POOL_GROUP_DIM), 0)).astype(_F32)
    for g, w in enumerate(POOL_WINDOWS):
        cols = slice(g * POOL_GROUP_DIM, (g + 1) * POOL_GROUP_DIM)
        u_g = u[:, cols]
        run = jnp.concatenate([halo[:, cols], u_g], axis=0)
        span = 1
        while span < w:
            run = run + pltpu.roll(run, span, axis=0)
            span *= 2
        win = run[POOL_HALO:]
        head_rows = POOL_HALO
        assert max(POOL_WINDOWS) <= head_rows and w & (w - 1) == 0
        cnt = jnp.minimum(tokens_so_far[:head_rows], float(w))
        mean = jnp.concatenate([win[:head_rows] / cnt, win[head_rows:] * (1.0 / w)], axis=0)
        pooled = mean - u_g
        mixed = jnp.dot(pooled.astype(_BF16), wp_ref[g].astype(_BF16), preferred_element_type=_F32)
        gate = rest_ref[:, gp0 + g * POOL_GROUP_DIM:gp0 + (g + 1) * POOL_GROUP_DIM].astype(_F32)
        o_ref[:, cols] = (mixed * ps_ref[:, cols] * gate).astype(_BF16)

    for hd in range(N_MEM_HEADS):
        cols = slice(hd * HEAD_DIM, (hd + 1) * HEAD_DIM)
        qm = rest_ref[:, qm0 + hd * HEAD_DIM:qm0 + (hd + 1) * HEAD_DIM]
        s = _nt_dot(qm, mk_ref[:, cols])
        m = jnp.max(s, axis=-1, keepdims=True)
        e = jnp.exp2(s - m)
        l = jnp.sum(e, axis=-1, keepdims=True)
        o = jnp.dot(e.astype(_BF16), mv_ref[:, cols], preferred_element_type=_F32) / l
        gate = rest_ref[:, gm0 + hd * HEAD_DIM:gm0 + (hd + 1) * HEAD_DIM].astype(_F32)
        o_ref[:, POOL_WIDTH + hd * HEAD_DIM:POOL_WIDTH + (hd + 1) * HEAD_DIM] = (o * gate).astype(_BF16)


def _pool_mem(rest, mk, mv, wp, pool_scale, *, seq):
    N = rest.shape[0]
    tm = POOL_MEM_ROWS
    tiles_per_batch = seq // tm
    halo_blocks_per_tile = tm // POOL_HALO
    M = mk.shape[1]
    kern = functools.partial(_pool_mem_kernel, tiles_per_batch=tiles_per_batch)
    return pl.pallas_call(
        kern,
        grid=(N // tm,),
        in_specs=[
            pl.BlockSpec((tm, _REST_W), lambda t: (t, 0)),
            pl.BlockSpec((POOL_HALO, POOL_WIDTH),
                         lambda t: (jnp.maximum(t * halo_blocks_per_tile - 1, 0), 0)),
            pl.BlockSpec((None, M, MEM_WIDTH), lambda t: (t // tiles_per_batch, 0, 0)),
            pl.BlockSpec((None, M, MEM_WIDTH), lambda t: (t // tiles_per_batch, 0, 0)),
            pl.BlockSpec(wp.shape, lambda t: (0, 0, 0)),
            pl.BlockSpec((1, POOL_WIDTH), lambda t: (0, 0)),
        ],
        out_specs=pl.BlockSpec((tm, POOL_WIDTH + MEM_WIDTH), lambda t: (t, 0)),
        out_shape=jax.ShapeDtypeStruct((N, POOL_WIDTH + MEM_WIDTH), _BF16),
        compiler_params=pltpu.CompilerParams(dimension_semantics=("arbitrary",)),
        name="pool_mem",
    )(rest, rest, mk, mv, wp, pool_scale)


def _out_proj_kernel(ya_ref, ypm_ref, x_ref, w_hbm, fg_ref, o_ref, w_ref, stage_sc, w_sem,
                     *, final_norm):
    @pl.when(pl.program_id(0) == 0)
    def _():
        _load_weight_as_bf16(w_hbm, w_ref, stage_sc, w_sem)

    for r0 in range(0, x_ref.shape[0], OUT_PROJ_SUBROWS):
        rows = slice(r0, r0 + OUT_PROJ_SUBROWS)
        y = jnp.dot(ya_ref[rows, :], w_ref[0:MOBA_WIDTH, :], preferred_element_type=_F32)
        y = y + jnp.dot(ypm_ref[rows, :], w_ref[MOBA_WIDTH:, :], preferred_element_type=_F32)
        r = x_ref[rows, :] + y
        o_ref[rows, :] = _rmsnorm(r, fg_ref[...]) if final_norm else r


def _out_proj(ya, ypm, xf, w, final_g, *, final_norm):
    N, D = xf.shape
    tm = OUT_PROJ_ROWS
    kern = functools.partial(_out_proj_kernel, final_norm=final_norm)
    return pl.pallas_call(
        kern,
        grid=(N // tm,),
        in_specs=[
            pl.BlockSpec((tm, MOBA_WIDTH), lambda t: (t, 0)),
            pl.BlockSpec((tm, POOL_WIDTH + MEM_WIDTH), lambda t: (t, 0)),
            pl.BlockSpec((tm, D), lambda t: (t, 0)),
            pl.BlockSpec(memory_space=pl.ANY),
            pl.BlockSpec((1, D), lambda t: (0, 0)),
        ],
        out_specs=pl.BlockSpec((tm, D), lambda t: (t, 0)),
        out_shape=jax.ShapeDtypeStruct((N, D), _F32),
        scratch_shapes=[
            pltpu.VMEM(w.shape, _BF16),
            pltpu.VMEM((2, _weight_stage_rows(w.shape[1]), w.shape[1]), _F32),
            pltpu.SemaphoreType.DMA((2,)),
        ],
        compiler_params=pltpu.CompilerParams(
            dimension_semantics=("arbitrary",), vmem_limit_bytes=V7X_VMEM_LIMIT_BYTES),
        name="out_proj",
    )(ya, ypm, xf, w, final_g)


def _layer(xf, mem, norm_g, mem_norm_g, w_in, w_mem_kv, w_pool, pool_scale, w_out, final_g,
           *, batch, seq, final_norm):
    mk, mv = _mem_kv(mem, mem_norm_g[None, :], w_mem_kv)
    q, k, vt, sga, rest, bias = _in_proj(xf, norm_g[None, :], w_in, batch=batch, seq=seq)
    ya = _moba(q, k, vt, bias, sga, batch=batch, seq=seq)
    ypm = _pool_mem(rest, mk, mv, w_pool, pool_scale[None, :], seq=seq)
    return _out_proj(ya, ypm, xf, w_out, final_g[None, :], final_norm=final_norm)


def kernel(x, mem, norm_g, mem_norm_g, w_in, w_mem_kv, w_pool, pool_scale, w_out, final_norm_g):
    batch, seq, d_model = x.shape
    depth = norm_g.shape[0]
    assert seq % MOBA_BLOCK == 0 and seq % IN_PROJ_ROWS == 0
    assert seq % POOL_MEM_ROWS == 0 and seq % OUT_PROJ_ROWS == 0
    assert w_in.shape[2] == 4 * MOBA_WIDTH + 2 * POOL_WIDTH + 2 * MEM_WIDTH
    xf = x.reshape(batch * seq, d_model)
    for l in range(depth):
        xf = _layer(xf, mem, norm_g[l], mem_norm_g[l], w_in[l], w_mem_kv[l], w_pool[l],
                    pool_scale[l], w_out[l], final_norm_g,
                    batch=batch, seq=seq, final_norm=(l == depth - 1))
    return xf.reshape(batch, seq, d_model)
```
